```python
import jax
import jax.numpy as jnp
from jax import lax
import numpy as np

D_MODEL = 4096
BATCH = 1
SEQ = 8192
DEPTH = 2

GRID_W = 64
CTX_LEN = 256
HEAD_DIM = 128
MIX_HEADS = D_MODEL // (2 * HEAD_DIM)
KV_HEADS = MIX_HEADS // 4
MIX_WIDTH = 2 * MIX_HEADS * HEAD_DIM
NA_WIN_R = 8
NA_WIN_C = 16
Q_BLOCK = 128
SW_WINDOW = 128
RET_CHUNK = 128
ROPE_THETA = 10000.0
PEER_HEADS = 8
PEER_NKEYS = 128
PEER_EXPERTS = PEER_NKEYS * PEER_NKEYS
PEER_QDIM = 256
PEER_TOPK = 16
PEER_BLOCK = 64
DEEPNORM_ALPHA = (2 * DEPTH) ** 0.25
DEEPNORM_BETA = (8 * DEPTH) ** -0.25
LN_EPS = 1e-6
NEG_INF = -1e30
N_EVEN = (DEPTH + 1) // 2
N_ODD = DEPTH // 2
HD = HEAD_DIM
EV_WIDTHS = (MIX_HEADS * HD, MIX_HEADS * HD, MIX_HEADS * HD, MIX_HEADS * HD, KV_HEADS * HD, KV_HEADS * HD)
OD_WIDTHS = (MIX_HEADS * HD, MIX_HEADS * HD, MIX_HEADS * HD, MIX_HEADS * HD, MIX_HEADS * HD, KV_HEADS * HD, KV_HEADS * HD)
EV_IN = sum(EV_WIDTHS)
OD_IN = sum(OD_WIDTHS)

kernel_name = 'hybrid_natten_gqa_retnet_swa_peer_dit'

F32 = jnp.float32


def layer_norm(h, g, b):
    hf = h.astype(F32)
    mu = jnp.mean(hf, -1, keepdims=True)
    var = jnp.mean(jnp.square(hf - mu), -1, keepdims=True)
    return ((hf - mu) * lax.rsqrt(var + LN_EPS) * g + b).astype(h.dtype)


def rms_norm(h, g):
    hf = h.astype(F32)
    return (hf * lax.rsqrt(jnp.mean(hf * hf, -1, keepdims=True) + LN_EPS) * g).astype(h.dtype)


def head_group_norm(o):
    mu = jnp.mean(o, -1, keepdims=True)
    var = jnp.mean(jnp.square(o - mu), -1, keepdims=True)
    return (o - mu) * lax.rsqrt(var + LN_EPS)


def modulate(h, shift, scale):
    return h * (1.0 + scale) + shift


def axial_rope_tables(n_tokens):
    t = jnp.arange(n_tokens, dtype=jnp.int32)
    row = (t // GRID_W).astype(F32)
    col = (t % GRID_W).astype(F32)
    n_freq = HEAD_DIM // 4
    inv = ROPE_THETA ** (-jnp.arange(n_freq, dtype=F32) / n_freq)
    ang = jnp.concatenate([row[:, None] * inv, col[:, None] * inv], -1)
    return jnp.cos(ang), jnp.sin(ang)


def apply_rope(h, cos, sin):
    pairs = h.astype(F32).reshape(h.shape[:-1] + (h.shape[-1] // 2, 2))
    c = cos[None, :, None, :]
    s = sin[None, :, None, :]
    x1, x2 = pairs[..., 0], pairs[..., 1]
    return jnp.stack([x1 * c - x2 * s, x1 * s + x2 * c], -1).reshape(h.shape).astype(h.dtype)


def split_heads(p, widths):
    cuts = np.cumsum(widths)[:-1].tolist()
    parts = jnp.split(p, cuts, axis=-1)
    return [t.reshape(t.shape[:-1] + (t.shape[-1] // HEAD_DIM, HEAD_DIM)) for t in parts]


def gqa_attend(q, k, v, sink=None):
    B, T, Hq, d = q.shape
    Hkv = k.shape[2]
    G = Hq // Hkv
    qg = q.reshape(B, T, Hkv, G, d)
    s = jnp.einsum('bqhgd,bkhd->bhgqk', qg, k).astype(F32) * (d ** -0.5)
    if sink is not None:
        s_sink = jnp.broadcast_to(sink.reshape(Hkv, G)[None, :, :, None, None].astype(F32), s.shape[:-1] + (1,))
        p = jax.nn.softmax(jnp.concatenate([s, s_sink], -1), -1)[..., :-1]
    else:
        p = jax.nn.softmax(s, -1)
    out = jnp.einsum('bhgqk,bkhd->bqhgd', p.astype(v.dtype), v)
    return out.reshape(B, T, Hq * d)


def dense_gqa_blocks(q, k, v, kc, vc):
    B, S, Hq, d = q.shape
    keys = jnp.concatenate([k, kc], 1)
    vals = jnp.concatenate([v, vc], 1)
    qb = q.reshape(B, S // Q_BLOCK, Q_BLOCK, Hq, d).transpose(1, 0, 2, 3, 4)
    out = lax.map(lambda qi: gqa_attend(qi, keys, vals), qb)
    return out.transpose(1, 0, 2, 3).reshape(B, S, Hq * d)


def neighbourhood_attention(q, k, v, kc, vc, rpb):
    B, S, H, d = q.shape
    rows = S // GRID_W
    kr = min(NA_WIN_R, rows)
    scale = d ** -0.5
    qg = q.reshape(B, rows, GRID_W, H, d)
    kg = k.reshape(B, rows, GRID_W, H, d)
    vg = v.reshape(B, rows, GRID_W, H, d)
    col = np.arange(GRID_W)
    col_idx = np.clip(col - NA_WIN_C // 2, 0, GRID_W - NA_WIN_C)[:, None] + np.arange(NA_WIN_C)[None, :]
    col_bias_idx = col_idx - col[:, None] + (NA_WIN_C - 1)
    n_lat = kr * NA_WIN_C

    def row_block(r):
        r0 = jnp.clip(r - kr // 2, 0, rows - kr)
        k_sel = jnp.take(lax.dynamic_slice_in_dim(kg, r0, kr, axis=1), col_idx, axis=2)
        v_sel = jnp.take(lax.dynamic_slice_in_dim(vg, r0, kr, axis=1), col_idx, axis=2)
        q_r = lax.dynamic_index_in_dim(qg, r, axis=1, keepdims=False)
        row_bias_idx = r0 + jnp.arange(kr) - r + (NA_WIN_R - 1)
        bias = rpb[:, row_bias_idx[:, None, None], col_bias_idx[None, :, :]].transpose(0, 2, 1, 3)
        s_lat = jnp.einsum('bqhd,brqchd->bhqrc', q_r, k_sel).astype(F32) * scale + bias.astype(F32)[None]
        s_lat = s_lat.reshape(B, H, GRID_W, n_lat)
        s_ctx = jnp.einsum('bqhd,bkhd->bhqk', q_r, kc).astype(F32) * scale
        p = jax.nn.softmax(jnp.concatenate([s_lat, s_ctx], -1), -1).astype(v.dtype)
        p_lat = p[..., :n_lat].reshape(B, H, GRID_W, kr, NA_WIN_C)
        return (jnp.einsum('bhqrc,brqchd->bqhd', p_lat, v_sel)
                + jnp.einsum('bhqk,bkhd->bqhd', p[..., n_lat:], vc))

    out = lax.map(row_block, jnp.arange(rows, dtype=jnp.int32))
    return out.transpose(1, 0, 2, 3, 4).reshape(B, S, H * d)


def sliding_window_gqa(q, k, v, kc, vc, sink):
    B, S, Hq, d = q.shape
    Hkv = k.shape[2]
    G = Hq // Hkv
    nb = S // Q_BLOCK
    pad = ((0, 0), (Q_BLOCK, Q_BLOCK), (0, 0), (0, 0))
    kp = jnp.pad(k, pad).reshape(B, nb + 2, Q_BLOCK, Hkv, d)
    vp = jnp.pad(v, pad).reshape(B, nb + 2, Q_BLOCK, Hkv, d)
    k_band = jnp.concatenate([kp[:, :-2], kp[:, 1:-1], kp[:, 2:]], axis=2)
    v_band = jnp.concatenate([vp[:, :-2], vp[:, 1:-1], vp[:, 2:]], axis=2)
    qb = q.reshape(B, nb, Q_BLOCK, Hkv, G, d)
    scale = d ** -0.5
    s_lat = jnp.einsum('bnqhgd,bnkhd->bnhgqk', qb, k_band).astype(F32) * scale
    qpos = jnp.arange(nb)[:, None, None] * Q_BLOCK + jnp.arange(Q_BLOCK)[None, :, None]
    kpos = jnp.arange(nb)[:, None, None] * Q_BLOCK - Q_BLOCK + jnp.arange(3 * Q_BLOCK)[None, None, :]
    mask = (jnp.abs(kpos - qpos) <= SW_WINDOW) & (kpos >= 0) & (kpos < S)
    s_lat = jnp.where(mask[None, :, None, None], s_lat, NEG_INF)
    s_ctx = jnp.einsum('bnqhgd,bkhd->bnhgqk', qb, kc).astype(F32) * scale
    s_sink = jnp.broadcast_to(sink.reshape(Hkv, G)[None, None, :, :, None, None].astype(F32), s_lat.shape[:-1] + (1,))
    p = jax.nn.softmax(jnp.concatenate([s_lat, s_ctx, s_sink], -1), -1)
    n_band = 3 * Q_BLOCK
    L = kc.shape[1]
    p_lat = p[..., :n_band].astype(v.dtype)
    p_ctx = p[..., n_band:n_band + L].astype(v.dtype)
    out = (jnp.einsum('bnhgqk,bnkhd->bnqhgd', p_lat, v_band)
           + jnp.einsum('bnhgqk,bkhd->bnqhgd', p_ctx, vc))
    return out.reshape(B, S, Hq * d)


def retention_scan(q, k, v, log_gamma, s0):
    B, S, H, dk = q.shape
    dv = v.shape[-1]
    C = RET_CHUNK
    nc = S // C
    pos = jnp.arange(C, dtype=F32)
    diff = pos[:, None] - pos[None, :]
    intra = jnp.where(diff >= 0, jnp.exp(log_gamma[:, None, None] * jnp.maximum(diff, 0.0)), 0.0)
    q_decay = jnp.exp(log_gamma[:, None] * (pos + 1.0))[None, :, :, None]
    k_decay = jnp.exp(log_gamma[:, None] * (C - 1.0 - pos))[None, :, :, None]
    chunk_decay = jnp.exp(log_gamma * C)[None, :, None, None]

    def to_chunks(t):
        return t.reshape(B, nc, C, H, t.shape[-1]).transpose(1, 0, 3, 2, 4)

    def step(state, chunk):
        qi, ki, vi = chunk
        inner = jnp.einsum('bhqd,bhkd->bhqk', qi, ki) * intra
        o = (jnp.einsum('bhqk,bhkv->bhqv', inner, vi)
             + jnp.einsum('bhqd,bhdv->bhqv', qi * q_decay, state))
        state = state * chunk_decay + jnp.einsum('bhkd,bhkv->bhdv', ki * k_decay, vi)
        return state, o

    _, o = lax.scan(step, s0, (to_chunks(q), to_chunks(k), to_chunks(v)))
    return o.transpose(1, 0, 3, 2, 4).reshape(B, S, H, dv)


def context_states(k, v, log_gamma):
    L = k.shape[1]
    j = jnp.arange(L, dtype=F32)
    w_f = jnp.exp(log_gamma[0][:, None] * (L - 1.0 - j))
    w_b = jnp.exp(log_gamma[1][:, None] * j)
    kf = k.astype(F32)
    vf = v.astype(F32)
    s_f = jnp.einsum('bjhd,hj,bjhv->bhdv', kf, w_f, vf)
    s_b = jnp.einsum('bjhd,hj,bjhv->bhdv', kf, w_b, vf)
    return s_f, s_b


def bidirectional_retention(q, k, v, gate, log_gamma, s_fwd, s_bwd):
    flip = lambda t: jnp.flip(t, 1)
    o = (retention_scan(q, k, v, log_gamma[0], s_fwd)
         + flip(retention_scan(flip(q), flip(k), flip(v), log_gamma[1], s_bwd)))
    B, T, H, dv = o.shape
    return (jax.nn.silu(gate.astype(F32)) * head_group_norm(o)).astype(v.dtype).reshape(B, T, H * dv)


def even_mixer(h, hc, w_in, w_out, rpb, q_gain, k_gain, cos, sin, need_ctx):
    S = h.shape[1]
    p = jnp.concatenate([h, hc], 1) @ w_in
    na_q, na_k, na_v, ga_q, ga_k, ga_v = split_heads(p, EV_WIDTHS)
    ga_q = rms_norm(ga_q, q_gain)
    ga_k = rms_norm(ga_k, k_gain)
    y_na = neighbourhood_attention(na_q[:, :S], na_k[:, :S], na_v[:, :S], na_k[:, S:], na_v[:, S:], rpb)
    y_ga = dense_gqa_blocks(apply_rope(ga_q[:, :S], cos, sin), apply_rope(ga_k[:, :S], cos, sin),
                            ga_v[:, :S], ga_k[:, S:], ga_v[:, S:])
    y = jnp.concatenate([y_na, y_ga], -1)
    if need_ctx:
        yc = jnp.concatenate([gqa_attend(na_q[:, S:], na_k[:, S:], na_v[:, S:]),
                              gqa_attend(ga_q[:, S:], ga_k[:, S:], ga_v[:, S:])], -1)
        out = jnp.concatenate([y, yc], 1) @ w_out
        return out[:, :S], out[:, S:]
    return y @ w_out, None


def odd_mixer(h, hc, w_in, w_out, decay_exp, sink, cos, sin, need_ctx):
    B, S, _ = h.shape
    p = jnp.concatenate([h, hc], 1) @ w_in
    r_q, r_k, r_v, r_g, s_q, s_k, s_v = split_heads(p, OD_WIDTHS)
    k_scale = HEAD_DIM ** -0.5
    log_gamma = jnp.log1p(-jnp.exp2(-decay_exp.astype(F32)))
    rkc = r_k[:, S:] * k_scale
    rvc = r_v[:, S:]
    s_fwd, s_bwd = context_states(rkc, rvc, log_gamma)
    y_ret = bidirectional_retention(apply_rope(r_q[:, :S], cos, sin), apply_rope(r_k[:, :S], cos, sin) * k_scale,
                                    r_v[:, :S], r_g[:, :S], log_gamma, s_fwd, s_bwd)
    y_sw = sliding_window_gqa(apply_rope(s_q[:, :S], cos, sin), apply_rope(s_k[:, :S], cos, sin),
                              s_v[:, :S], s_k[:, S:], s_v[:, S:], sink)
    y = jnp.concatenate([y_ret, y_sw], -1)
    if need_ctx:
        zero = jnp.zeros((B, MIX_HEADS, HEAD_DIM, HEAD_DIM), F32)
        yc_ret = bidirectional_retention(r_q[:, S:], rkc, rvc, r_g[:, S:], log_gamma, zero, zero)
        yc_sw = gqa_attend(s_q[:, S:], s_k[:, S:], s_v[:, S:], sink)
        out = jnp.concatenate([y, jnp.concatenate([yc_ret, yc_sw], -1)], 1) @ w_out
        return out[:, :S], out[:, S:]
    return y @ w_out, None


def peer(h, w_q, sub_keys, u_tab, v_tab):
    B, T, D = h.shape
    P = PEER_BLOCK
    half = PEER_QDIM // 2
    K = PEER_TOPK

    def block(xb):
        q = (xb @ w_q).reshape(P, PEER_HEADS, 2, half)
        s = jnp.einsum('phcd,hcnd->phcn', q, sub_keys).astype(F32)
        s_top, i_top = lax.top_k(s, K)
        cand = (s_top[:, :, 0, :, None] + s_top[:, :, 1, None, :]).reshape(P, PEER_HEADS, K * K)
        cand_idx = (i_top[:, :, 0, :, None] * PEER_NKEYS + i_top[:, :, 1, None, :]).reshape(P, PEER_HEADS, K * K)
        best, pos = lax.top_k(cand, K)
        idx = jnp.take_along_axis(cand_idx, pos, -1)
        gates = jax.nn.softmax(best, -1)
        u = jnp.take(u_tab, idx, axis=0)
        act = jax.nn.gelu(jnp.einsum('phkd,pd->phk', u, xb).astype(F32))
        v = jnp.take(v_tab, idx, axis=0)
        return jnp.einsum('phk,phkd->pd', (gates * act).astype(v.dtype), v)

    out = lax.map(block, h.reshape(-1, P, D))
    return out.reshape(B, T, D)


def setup_inputs(seed: int = 0) -> dict:
    key = jax.random.key(seed)
    ks = jax.random.split(key, 24)
    D = D_MODEL

    def nrm(k, shape, scale):
        return jax.random.normal(k, shape, F32) * scale

    decay_base = 5.0 + jnp.arange(MIX_HEADS, dtype=F32)
    return {
        'x': nrm(ks[0], (BATCH, SEQ, D), 1.0),
        'c': nrm(ks[1], (BATCH, D), 1.0),
        'ctx': nrm(ks[2], (BATCH, CTX_LEN, D), 1.0),
        'c_ctx': nrm(ks[3], (D,), 1.0),
        'ada_w': nrm(ks[4], (DEPTH, D, 6 * D), D ** -0.5),
        'ada_b': nrm(ks[5], (DEPTH, 6 * D), 0.02),
        'ln_g': 1.0 + nrm(ks[6], (DEPTH, 2, D), 0.02),
        'ln_b': nrm(ks[7], (DEPTH, 2, D), 0.02),
        'ev_w_in': nrm(ks[8], (N_EVEN, D, EV_IN), D ** -0.5),
        'ev_w_out': nrm(ks[9], (N_EVEN, MIX_WIDTH, D), DEEPNORM_BETA * MIX_WIDTH ** -0.5),
        'ev_na_rpb': nrm(ks[10], (N_EVEN, MIX_HEADS, 2 * NA_WIN_R - 1, 2 * NA_WIN_C - 1), 0.1),
        'ev_ga_q_gain': 1.0 + nrm(ks[11], (N_EVEN, HEAD_DIM), 0.02),
        'ev_ga_k_gain': 1.0 + nrm(ks[12], (N_EVEN, HEAD_DIM), 0.02),
        'od_w_in': nrm(ks[13], (N_ODD, D, OD_IN), D ** -0.5),
        'od_w_out': nrm(ks[14], (N_ODD, MIX_WIDTH, D), DEEPNORM_BETA * MIX_WIDTH ** -0.5),
        'od_ret_decay_exp': decay_base[None, None, :] + nrm(ks[15], (N_ODD, 2, MIX_HEADS), 0.1),
        'od_sw_sink': nrm(ks[16], (N_ODD, MIX_HEADS), 0.5),
        'peer_w_q': nrm(ks[17], (DEPTH, D, PEER_HEADS * PEER_QDIM), D ** -0.5),
        'peer_sub_keys': nrm(ks[18], (DEPTH, PEER_HEADS, 2, PEER_NKEYS, PEER_QDIM // 2), (PEER_QDIM // 2) ** -0.5),
        'peer_u': nrm(ks[19], (DEPTH, PEER_EXPERTS, D), D ** -0.5),
        'peer_v': nrm(ks[20], (DEPTH, PEER_EXPERTS, D), DEEPNORM_BETA),
    }


def reference(x, c, ctx, c_ctx, ada_w, ada_b, ln_g, ln_b, ev_w_in, ev_w_out, ev_na_rpb, ev_ga_q_gain,
              ev_ga_k_gain, od_w_in, od_w_out, od_ret_decay_exp, od_sw_sink, peer_w_q, peer_sub_keys,
              peer_u, peer_v):
    S = x.shape[1]
    cos, sin = axial_rope_tables(S)
    for layer in range(DEPTH):
        need_ctx = layer < DEPTH - 1
        mod = jnp.einsum('bd,de->be', jax.nn.silu(c), ada_w[layer]) + ada_b[layer]
        mod_c = jax.nn.silu(c_ctx) @ ada_w[layer] + ada_b[layer]
        sh_a, sc_a, g_a, sh_f, sc_f, g_f = [m[:, None, :] for m in jnp.split(mod, 6, -1)]
        csh_a, csc_a, cg_a, csh_f, csc_f, cg_f = jnp.split(mod_c, 6, -1)
        h = modulate(x, sh_a, sc_a)
        hc = modulate(ctx, csh_a, csc_a)
        if layer % 2 == 0:
            i = layer // 2
            y, yc = even_mixer(h, hc, ev_w_in[i], ev_w_out[i], ev_na_rpb[i], ev_ga_q_gain[i], ev_ga_k_gain[i],
                               cos, sin, need_ctx)
        else:
            i = layer // 2
            y, yc = odd_mixer(h, hc, od_w_in[i], od_w_out[i], od_ret_decay_exp[i], od_sw_sink[i],
                              cos, sin, need_ctx)
        x = layer_norm(DEEPNORM_ALPHA * x + g_a * y, ln_g[layer, 0], ln_b[layer, 0])
        if need_ctx:
            ctx = layer_norm(DEEPNORM_ALPHA * ctx + cg_a * yc, ln_g[layer, 0], ln_b[layer, 0])
            hf = jnp.concatenate([modulate(x, sh_f, sc_f), modulate(ctx, csh_f, csc_f)], 1)
        else:
            hf = modulate(x, sh_f, sc_f)
        f = peer(hf, peer_w_q[layer], peer_sub_keys[layer], peer_u[layer], peer_v[layer])
        x = layer_norm(DEEPNORM_ALPHA * x + g_f * f[:, :S], ln_g[layer, 1], ln_b[layer, 1])
        if need_ctx:
            ctx = layer_norm(DEEPNORM_ALPHA * ctx + cg_f * f[:, S:], ln_g[layer, 1], ln_b[layer, 1])
    return x
```

```python
import functools
import math

import numpy as np
import jax
import jax.numpy as jnp
from jax import lax
from jax.experimental import pallas as pl
from jax.experimental.pallas import tpu as pltpu

F32 = jnp.float32
BF16 = jnp.bfloat16

GRID_W = 64
HEAD_DIM = 128
NA_WIN_R = 8
NA_WIN_C = 16
NA_ROWS_PER_STEP = 4
SW_WINDOW = 128
SW_BLOCK = 256
RET_CHUNK = 128
ROPE_THETA = 10000.0
PEER_TOPK = 16
LN_EPS = 1e-6
NEG_INF = -1e30
VMEM_LIMIT_BYTES = 56 * 1024 * 1024

_NT = (((1,), (1,)), ((), ()))
_TN = (((0,), (0,)), ((), ()))


def _params(*sem):
    return pltpu.CompilerParams(dimension_semantics=sem, vmem_limit_bytes=VMEM_LIMIT_BYTES)


def _row_chunks(n, size=128):
    return [slice(r, min(r + size, n)) for r in range(0, n, size)]


def _pick(n, cap, mult=128):
    best = None
    for t in range(mult, min(n, cap) + 1, mult):
        if n % t == 0:
            best = t
    assert best is not None, (n, cap, mult)
    return best


def _ada_kernel(c_ref, w_ref, b_ref, o_ref):
    c = c_ref[...]
    s = c * (1.0 / (1.0 + jnp.exp(-c)))
    o_ref[0] = jnp.dot(s.astype(BF16), w_ref[0].astype(BF16), preferred_element_type=F32) + b_ref[0]


def _ada_modulation(c, c_ctx, ada_w, ada_b):
    depth, d, n = ada_w.shape
    cc = jnp.zeros((8, d), F32).at[0].set(c[0]).at[1].set(c_ctx)
    tn = _pick(n, 512)
    out = pl.pallas_call(
        _ada_kernel,
        grid=(depth, n // tn),
        in_specs=[pl.BlockSpec((8, d), lambda l, j: (0, 0)),
                  pl.BlockSpec((1, d, tn), lambda l, j: (l, 0, j)),
                  pl.BlockSpec((1, 1, tn), lambda l, j: (l, 0, j))],
        out_specs=pl.BlockSpec((1, 8, tn), lambda l, j: (l, 0, j)),
        out_shape=jax.ShapeDtypeStruct((depth, 8, n), F32),
        compiler_params=_params("parallel", "parallel"),
    )(cc, ada_w, ada_b.reshape(depth, 1, n))
    return out


def _proj_kernel(x_ref, sh_ref, sc_ref, w_ref, o_ref, *rest, emit_h):
    h_ref = rest[-1]

    @pl.when(pl.program_id(1) == 0)
    def _():
        for rows in _row_chunks(x_ref.shape[0]):
            h = (x_ref[rows, :] * (1.0 + sc_ref[...]) + sh_ref[...]).astype(BF16)
            h_ref[rows, :] = h
            if emit_h:
                rest[0][rows, :] = h

    o_ref[...] = jnp.dot(h_ref[...], w_ref[...], preferred_element_type=F32).astype(o_ref.dtype)


def _proj(x, shift, scale, w, out_dtype, tm, emit_h=False):
    m, d = x.shape
    n = w.shape[1]
    tm = min(tm, m)
    tn = _pick(n, 1024)
    out_shape = [jax.ShapeDtypeStruct((m, n), out_dtype)]
    out_specs = [pl.BlockSpec((tm, tn), lambda i, j: (i, j))]
    if emit_h:
        out_shape.append(jax.ShapeDtypeStruct((m, d), BF16))
        out_specs.append(pl.BlockSpec((tm, d), lambda i, j: (i, 0)))
    res = pl.pallas_call(
        functools.partial(_proj_kernel, emit_h=emit_h),
        grid=(m // tm, n // tn),
        in_specs=[pl.BlockSpec((tm, d), lambda i, j: (i, 0)),
                  pl.BlockSpec((1, d), lambda i, j: (0, 0)),
                  pl.BlockSpec((1, d), lambda i, j: (0, 0)),
                  pl.BlockSpec((d, tn), lambda i, j: (0, j))],
        out_specs=out_specs,
        out_shape=out_shape,
        scratch_shapes=[pltpu.VMEM((tm, d), BF16)],
        compiler_params=_params("parallel", "arbitrary"),
    )(x, shift, scale, w)
    return res if emit_h else res[0]


def _prep_kernel(x_ref, g_ref, cos_ref, sin_ref, o_ref, *, heads, norm, rope, scale):
    for hh in range(heads):
        cols = slice(hh * HEAD_DIM, (hh + 1) * HEAD_DIM)
        x = x_ref[:, cols].astype(F32)
        if norm:
            x = x * lax.rsqrt(jnp.mean(x * x, -1, keepdims=True) + LN_EPS) * g_ref[...]
        if rope:
            lane = lax.broadcasted_iota(jnp.int32, x.shape, 1)
            partner = jnp.where(lane % 2 == 0, pltpu.roll(x, HEAD_DIM - 1, 1), pltpu.roll(x, 1, 1))
            x = x * cos_ref[...] + partner * sin_ref[...]
        if scale != 1.0:
            x = x * scale
        o_ref[:, cols] = x.astype(o_ref.dtype)


def _prep(p, col0, n_heads, gain, cos_i, sin_s, *, norm=False, rope=False, scale=1.0):
    t = p.shape[0]
    hb = 4 if n_heads % 4 == 0 else 1
    bw = hb * HEAD_DIM
    assert col0 % bw == 0
    tt = min(t, 1024)
    c0 = col0 // bw
    return pl.pallas_call(
        functools.partial(_prep_kernel, heads=hb, norm=norm, rope=rope, scale=scale),
        grid=(t // tt, n_heads // hb),
        in_specs=[pl.BlockSpec((tt, bw), lambda i, j: (i, c0 + j)),
                  pl.BlockSpec((1, HEAD_DIM), lambda i, j: (0, 0)),
                  pl.BlockSpec((tt, HEAD_DIM), lambda i, j: (i, 0)),
                  pl.BlockSpec((tt, HEAD_DIM), lambda i, j: (i, 0))],
        out_specs=pl.BlockSpec((tt, bw), lambda i, j: (i, j)),
        out_shape=jax.ShapeDtypeStruct((t, n_heads * HEAD_DIM), BF16),
        compiler_params=_params("parallel", "parallel"),
    )(p, gain.reshape(1, HEAD_DIM).astype(F32), cos_i[:t], sin_s[:t])


def _rope_tables(n_tokens):
    t = jnp.arange(n_tokens, dtype=jnp.int32)
    row = (t // GRID_W).astype(F32)
    col = (t % GRID_W).astype(F32)
    n_freq = HEAD_DIM // 4
    inv = ROPE_THETA ** (-jnp.arange(n_freq, dtype=F32) / n_freq)
    ang = jnp.concatenate([row[:, None] * inv, col[:, None] * inv], -1)
    cos, sin = jnp.cos(ang), jnp.sin(ang)
    cos_i = jnp.repeat(cos, 2, axis=-1)
    sin_s = jnp.stack([-sin, sin], -1).reshape(n_tokens, HEAD_DIM)
    return cos_i, sin_s


def _flash_kernel(q_ref, k_ref, v_ref, o_ref, qs_ref, m_ref, l_ref, acc_ref, *, group, tq, scale):
    j = pl.program_id(2)

    @pl.when(j == 0)
    def _():
        for g in range(group):
            q = q_ref[:, g * HEAD_DIM:(g + 1) * HEAD_DIM]
            if scale != 1.0:
                q = (q.astype(F32) * scale).astype(BF16)
            qs_ref[g * tq:(g + 1) * tq, :] = q
        m_ref[...] = jnp.full_like(m_ref, NEG_INF)
        l_ref[...] = jnp.zeros_like(l_ref)
        acc_ref[...] = jnp.zeros_like(acc_ref)

    s = lax.dot_general(qs_ref[...], k_ref[...], _NT, preferred_element_type=F32)
    m_prev = m_ref[...]
    m_new = jnp.maximum(m_prev, jnp.max(s, -1, keepdims=True))
    alpha = jnp.exp(m_prev - m_new)
    p = jnp.exp(s - m_new)
    l_ref[...] = alpha * l_ref[...] + jnp.sum(p, -1, keepdims=True)
    acc_ref[...] = alpha * acc_ref[...] + jnp.dot(p.astype(BF16), v_ref[...], preferred_element_type=F32)
    m_ref[...] = m_new

    @pl.when(j == pl.num_programs(2) - 1)
    def _():
        out = acc_ref[...] / l_ref[...]
        for g in range(group):
            o_ref[:, g * HEAD_DIM:(g + 1) * HEAD_DIM] = out[g * tq:(g + 1) * tq].astype(o_ref.dtype)


def _flash(q, q_col0, k, k_col0, v, v_col0, *, kv_heads, group, tq, tk, scale=1.0):
    n_q, n_k = q.shape[0], k.shape[0]
    tq, tk = min(tq, n_q), min(tk, n_k)
    assert n_q % tq == 0 and n_k % tk == 0
    qw = group * HEAD_DIM
    assert q_col0 % qw == 0 and k_col0 % HEAD_DIM == 0 and v_col0 % HEAD_DIM == 0
    qc, kc, vc = q_col0 // qw, k_col0 // HEAD_DIM, v_col0 // HEAD_DIM
    return pl.pallas_call(
        functools.partial(_flash_kernel, group=group, tq=tq, scale=scale),
        grid=(kv_heads, n_q // tq, n_k // tk),
        in_specs=[pl.BlockSpec((tq, qw), lambda h, i, j: (i, qc + h)),
                  pl.BlockSpec((tk, HEAD_DIM), lambda h, i, j: (j, kc + h)),
                  pl.BlockSpec((tk, HEAD_DIM), lambda h, i, j: (j, vc + h))],
        out_specs=pl.BlockSpec((tq, qw), lambda h, i, j: (i, h)),
        out_shape=jax.ShapeDtypeStruct((n_q, kv_heads * qw), BF16),
        scratch_shapes=[pltpu.VMEM((group * tq, HEAD_DIM), BF16),
                        pltpu.VMEM((group * tq, 1), F32),
                        pltpu.VMEM((group * tq, 1), F32),
                        pltpu.VMEM((group * tq, HEAD_DIM), F32)],
        compiler_params=_params("parallel", "parallel", "arbitrary"),
    )(q, k, v)


def _na_bias_tables(rpb, rows):
    n_heads = rpb.shape[0]
    rg, w, kr, kc = NA_ROWS_PER_STEP, GRID_W, NA_WIN_R, NA_WIN_C
    n_steps = rows // rg
    assert rows % rg == 0 and n_steps >= 3 and kr == 2 * rg and rows >= kr
    c = np.arange(w)
    dc = np.clip(c[None, :] - c[:, None] + (kc - 1), 0, 2 * kc - 2)
    oh_c = (np.arange(2 * kc - 1)[:, None, None] == dc[None]).astype(np.float32)
    rl, krl = np.arange(rg), np.arange(3 * rg)
    da = krl[None, :] - rl[:, None] - rg + (kr - 1)
    assert da.min() >= 0 and da.max() <= 2 * kr - 2
    oh_a = (np.arange(2 * kr - 1)[:, None, None] == da[None]).astype(np.float32)
    t1 = jnp.einsum('hab,bck->hack', rpb.astype(F32), oh_c, precision=lax.Precision.HIGHEST)
    bias = jnp.einsum('hack,arl->hrclk', t1, oh_a, precision=lax.Precision.HIGHEST)
    bias = bias.reshape(n_heads, rg * w, 3 * rg * w)
    c0 = np.clip(c - kc // 2, 0, w - kc)
    col_ok = (c[None, :] >= c0[:, None]) & (c[None, :] < c0[:, None] + kc)
    masks = []
    for g in (0, 1, n_steps - 1):
        r = rg * g + rl
        r0 = np.clip(r - kr // 2, 0, rows - kr)
        key_row = rg * (g - 1) + krl
        row_ok = (key_row[None, :] >= r0[:, None]) & (key_row[None, :] < r0[:, None] + kr)
        ok = row_ok[:, None, :, None] & col_ok[None, :, None, :]
        masks.append(ok.reshape(rg * w, 3 * rg * w))
    masks = np.stack(masks)
    return jnp.where(masks[:, None], bias[None], NEG_INF)


def _na_kernel(q_ref, kp_ref, kc_ref, kn_ref, vp_ref, vc_ref, vn_ref, kx_ref, vx_ref, b_ref, o_ref, *, heads, scale):
    tq = q_ref.shape[0]
    for hh in range(heads):
        cols = slice(hh * HEAD_DIM, (hh + 1) * HEAD_DIM)
        q = (q_ref[:, cols].astype(F32) * scale).astype(BF16)
        ss = []
        for idx, k_ref in enumerate((kp_ref, kc_ref, kn_ref)):
            s = lax.dot_general(q, k_ref[:, cols], _NT, preferred_element_type=F32)
            ss.append(s + b_ref[0, hh, :, idx * tq:(idx + 1) * tq])
        ss.append(lax.dot_general(q, kx_ref[:, cols], _NT, preferred_element_type=F32))
        m = functools.reduce(jnp.maximum, [jnp.max(s, -1, keepdims=True) for s in ss])
        ps = [jnp.exp(s - m) for s in ss]
        l = functools.reduce(jnp.add, [jnp.sum(p, -1, keepdims=True) for p in ps])
        acc = None
        for p, v_ref in zip(ps, (vp_ref, vc_ref, vn_ref, vx_ref)):
            pv = jnp.dot(p.astype(BF16), v_ref[:, cols], preferred_element_type=F32)
            acc = pv if acc is None else acc + pv
        o_ref[:, cols] = (acc / l).astype(o_ref.dtype)


def _neighbourhood_attention(p, p_ctx, bias, n_heads, q_col0, k_col0, v_col0):
    s_len = p.shape[0]
    tq = NA_ROWS_PER_STEP * GRID_W
    n_steps = s_len // tq
    hb = 4 if n_heads % 4 == 0 else 1
    bw = hb * HEAD_DIM
    assert p_ctx.shape[0] == tq
    qc, kc, vc = q_col0 // bw, k_col0 // bw, v_col0 // bw
    prev = lambda g: jnp.maximum(g - 1, 0)
    nxt = lambda g: jnp.minimum(g + 1, n_steps - 1)
    variant = lambda g: jnp.where(g == 0, 0, jnp.where(g == n_steps - 1, 2, 1))
    lat = lambda col, row: pl.BlockSpec((tq, bw), lambda h, g: (row(g), col + h))
    ident = lambda g: g
    return pl.pallas_call(
        functools.partial(_na_kernel, heads=hb, scale=HEAD_DIM ** -0.5),
        grid=(n_heads // hb, n_steps),
        in_specs=[lat(qc, ident),
                  lat(kc, prev), lat(kc, ident), lat(kc, nxt),
                  lat(vc, prev), lat(vc, ident), lat(vc, nxt),
                  pl.BlockSpec((tq, bw), lambda h, g: (0, kc + h)),
                  pl.BlockSpec((tq, bw), lambda h, g: (0, vc + h)),
                  pl.BlockSpec((1, hb, tq, 3 * tq), lambda h, g: (variant(g), h, 0, 0))],
        out_specs=pl.BlockSpec((tq, bw), lambda h, g: (g, h)),
        out_shape=jax.ShapeDtypeStruct((s_len, n_heads * HEAD_DIM), BF16),
        compiler_params=_params("parallel", "arbitrary"),
    )(p, p, p, p, p, p, p, p_ctx, p_ctx, bias)


def _swa_kernel(sink_ref, q_ref, kp_ref, kc_ref, kn_ref, vp_ref, vc_ref, vn_ref, kx_ref, vx_ref, o_ref, *, group, window):
    h, n = pl.program_id(0), pl.program_id(1)
    tq = q_ref.shape[0]
    q = jnp.concatenate([q_ref[:, g * HEAD_DIM:(g + 1) * HEAD_DIM] for g in range(group)], 0)
    ql = lax.broadcasted_iota(jnp.int32, (group * tq, tq), 0) % tq
    kl = lax.broadcasted_iota(jnp.int32, (group * tq, tq), 1)
    diff = kl - ql
    ok_p = (diff - tq >= -window) & (n > 0)
    ok_c = (diff <= window) & (diff >= -window)
    ok_n = (diff + tq <= window) & (n < pl.num_programs(1) - 1)
    ss = []
    for k_ref, ok in ((kp_ref, ok_p), (kc_ref, ok_c), (kn_ref, ok_n)):
        s = lax.dot_general(q, k_ref[...], _NT, preferred_element_type=F32)
        ss.append(jnp.where(ok, s, NEG_INF))
    ss.append(lax.dot_general(q, kx_ref[...], _NT, preferred_element_type=F32))
    row_g = lax.broadcasted_iota(jnp.int32, (group * tq, 1), 0) // tq
    sink = jnp.zeros((group * tq, 1), F32)
    for g in range(group):
        sink = jnp.where(row_g == g, sink_ref[h * group + g], sink)
    m = functools.reduce(jnp.maximum, [jnp.max(s, -1, keepdims=True) for s in ss] + [sink])
    ps = [jnp.exp(s - m) for s in ss]
    l = functools.reduce(jnp.add, [jnp.sum(p, -1, keepdims=True) for p in ps]) + jnp.exp(sink - m)
    acc = None
    for p, v_ref in zip(ps, (vp_ref, vc_ref, vn_ref, vx_ref)):
        pv = jnp.dot(p.astype(BF16), v_ref[...], preferred_element_type=F32)
        acc = pv if acc is None else acc + pv
    out = acc / l
    for g in range(group):
        o_ref[:, g * HEAD_DIM:(g + 1) * HEAD_DIM] = out[g * tq:(g + 1) * tq].astype(o_ref.dtype)


def _sliding_window_attention(q, k, v, v_col0, kx, kx_col0, vx, vx_col0, sink, *, kv_heads, group):
    s_len = q.shape[0]
    tq = SW_BLOCK
    assert s_len % tq == 0 and tq >= SW_WINDOW and kx.shape[0] == vx.shape[0]
    n_blk = s_len // tq
    n_ctx = kx.shape[0]
    qw = group * HEAD_DIM
    vc, kxc, vxc = v_col0 // HEAD_DIM, kx_col0 // HEAD_DIM, vx_col0 // HEAD_DIM
    prev = lambda n: jnp.maximum(n - 1, 0)
    nxt = lambda n: jnp.minimum(n + 1, n_blk - 1)
    ident = lambda n: n
    blk = lambda col, row: pl.BlockSpec((tq, HEAD_DIM), lambda h, n: (row(n), col + h))
    return pl.pallas_call(
        functools.partial(_swa_kernel, group=group, window=SW_WINDOW),
        grid=(kv_heads, n_blk),
        in_specs=[pl.BlockSpec(memory_space=pltpu.SMEM),
                  pl.BlockSpec((tq, qw), lambda h, n: (n, h)),
                  blk(0, prev), blk(0, ident), blk(0, nxt),
                  blk(vc, prev), blk(vc, ident), blk(vc, nxt),
                  pl.BlockSpec((n_ctx, HEAD_DIM), lambda h, n: (0, kxc + h)),
                  pl.BlockSpec((n_ctx, HEAD_DIM), lambda h, n: (0, vxc + h))],
        out_specs=pl.BlockSpec((tq, qw), lambda h, n: (n, h)),
        out_shape=jax.ShapeDtypeStruct((s_len, kv_heads * qw), BF16),
        compiler_params=_params("parallel", "arbitrary"),
    )(sink.astype(F32), q, k, k, k, v, v, v, kx, vx)


def _ret_kernel(cd_ref, qf_ref, kf_ref, vf_ref, qb_ref, kb_ref, vb_ref, kx_ref, vx_ref,
                intra_ref, qd_ref, kd_ref, wx_ref, of_ref, ob_ref, st_ref, *, heads):
    hb, n = pl.program_id(0), pl.program_id(1)
    for hh in range(heads):
        cols = slice(hh * HEAD_DIM, (hh + 1) * HEAD_DIM)
        head = hb * heads + hh
        for d, (q_ref, k_ref, v_ref, o_ref) in enumerate(((qf_ref, kf_ref, vf_ref, of_ref),
                                                          (qb_ref, kb_ref, vb_ref, ob_ref))):
            @pl.when(n == 0)
            def _():
                kw = (kx_ref[:, cols].astype(F32) * wx_ref[d, hh]).astype(BF16)
                st_ref[d, hh] = lax.dot_general(kw, vx_ref[:, cols], _TN, preferred_element_type=F32)

            q, k, v = q_ref[:, cols], k_ref[:, cols], v_ref[:, cols]
            state = st_ref[d, hh]
            inner = lax.dot_general(q, k, _NT, preferred_element_type=F32) * intra_ref[d, hh]
            qs = (q.astype(F32) * qd_ref[d, hh]).astype(BF16)
            o_ref[:, cols] = (jnp.dot(inner.astype(BF16), v, preferred_element_type=F32)
                              + jnp.dot(qs, state.astype(BF16), preferred_element_type=F32))
            ks = (k.astype(F32) * kd_ref[d, hh]).astype(BF16)
            st_ref[d, hh] = state * cd_ref[d, head] + lax.dot_general(ks, v, _TN, preferred_element_type=F32)


def _retention(rq, rk, p, v_col0, rkx, p_ctx, log_gamma, n_heads):
    s_len = rq.shape[0]
    c = RET_CHUNK
    n_chunks = s_len // c
    n_ctx = rkx.shape[0]
    hb = 4 if n_heads % 4 == 0 else 1
    bw = hb * HEAD_DIM
    vc = v_col0 // bw
    lg = log_gamma.astype(F32)
    pos = jnp.arange(c, dtype=F32)
    diff = pos[:, None] - pos[None, :]
    intra_f = jnp.where(diff >= 0, jnp.exp(lg[0][:, None, None] * jnp.maximum(diff, 0.0)), 0.0)
    intra_b = jnp.where(diff <= 0, jnp.exp(lg[1][:, None, None] * jnp.maximum(-diff, 0.0)), 0.0)
    intra = jnp.stack([intra_f, intra_b])
    lanes = lambda t: jnp.broadcast_to(t[..., None], t.shape + (HEAD_DIM,))
    qd = lanes(jnp.stack([jnp.exp(lg[0][:, None] * (pos + 1.0)), jnp.exp(lg[1][:, None] * (c - pos))]))
    kd = lanes(jnp.stack([jnp.exp(lg[0][:, None] * (c - 1.0 - pos)), jnp.exp(lg[1][:, None] * pos)]))
    cd = jnp.exp(lg * c)
    jx = jnp.arange(n_ctx, dtype=F32)
    wx = lanes(jnp.stack([jnp.exp(lg[0][:, None] * (n_ctx - 1.0 - jx)), jnp.exp(lg[1][:, None] * jx)]))
    fwd = lambda col: pl.BlockSpec((c, bw), lambda h, n: (n, col + h))
    bwd = lambda col: pl.BlockSpec((c, bw), lambda h, n: (n_chunks - 1 - n, col + h))
    tab = lambda rows: pl.BlockSpec((2, hb, rows, HEAD_DIM), lambda h, n: (0, h, 0, 0))
    out_sds = jax.ShapeDtypeStruct((s_len, n_heads * HEAD_DIM), F32)
    return pl.pallas_call(
        functools.partial(_ret_kernel, heads=hb),
        grid=(n_heads // hb, n_chunks),
        in_specs=[pl.BlockSpec(memory_space=pltpu.SMEM),
                  fwd(0), fwd(0), fwd(vc), bwd(0), bwd(0), bwd(vc),
                  pl.BlockSpec((n_ctx, bw), lambda h, n: (0, h)),
                  pl.BlockSpec((n_ctx, bw), lambda h, n: (0, vc + h)),
                  tab(c), tab(c), tab(c), tab(n_ctx)],
        out_specs=[fwd(0), bwd(0)],
        out_shape=[out_sds, out_sds],
        scratch_shapes=[pltpu.VMEM((2, hb, HEAD_DIM, HEAD_DIM), F32)],
        compiler_params=_params("parallel", "arbitrary"),
    )(cd, rq, rk, p, rq, rk, p, rkx, p_ctx, intra, qd, kd, wx)


def _ret_out_kernel(of_ref, ob_ref, g_ref, o_ref, *, heads):
    for hh in range(heads):
        cols = slice(hh * HEAD_DIM, (hh + 1) * HEAD_DIM)
        o = of_ref[:, cols] + ob_ref[:, cols]
        mu = jnp.mean(o, -1, keepdims=True)
        var = jnp.mean(jnp.square(o - mu), -1, keepdims=True)
        gate = g_ref[:, cols].astype(F32)
        gate = gate * (1.0 / (1.0 + jnp.exp(-gate)))
        o_ref[:, cols] = (gate * ((o - mu) * lax.rsqrt(var + LN_EPS))).astype(o_ref.dtype)


def _retention_output(o_f, o_b, p, gate_col0, n_heads):
    s_len = o_f.shape[0]
    hb = 4 if n_heads % 4 == 0 else 1
    bw = hb * HEAD_DIM
    gc = gate_col0 // bw
    tt = min(s_len, 512)
    spec = pl.BlockSpec((tt, bw), lambda i, j: (i, j))
    return pl.pallas_call(
        functools.partial(_ret_out_kernel, heads=hb),
        grid=(s_len // tt, n_heads // hb),
        in_specs=[spec, spec, pl.BlockSpec((tt, bw), lambda i, j: (i, gc + j))],
        out_specs=spec,
        out_shape=jax.ShapeDtypeStruct((s_len, n_heads * HEAD_DIM), BF16),
        compiler_params=_params("parallel", "parallel"),
    )(o_f, o_b, p)


def _residual_ln(x, y, gate, g, b, alpha):
    h = alpha * x + gate * y
    mu = jnp.mean(h, -1, keepdims=True)
    var = jnp.mean(jnp.square(h - mu), -1, keepdims=True)
    return (h - mu) * lax.rsqrt(var + LN_EPS) * g + b


def _outproj_kernel(y_ref, w_ref, x_ref, gate_ref, g_ref, b_ref, o_ref, *, alpha):
    k = pl.program_id(1)

    @pl.when(k == 0)
    def _():
        o_ref[...] = jnp.zeros_like(o_ref)

    o_ref[...] += jnp.dot(y_ref[...], w_ref[...], preferred_element_type=F32)

    @pl.when(k == pl.num_programs(1) - 1)
    def _():
        for rows in _row_chunks(x_ref.shape[0]):
            o_ref[rows, :] = _residual_ln(x_ref[rows, :], o_ref[rows, :], gate_ref[...], g_ref[...], b_ref[...], alpha)


def _outproj_ln(y, w, x, gate, g, b, alpha, tm):
    m, kdim = y.shape
    d = w.shape[1]
    tm = min(tm, m)
    tk = _pick(kdim, 512)
    vec = pl.BlockSpec((1, d), lambda i, k: (0, 0))
    return pl.pallas_call(
        functools.partial(_outproj_kernel, alpha=alpha),
        grid=(m // tm, kdim // tk),
        in_specs=[pl.BlockSpec((tm, tk), lambda i, k: (i, k)),
                  pl.BlockSpec((tk, d), lambda i, k: (k, 0)),
                  pl.BlockSpec((tm, d), lambda i, k: (i, 0)),
                  vec, vec, vec],
        out_specs=pl.BlockSpec((tm, d), lambda i, k: (i, 0)),
        out_shape=jax.ShapeDtypeStruct((m, d), F32),
        compiler_params=_params("parallel", "arbitrary"),
    )(y, w, x, gate, g.reshape(1, d), b.reshape(1, d))


def _add_ln_kernel(x_ref, f_ref, gate_ref, g_ref, b_ref, o_ref, *, alpha):
    o_ref[...] = _residual_ln(x_ref[...], f_ref[...].astype(F32), gate_ref[...], g_ref[...], b_ref[...], alpha)


def _add_ln(x, f, gate, g, b, alpha):
    m, d = x.shape
    tt = min(m, 256)
    row = pl.BlockSpec((tt, d), lambda i: (i, 0))
    vec = pl.BlockSpec((1, d), lambda i: (0, 0))
    return pl.pallas_call(
        functools.partial(_add_ln_kernel, alpha=alpha),
        grid=(m // tt,),
        in_specs=[row, row, vec, vec, vec],
        out_specs=row,
        out_shape=jax.ShapeDtypeStruct((m, d), F32),
        compiler_params=_params("parallel"),
    )(x, f, gate, g.reshape(1, d), b.reshape(1, d))


def _peer_gate_kernel(q_ref, keys_ref, s1_ref, f1_ref, s2_ref, e2_ref, tau_ref, a_ref, b_ref, cand_ref):
    half = keys_ref.shape[-1]
    k = PEER_TOPK
    s1 = lax.dot_general(keys_ref[0, 0], q_ref[:, :half].astype(BF16), _NT, preferred_element_type=F32)
    s2 = lax.dot_general(keys_ref[0, 1], q_ref[:, half:].astype(BF16), _NT, preferred_element_type=F32)

    def top_rows(s, dst_ref):
        cur = s
        for r in range(k):
            m = jnp.max(cur, 0, keepdims=True)
            dst_ref[r:r + 1, :] = m
            cur = jnp.where(cur == m, -jnp.inf, cur)

    top_rows(s1, a_ref)
    top_rows(s2, b_ref)
    for r in range(k):
        cand_ref[r * k:(r + 1) * k, :] = a_ref[r:r + 1, :] + b_ref[...]
    best = a_ref[0:1, :] + b_ref[0:1, :]
    cur = cand_ref[...]
    z = jnp.zeros_like(best)
    m = best
    for r in range(k):
        m = jnp.max(cur, 0, keepdims=True)
        z = z + jnp.exp(m - best)
        cur = jnp.where(cur == m, -jnp.inf, cur)
    s1_ref[0] = s1
    f1_ref[0] = jnp.exp(s1 - a_ref[0:1, :]) / z
    s2_ref[0] = s2
    e2_ref[0] = jnp.exp(s2 - b_ref[0:1, :])
    tau_ref[0] = m


def _peer_gates(q, sub_keys, tm):
    t = q.shape[0]
    n_heads, _, n_keys, half = sub_keys.shape
    tm = min(tm, t)
    big = jax.ShapeDtypeStruct((n_heads, n_keys, t), F32)
    blk = pl.BlockSpec((1, n_keys, tm), lambda i, h: (h, 0, i))
    return pl.pallas_call(
        _peer_gate_kernel,
        grid=(t // tm, n_heads),
        in_specs=[pl.BlockSpec((tm, 2 * half), lambda i, h: (i, h)),
                  pl.BlockSpec((1, 2, n_keys, half), lambda i, h: (h, 0, 0, 0))],
        out_specs=[blk, blk, blk, blk, pl.BlockSpec((1, 1, tm), lambda i, h: (h, 0, i))],
        out_shape=[big, big, big, big, jax.ShapeDtypeStruct((n_heads, 1, t), F32)],
        scratch_shapes=[pltpu.VMEM((PEER_TOPK, tm), F32), pltpu.VMEM((PEER_TOPK, tm), F32),
                        pltpu.VMEM((PEER_TOPK * PEER_TOPK, tm), F32)],
        compiler_params=_params("parallel", "parallel"),
    )(q, sub_keys)


def _gelu_tanh(x):
    return 0.5 * x * (1.0 + jnp.tanh(math.sqrt(2.0 / math.pi) * (x + 0.044715 * (x * x * x))))


def _peer_kernel(h_ref, u_ref, v_ref, s1_ref, f1_ref, s2_ref, e2_ref, tau_ref, o_ref, acc_ref):
    cj, ci = pl.program_id(1), pl.program_id(2)
    n_heads, ni, _ = s1_ref.shape
    nj = s2_ref.shape[1]
    d = h_ref.shape[1]

    @pl.when((cj == 0) & (ci == 0))
    def _():
        acc_ref[...] = jnp.zeros_like(acc_ref)

    u = u_ref[...].reshape(ni * nj, d)
    act = _gelu_tanh(lax.dot_general(u, h_ref[...], _NT, preferred_element_type=F32))
    rows = []
    for ii in range(ni):
        g = None
        for h in range(n_heads):
            pair = s1_ref[h, ii:ii + 1, :] + s2_ref[h]
            w = f1_ref[h, ii:ii + 1, :] * e2_ref[h]
            sel = jnp.where(pair >= tau_ref[h], w, 0.0)
            g = sel if g is None else g + sel
        rows.append(g)
    gates = jnp.concatenate(rows, 0)
    wt = (gates * act).astype(BF16)
    acc_ref[...] += lax.dot_general(wt, v_ref[...].reshape(ni * nj, d), _TN, preferred_element_type=F32)

    @pl.when((cj == pl.num_programs(1) - 1) & (ci == pl.num_programs(2) - 1))
    def _():
        o_ref[...] = acc_ref[...].astype(o_ref.dtype)


def _peer_dense(hf, u3, v3, s1, f1, s2, e2, tau, tm):
    t, d = hf.shape
    n_keys = u3.shape[0]
    n_heads = s1.shape[0]
    tm = min(tm, t)
    ni, nj = 8, 64
    tab = pl.BlockSpec((ni, nj, d), lambda i, cj, ci: (ci, cj, 0))
    return pl.pallas_call(
        _peer_kernel,
        grid=(t // tm, n_keys // nj, n_keys // ni),
        in_specs=[pl.BlockSpec((tm, d), lambda i, cj, ci: (i, 0)),
                  tab, tab,
                  pl.BlockSpec((n_heads, ni, tm), lambda i, cj, ci: (0, ci, i)),
                  pl.BlockSpec((n_heads, ni, tm), lambda i, cj, ci: (0, ci, i)),
                  pl.BlockSpec((n_heads, nj, tm), lambda i, cj, ci: (0, cj, i)),
                  pl.BlockSpec((n_heads, nj, tm), lambda i, cj, ci: (0, cj, i)),
                  pl.BlockSpec((n_heads, 1, tm), lambda i, cj, ci: (0, 0, i))],
        out_specs=pl.BlockSpec((tm, d), lambda i, cj, ci: (i, 0)),
        out_shape=jax.ShapeDtypeStruct((t, d), BF16),
        scratch_shapes=[pltpu.VMEM((tm, d), F32)],
        compiler_params=_params("parallel", "arbitrary", "arbitrary"),
    )(hf, u3, v3, s1, f1, s2, e2, tau)


def _peer_block(x, shift, scale, gate, w_q, sub_keys, u3, v3, g, b, alpha, tm):
    q, hf = _proj(x, shift, scale, w_q, F32, tm, emit_h=True)
    s1, f1, s2, e2, tau = _peer_gates(q, sub_keys, tm)
    f = _peer_dense(hf, u3, v3, s1, f1, s2, e2, tau, tm)
    return _add_ln(x, f, gate, g, b, alpha)


def kernel(x, c, ctx, c_ctx, ada_w, ada_b, ln_g, ln_b, ev_w_in, ev_w_out, ev_na_rpb, ev_ga_q_gain, ev_ga_k_gain,
           od_w_in, od_w_out, od_ret_decay_exp, od_sw_sink, peer_w_q, peer_sub_keys, peer_u, peer_v):
    assert x.shape[0] == 1 and ada_w.shape[0] == 2
    depth = ada_w.shape[0]
    _, s_len, d = x.shape
    hd = HEAD_DIM
    mix_heads = d // (2 * hd)
    kv_heads = mix_heads // 4
    group = mix_heads // kv_heads
    alpha = (2 * depth) ** 0.25
    att_scale = hd ** -0.5
    n_keys = peer_sub_keys.shape[3]
    tm = 512

    xs, cs = x[0], ctx[0]
    mods = _ada_modulation(c, c_ctx, ada_w, ada_b)
    cos_i, sin_s = _rope_tables(s_len)
    ones = jnp.ones((hd,), F32)

    def mod6(layer, which):
        return [m.reshape(1, d) for m in jnp.split(mods[layer, which], 6)]

    def peer_args(layer):
        return (peer_w_q[layer].astype(BF16), peer_sub_keys[layer].astype(BF16),
                peer_u[layer].astype(BF16).reshape(n_keys, n_keys, d),
                peer_v[layer].astype(BF16).reshape(n_keys, n_keys, d),
                ln_g[layer, 1], ln_b[layer, 1], alpha)

    sh_a, sc_a, g_a, sh_f, sc_f, g_f = mod6(0, 0)
    csh_a, csc_a, cg_a, csh_f, csc_f, cg_f = mod6(0, 1)
    w_in = ev_w_in[0].astype(BF16)
    p = _proj(xs, sh_a, sc_a, w_in, BF16, tm)
    pc = _proj(cs, csh_a, csc_a, w_in, BF16, tm)
    mh = mix_heads * hd
    kvw = kv_heads * hd
    na_q, na_k, na_v, ga_q, ga_k, ga_v = 0, mh, 2 * mh, 3 * mh, 4 * mh, 4 * mh + kvw
    qg = _prep(p, ga_q, mix_heads, ev_ga_q_gain[0], cos_i, sin_s, norm=True, rope=True, scale=att_scale)
    kg = _prep(p, ga_k, kv_heads, ev_ga_k_gain[0], cos_i, sin_s, norm=True, rope=True)
    qgc = _prep(pc, ga_q, mix_heads, ev_ga_q_gain[0], cos_i, sin_s, norm=True, scale=att_scale)
    kgc = _prep(pc, ga_k, kv_heads, ev_ga_k_gain[0], cos_i, sin_s, norm=True)
    k_all = jnp.concatenate([kg, kgc], 0)
    v_all = jnp.concatenate([p[:, ga_v:ga_v + kvw], pc[:, ga_v:ga_v + kvw]], 0)
    tk = _pick(k_all.shape[0], 1024, 256)
    y_ga = _flash(qg, 0, k_all, 0, v_all, 0, kv_heads=kv_heads, group=group, tq=256, tk=tk)
    yc_ga = _flash(qgc, 0, kgc, 0, pc, ga_v, kv_heads=kv_heads, group=group, tq=256, tk=256)
    bias = _na_bias_tables(ev_na_rpb[0], s_len // GRID_W)
    y_na = _neighbourhood_attention(p, pc, bias, mix_heads, na_q, na_k, na_v)
    yc_na = _flash(pc, na_q, pc, na_k, pc, na_v, kv_heads=mix_heads, group=1, tq=256, tk=256, scale=att_scale)
    w_out = ev_w_out[0].astype(BF16)
    xs = _outproj_ln(jnp.concatenate([y_na, y_ga], -1), w_out, xs, g_a, ln_g[0, 0], ln_b[0, 0], alpha, tm)
    cs = _outproj_ln(jnp.concatenate([yc_na, yc_ga], -1), w_out, cs, cg_a, ln_g[0, 0], ln_b[0, 0], alpha, tm)
    pa = peer_args(0)
    xs = _peer_block(xs, sh_f, sc_f, g_f, *pa, tm)
    cs = _peer_block(cs, csh_f, csc_f, cg_f, *pa, tm)

    sh_a, sc_a, g_a, sh_f, sc_f, g_f = mod6(1, 0)
    csh_a, csc_a = mod6(1, 1)[:2]
    w_in = od_w_in[0].astype(BF16)
    p = _proj(xs, sh_a, sc_a, w_in, BF16, tm)
    pc = _proj(cs, csh_a, csc_a, w_in, BF16, tm)
    r_q, r_k, r_v, r_g, s_q, s_k, s_v = 0, mh, 2 * mh, 3 * mh, 4 * mh, 5 * mh, 5 * mh + kvw
    rq = _prep(p, r_q, mix_heads, ones, cos_i, sin_s, rope=True)
    rk = _prep(p, r_k, mix_heads, ones, cos_i, sin_s, rope=True, scale=att_scale)
    rkc = _prep(pc, r_k, mix_heads, ones, cos_i, sin_s, scale=att_scale)
    sq = _prep(p, s_q, mix_heads, ones, cos_i, sin_s, rope=True, scale=att_scale)
    sk = _prep(p, s_k, kv_heads, ones, cos_i, sin_s, rope=True)
    log_gamma = jnp.log1p(-jnp.exp2(-od_ret_decay_exp[0].astype(F32)))
    o_f, o_b = _retention(rq, rk, p, r_v, rkc, pc, log_gamma, mix_heads)
    y_ret = _retention_output(o_f, o_b, p, r_g, mix_heads)
    y_sw = _sliding_window_attention(sq, sk, p, s_v, pc, s_k, pc, s_v, od_sw_sink[0],
                                     kv_heads=kv_heads, group=group)
    xs = _outproj_ln(jnp.concatenate([y_ret, y_sw], -1), od_w_out[0].astype(BF16), xs, g_a,
                     ln_g[1, 0], ln_b[1, 0], alpha, tm)
    xs = _peer_block(xs, sh_f, sc_f, g_f, *peer_args(1), tm)
    return xs[None]
```

```python
import functools
import math

import numpy as np
import jax
import jax.numpy as jnp
from jax import lax
from jax.experimental import pallas as pl
from jax.experimental.pallas import tpu as pltpu

F32 = jnp.float32
BF16 = jnp.bfloat16

GRID_W = 64
HEAD_DIM = 128
NA_WIN_R = 8
NA_WIN_C = 16
NA_ROWS_PER_STEP = 4
SW_WINDOW = 128
SW_BLOCK = 256
RET_CHUNK = 128
ROPE_THETA = 10000.0
PEER_TOPK = 16
LN_EPS = 1e-6
NEG_INF = -1e30
LOG2E = math.log2(math.e)
VMEM_LIMIT_BYTES = 56 * 1024 * 1024

_NT = (((1,), (1,)), ((), ()))
_TN = (((0,), (0,)), ((), ()))


def _params(*sem):
    return pltpu.CompilerParams(dimension_semantics=sem, vmem_limit_bytes=VMEM_LIMIT_BYTES)


def _row_chunks(n, size=128):
    return [slice(r, min(r + size, n)) for r in range(0, n, size)]


def _pick(n, cap, mult=128):
    best = None
    for t in range(mult, min(n, cap) + 1, mult):
        if n % t == 0:
            best = t
    assert best is not None, (n, cap, mult)
    return best


def _cast_kernel(x_ref, o_ref):
    o_ref[...] = x_ref[...].astype(o_ref.dtype)


def _to_bf16(w, layer):
    _, rows, cols = w.shape
    tr = _pick(rows, max(8, (2 * 1024 * 1024) // cols), 8)
    return pl.pallas_call(
        _cast_kernel,
        grid=(rows // tr,),
        in_specs=[pl.BlockSpec((None, tr, cols), lambda i: (layer, i, 0))],
        out_specs=pl.BlockSpec((tr, cols), lambda i: (i, 0)),
        out_shape=jax.ShapeDtypeStruct((rows, cols), BF16),
        compiler_params=_params("parallel"),
    )(w)


def _ada_kernel(c_ref, w_ref, b_ref, o_ref):
    c = c_ref[...]
    s = c * (1.0 / (1.0 + jnp.exp(-c)))
    o_ref[0] = jnp.dot(s.astype(BF16), w_ref[0].astype(BF16), preferred_element_type=F32) + b_ref[0]


def _ada_modulation(c, c_ctx, ada_w, ada_b):
    depth, d, n = ada_w.shape
    cc = jnp.zeros((8, d), F32).at[0].set(c[0]).at[1].set(c_ctx)
    tn = _pick(n, 512)
    out = pl.pallas_call(
        _ada_kernel,
        grid=(depth, n // tn),
        in_specs=[pl.BlockSpec((8, d), lambda l, j: (0, 0)),
                  pl.BlockSpec((1, d, tn), lambda l, j: (l, 0, j)),
                  pl.BlockSpec((1, 1, tn), lambda l, j: (l, 0, j))],
        out_specs=pl.BlockSpec((1, 8, tn), lambda l, j: (l, 0, j)),
        out_shape=jax.ShapeDtypeStruct((depth, 8, n), F32),
        compiler_params=_params("parallel", "parallel"),
    )(cc, ada_w, ada_b.reshape(depth, 1, n))
    return out


def _proj_kernel(x_ref, sh_ref, sc_ref, w_ref, o_ref, *rest, emit_h):
    h_ref = rest[-1]

    @pl.when(pl.program_id(1) == 0)
    def _():
        for rows in _row_chunks(x_ref.shape[0]):
            h = (x_ref[rows, :] * (1.0 + sc_ref[...]) + sh_ref[...]).astype(BF16)
            h_ref[rows, :] = h
            if emit_h:
                rest[0][rows, :] = h

    o_ref[...] = jnp.dot(h_ref[...], w_ref[...], preferred_element_type=F32).astype(o_ref.dtype)


def _proj(x, shift, scale, w, out_dtype, tm, emit_h=False):
    m, d = x.shape
    n = w.shape[1]
    tm = min(tm, m)
    tn = _pick(n, 1024)
    out_shape = [jax.ShapeDtypeStruct((m, n), out_dtype)]
    out_specs = [pl.BlockSpec((tm, tn), lambda i, j: (i, j))]
    if emit_h:
        out_shape.append(jax.ShapeDtypeStruct((m, d), BF16))
        out_specs.append(pl.BlockSpec((tm, d), lambda i, j: (i, 0)))
    res = pl.pallas_call(
        functools.partial(_proj_kernel, emit_h=emit_h),
        grid=(m // tm, n // tn),
        in_specs=[pl.BlockSpec((tm, d), lambda i, j: (i, 0)),
                  pl.BlockSpec((1, d), lambda i, j: (0, 0)),
                  pl.BlockSpec((1, d), lambda i, j: (0, 0)),
                  pl.BlockSpec((d, tn), lambda i, j: (0, j))],
        out_specs=out_specs,
        out_shape=out_shape,
        scratch_shapes=[pltpu.VMEM((tm, d), BF16)],
        compiler_params=_params("parallel", "arbitrary"),
    )(x, shift, scale, w)
    return res if emit_h else res[0]


def _prep_kernel(x_ref, g_ref, cos_ref, sin_ref, o_ref, *, heads, norm, rope, scale):
    for hh in range(heads):
        cols = slice(hh * HEAD_DIM, (hh + 1) * HEAD_DIM)
        x = x_ref[:, cols].astype(F32)
        if norm:
            x = x * lax.rsqrt(jnp.mean(x * x, -1, keepdims=True) + LN_EPS) * g_ref[...]
        if rope:
            lane = lax.broadcasted_iota(jnp.int32, x.shape, 1)
            partner = jnp.where(lane % 2 == 0, pltpu.roll(x, HEAD_DIM - 1, 1), pltpu.roll(x, 1, 1))
            x = x * cos_ref[...] + partner * sin_ref[...]
        if scale != 1.0:
            x = x * scale
        o_ref[:, cols] = x.astype(o_ref.dtype)


def _prep(p, col0, n_heads, gain, cos_i, sin_s, *, norm=False, rope=False, scale=1.0):
    t = p.shape[0]
    hb = 4 if n_heads % 4 == 0 else 1
    bw = hb * HEAD_DIM
    assert col0 % bw == 0
    tt = min(t, 1024)
    c0 = col0 // bw
    return pl.pallas_call(
        functools.partial(_prep_kernel, heads=hb, norm=norm, rope=rope, scale=scale),
        grid=(t // tt, n_heads // hb),
        in_specs=[pl.BlockSpec((tt, bw), lambda i, j: (i, c0 + j)),
                  pl.BlockSpec((1, HEAD_DIM), lambda i, j: (0, 0)),
                  pl.BlockSpec((tt, HEAD_DIM), lambda i, j: (i, 0)),
                  pl.BlockSpec((tt, HEAD_DIM), lambda i, j: (i, 0))],
        out_specs=pl.BlockSpec((tt, bw), lambda i, j: (i, j)),
        out_shape=jax.ShapeDtypeStruct((t, n_heads * HEAD_DIM), BF16),
        compiler_params=_params("parallel", "parallel"),
    )(p, gain.reshape(1, HEAD_DIM).astype(F32), cos_i[:t], sin_s[:t])


def _rope_tables(n_tokens):
    t = jnp.arange(n_tokens, dtype=jnp.int32)
    row = (t // GRID_W).astype(F32)
    col = (t % GRID_W).astype(F32)
    n_freq = HEAD_DIM // 4
    inv = ROPE_THETA ** (-jnp.arange(n_freq, dtype=F32) / n_freq)
    ang = jnp.concatenate([row[:, None] * inv, col[:, None] * inv], -1)
    cos, sin = jnp.cos(ang), jnp.sin(ang)
    cos_i = jnp.repeat(cos, 2, axis=-1)
    sin_s = jnp.stack([-sin, sin], -1).reshape(n_tokens, HEAD_DIM)
    return cos_i, sin_s


def _flash_kernel(q_ref, k_ref, v_ref, o_ref, m_ref, l_ref, acc_ref, *, group, tk, scale):
    n_blocks = k_ref.shape[0] // tk
    m_ref[...] = jnp.full_like(m_ref, NEG_INF)
    l_ref[...] = jnp.zeros_like(l_ref)
    acc_ref[...] = jnp.zeros_like(acc_ref)

    def block(start):
        k = k_ref[pl.ds(start, tk), :]
        v = v_ref[pl.ds(start, tk), :]
        for g in range(group):
            q = q_ref[:, g * HEAD_DIM:(g + 1) * HEAD_DIM]
            if scale != 1.0:
                q = (q.astype(F32) * scale).astype(BF16)
            s = lax.dot_general(q, k, _NT, preferred_element_type=F32)
            m_prev = m_ref[g]
            m_new = jnp.maximum(m_prev, jnp.max(s, -1, keepdims=True))
            alpha = jnp.exp2(m_prev - m_new)
            p = jnp.exp2(s - jnp.concatenate([m_new] * (tk // HEAD_DIM), 1))
            l_ref[g] = alpha * l_ref[g] + jnp.sum(p, -1, keepdims=True)
            acc_ref[g] = alpha * acc_ref[g] + jnp.dot(p.astype(BF16), v, preferred_element_type=F32)
            m_ref[g] = m_new

    def pair(i, carry):
        start = i * (2 * tk)
        block(pl.multiple_of(start, tk))
        block(pl.multiple_of(start + tk, tk))
        return carry

    lax.fori_loop(0, n_blocks // 2, pair, 0)
    if n_blocks % 2:
        block((n_blocks - 1) * tk)
    for g in range(group):
        o_ref[:, g * HEAD_DIM:(g + 1) * HEAD_DIM] = (acc_ref[g] / l_ref[g]).astype(o_ref.dtype)


def _flash(q, q_col0, k, k_col0, v, v_col0, *, kv_heads, group, tq, tk, scale=1.0):
    n_q, n_k = q.shape[0], k.shape[0]
    tq, tk = min(tq, n_q), min(tk, n_k)
    assert n_q % tq == 0 and n_k % tk == 0 and tk % HEAD_DIM == 0
    qw = group * HEAD_DIM
    assert q_col0 % qw == 0 and k_col0 % HEAD_DIM == 0 and v_col0 % HEAD_DIM == 0
    qc, kc, vc = q_col0 // qw, k_col0 // HEAD_DIM, v_col0 // HEAD_DIM
    return pl.pallas_call(
        functools.partial(_flash_kernel, group=group, tk=tk, scale=scale),
        grid=(kv_heads, n_q // tq),
        in_specs=[pl.BlockSpec((tq, qw), lambda h, i: (i, qc + h)),
                  pl.BlockSpec((n_k, HEAD_DIM), lambda h, i: (0, kc + h)),
                  pl.BlockSpec((n_k, HEAD_DIM), lambda h, i: (0, vc + h))],
        out_specs=pl.BlockSpec((tq, qw), lambda h, i: (i, h)),
        out_shape=jax.ShapeDtypeStruct((n_q, kv_heads * qw), BF16),
        scratch_shapes=[pltpu.VMEM((group, tq, HEAD_DIM), F32),
                        pltpu.VMEM((group, tq, HEAD_DIM), F32),
                        pltpu.VMEM((group, tq, HEAD_DIM), F32)],
        compiler_params=_params("parallel", "parallel"),
    )(q, k, v)


def _na_bias_tables(rpb, rows):
    n_heads = rpb.shape[0]
    rg, w, kr, kc = NA_ROWS_PER_STEP, GRID_W, NA_WIN_R, NA_WIN_C
    n_steps = rows // rg
    assert rows % rg == 0 and n_steps >= 3 and kr == 2 * rg and rows >= kr
    c = np.arange(w)
    dc = np.clip(c[None, :] - c[:, None] + (kc - 1), 0, 2 * kc - 2)
    oh_c = (np.arange(2 * kc - 1)[:, None, None] == dc[None]).astype(np.float32)
    rl, krl = np.arange(rg), np.arange(3 * rg)
    da = krl[None, :] - rl[:, None] - rg + (kr - 1)
    assert da.min() >= 0 and da.max() <= 2 * kr - 2
    oh_a = (np.arange(2 * kr - 1)[:, None, None] == da[None]).astype(np.float32)
    t1 = jnp.einsum('hab,bck->hack', rpb.astype(F32), oh_c, precision=lax.Precision.HIGHEST)
    bias = jnp.einsum('hack,arl->hrclk', t1, oh_a, precision=lax.Precision.HIGHEST)
    bias = bias.reshape(n_heads, rg * w, 3 * rg * w)
    c0 = np.clip(c - kc // 2, 0, w - kc)
    col_ok = (c[None, :] >= c0[:, None]) & (c[None, :] < c0[:, None] + kc)
    masks = []
    for g in (0, 1, n_steps - 1):
        r = rg * g + rl
        r0 = np.clip(r - kr // 2, 0, rows - kr)
        key_row = rg * (g - 1) + krl
        row_ok = (key_row[None, :] >= r0[:, None]) & (key_row[None, :] < r0[:, None] + kr)
        ok = row_ok[:, None, :, None] & col_ok[None, :, None, :]
        masks.append(ok.reshape(rg * w, 3 * rg * w))
    masks = np.stack(masks)
    return jnp.where(masks[:, None], bias[None] * LOG2E, NEG_INF)


def _na_kernel(q_ref, kp_ref, kc_ref, kn_ref, vp_ref, vc_ref, vn_ref, kx_ref, vx_ref, b_ref, o_ref, *, heads, scale):
    tq = q_ref.shape[0]
    for hh in range(heads):
        cols = slice(hh * HEAD_DIM, (hh + 1) * HEAD_DIM)
        q = (q_ref[:, cols].astype(F32) * scale).astype(BF16)
        ss = []
        for idx, k_ref in enumerate((kp_ref, kc_ref, kn_ref)):
            s = lax.dot_general(q, k_ref[:, cols], _NT, preferred_element_type=F32)
            ss.append(s + b_ref[0, hh, :, idx * tq:(idx + 1) * tq])
        ss.append(lax.dot_general(q, kx_ref[:, cols], _NT, preferred_element_type=F32))
        m = functools.reduce(jnp.maximum, [jnp.max(s, -1, keepdims=True) for s in ss])
        ps = [jnp.exp2(s - m) for s in ss]
        l = functools.reduce(jnp.add, [jnp.sum(p, -1, keepdims=True) for p in ps])
        acc = None
        for p, v_ref in zip(ps, (vp_ref, vc_ref, vn_ref, vx_ref)):
            pv = jnp.dot(p.astype(BF16), v_ref[:, cols], preferred_element_type=F32)
            acc = pv if acc is None else acc + pv
        o_ref[:, cols] = (acc / l).astype(o_ref.dtype)


def _neighbourhood_attention(p, p_ctx, bias, n_heads, q_col0, k_col0, v_col0):
    s_len = p.shape[0]
    tq = NA_ROWS_PER_STEP * GRID_W
    n_steps = s_len // tq
    hb = 4 if n_heads % 4 == 0 else 1
    bw = hb * HEAD_DIM
    assert p_ctx.shape[0] == tq
    qc, kc, vc = q_col0 // bw, k_col0 // bw, v_col0 // bw
    prev = lambda g: jnp.maximum(g - 1, 0)
    nxt = lambda g: jnp.minimum(g + 1, n_steps - 1)
    variant = lambda g: jnp.where(g == 0, 0, jnp.where(g == n_steps - 1, 2, 1))
    lat = lambda col, row: pl.BlockSpec((tq, bw), lambda h, g: (row(g), col + h))
    ident = lambda g: g
    return pl.pallas_call(
        functools.partial(_na_kernel, heads=hb, scale=HEAD_DIM ** -0.5 * LOG2E),
        grid=(n_heads // hb, n_steps),
        in_specs=[lat(qc, ident),
                  lat(kc, prev), lat(kc, ident), lat(kc, nxt),
                  lat(vc, prev), lat(vc, ident), lat(vc, nxt),
                  pl.BlockSpec((tq, bw), lambda h, g: (0, kc + h)),
                  pl.BlockSpec((tq, bw), lambda h, g: (0, vc + h)),
                  pl.BlockSpec((1, hb, tq, 3 * tq), lambda h, g: (variant(g), h, 0, 0))],
        out_specs=pl.BlockSpec((tq, bw), lambda h, g: (g, h)),
        out_shape=jax.ShapeDtypeStruct((s_len, n_heads * HEAD_DIM), BF16),
        compiler_params=_params("parallel", "arbitrary"),
    )(p, p, p, p, p, p, p, p_ctx, p_ctx, bias)


def _swa_kernel(sink_ref, q_ref, kp_ref, kc_ref, kn_ref, vp_ref, vc_ref, vn_ref, kx_ref, vx_ref, o_ref, *, group, window):
    h, n = pl.program_id(0), pl.program_id(1)
    tq = q_ref.shape[0]
    q = jnp.concatenate([q_ref[:, g * HEAD_DIM:(g + 1) * HEAD_DIM] for g in range(group)], 0)
    ql = lax.broadcasted_iota(jnp.int32, (group * tq, tq), 0) % tq
    kl = lax.broadcasted_iota(jnp.int32, (group * tq, tq), 1)
    diff = kl - ql
    ok_p = (diff - tq >= -window) & (n > 0)
    ok_c = (diff <= window) & (diff >= -window)
    ok_n = (diff + tq <= window) & (n < pl.num_programs(1) - 1)
    ss = []
    for k_ref, ok in ((kp_ref, ok_p), (kc_ref, ok_c), (kn_ref, ok_n)):
        s = lax.dot_general(q, k_ref[...], _NT, preferred_element_type=F32)
        ss.append(jnp.where(ok, s, NEG_INF))
    ss.append(lax.dot_general(q, kx_ref[...], _NT, preferred_element_type=F32))
    row_g = lax.broadcasted_iota(jnp.int32, (group * tq, 1), 0) // tq
    sink = jnp.zeros((group * tq, 1), F32)
    for g in range(group):
        sink = jnp.where(row_g == g, sink_ref[h * group + g] * LOG2E, sink)
    m = functools.reduce(jnp.maximum, [jnp.max(s, -1, keepdims=True) for s in ss] + [sink])
    ps = [jnp.exp2(s - m) for s in ss]
    l = functools.reduce(jnp.add, [jnp.sum(p, -1, keepdims=True) for p in ps]) + jnp.exp2(sink - m)
    acc = None
    for p, v_ref in zip(ps, (vp_ref, vc_ref, vn_ref, vx_ref)):
        pv = jnp.dot(p.astype(BF16), v_ref[...], preferred_element_type=F32)
        acc = pv if acc is None else acc + pv
    out = acc / l
    for g in range(group):
        o_ref[:, g * HEAD_DIM:(g + 1) * HEAD_DIM] = out[g * tq:(g + 1) * tq].astype(o_ref.dtype)


def _sliding_window_attention(q, k, v, v_col0, kx, kx_col0, vx, vx_col0, sink, *, kv_heads, group):
    s_len = q.shape[0]
    tq = SW_BLOCK
    assert s_len % tq == 0 and tq >= SW_WINDOW and kx.shape[0] == vx.shape[0]
    n_blk = s_len // tq
    n_ctx = kx.shape[0]
    qw = group * HEAD_DIM
    vc, kxc, vxc = v_col0 // HEAD_DIM, kx_col0 // HEAD_DIM, vx_col0 // HEAD_DIM
    prev = lambda n: jnp.maximum(n - 1, 0)
    nxt = lambda n: jnp.minimum(n + 1, n_blk - 1)
    ident = lambda n: n
    blk = lambda col, row: pl.BlockSpec((tq, HEAD_DIM), lambda h, n: (row(n), col + h))
    return pl.pallas_call(
        functools.partial(_swa_kernel, group=group, window=SW_WINDOW),
        grid=(kv_heads, n_blk),
        in_specs=[pl.BlockSpec(memory_space=pltpu.SMEM),
                  pl.BlockSpec((tq, qw), lambda h, n: (n, h)),
                  blk(0, prev), blk(0, ident), blk(0, nxt),
                  blk(vc, prev), blk(vc, ident), blk(vc, nxt),
                  pl.BlockSpec((n_ctx, HEAD_DIM), lambda h, n: (0, kxc + h)),
                  pl.BlockSpec((n_ctx, HEAD_DIM), lambda h, n: (0, vxc + h))],
        out_specs=pl.BlockSpec((tq, qw), lambda h, n: (n, h)),
        out_shape=jax.ShapeDtypeStruct((s_len, kv_heads * qw), BF16),
        compiler_params=_params("parallel", "arbitrary"),
    )(sink.astype(F32), q, k, k, k, v, v, v, kx, vx)


def _ret_kernel(cd_ref, qf_ref, kf_ref, vf_ref, qb_ref, kb_ref, vb_ref, kx_ref, vx_ref,
                intra_ref, qd_ref, kd_ref, wx_ref, of_ref, ob_ref, st_ref, *, heads):
    hb, n = pl.program_id(0), pl.program_id(1)

    @pl.when(n == 0)
    def _():
        for hh in range(heads):
            cols = slice(hh * HEAD_DIM, (hh + 1) * HEAD_DIM)
            for d in range(2):
                kw = (kx_ref[:, cols].astype(F32) * wx_ref[d, hh]).astype(BF16)
                st_ref[d, hh] = lax.dot_general(kw, vx_ref[:, cols], _TN, preferred_element_type=F32)

    for hh in range(heads):
        cols = slice(hh * HEAD_DIM, (hh + 1) * HEAD_DIM)
        head = hb * heads + hh
        for d, (q_ref, k_ref, v_ref, o_ref) in enumerate(((qf_ref, kf_ref, vf_ref, of_ref),
                                                          (qb_ref, kb_ref, vb_ref, ob_ref))):
            q, k, v = q_ref[:, cols], k_ref[:, cols], v_ref[:, cols]
            state = st_ref[d, hh]
            inner = lax.dot_general(q, k, _NT, preferred_element_type=F32) * intra_ref[d, hh]
            qs = (q.astype(F32) * qd_ref[d, hh]).astype(BF16)
            o_ref[:, cols] = (jnp.dot(inner.astype(BF16), v, preferred_element_type=F32)
                              + jnp.dot(qs, state.astype(BF16), preferred_element_type=F32))
            ks = (k.astype(F32) * kd_ref[d, hh]).astype(BF16)
            st_ref[d, hh] = state * cd_ref[d, head] + lax.dot_general(ks, v, _TN, preferred_element_type=F32)


def _retention(rq, rk, p, v_col0, rkx, p_ctx, log_gamma, n_heads):
    s_len = rq.shape[0]
    c = RET_CHUNK
    n_chunks = s_len // c
    n_ctx = rkx.shape[0]
    hb = 4 if n_heads % 4 == 0 else 1
    bw = hb * HEAD_DIM
    vc = v_col0 // bw
    lg = log_gamma.astype(F32)
    pos = jnp.arange(c, dtype=F32)
    diff = pos[:, None] - pos[None, :]
    intra_f = jnp.where(diff >= 0, jnp.exp(lg[0][:, None, None] * jnp.maximum(diff, 0.0)), 0.0)
    intra_b = jnp.where(diff <= 0, jnp.exp(lg[1][:, None, None] * jnp.maximum(-diff, 0.0)), 0.0)
    intra = jnp.stack([intra_f, intra_b])
    lanes = lambda t: jnp.broadcast_to(t[..., None], t.shape + (HEAD_DIM,))
    qd = lanes(jnp.stack([jnp.exp(lg[0][:, None] * (pos + 1.0)), jnp.exp(lg[1][:, None] * (c - pos))]))
    kd = lanes(jnp.stack([jnp.exp(lg[0][:, None] * (c - 1.0 - pos)), jnp.exp(lg[1][:, None] * pos)]))
    cd = jnp.exp(lg * c)
    jx = jnp.arange(n_ctx, dtype=F32)
    wx = lanes(jnp.stack([jnp.exp(lg[0][:, None] * (n_ctx - 1.0 - jx)), jnp.exp(lg[1][:, None] * jx)]))
    fwd = lambda col: pl.BlockSpec((c, bw), lambda h, n: (n, col + h))
    bwd = lambda col: pl.BlockSpec((c, bw), lambda h, n: (n_chunks - 1 - n, col + h))
    tab = lambda rows: pl.BlockSpec((2, hb, rows, HEAD_DIM), lambda h, n: (0, h, 0, 0))
    out_sds = jax.ShapeDtypeStruct((s_len, n_heads * HEAD_DIM), F32)
    return pl.pallas_call(
        functools.partial(_ret_kernel, heads=hb),
        grid=(n_heads // hb, n_chunks),
        in_specs=[pl.BlockSpec(memory_space=pltpu.SMEM),
                  fwd(0), fwd(0), fwd(vc), bwd(0), bwd(0), bwd(vc),
                  pl.BlockSpec((n_ctx, bw), lambda h, n: (0, h)),
                  pl.BlockSpec((n_ctx, bw), lambda h, n: (0, vc + h)),
                  tab(c), tab(c), tab(c), tab(n_ctx)],
        out_specs=[fwd(0), bwd(0)],
        out_shape=[out_sds, out_sds],
        scratch_shapes=[pltpu.VMEM((2, hb, HEAD_DIM, HEAD_DIM), F32)],
        compiler_params=_params("parallel", "arbitrary"),
    )(cd, rq, rk, p, rq, rk, p, rkx, p_ctx, intra, qd, kd, wx)


def _ret_out_kernel(of_ref, ob_ref, g_ref, o_ref, *, heads):
    for hh in range(heads):
        cols = slice(hh * HEAD_DIM, (hh + 1) * HEAD_DIM)
        o = of_ref[:, cols] + ob_ref[:, cols]
        mu = jnp.mean(o, -1, keepdims=True)
        var = jnp.mean(jnp.square(o - mu), -1, keepdims=True)
        gate = g_ref[:, cols].astype(F32)
        gate = gate * (1.0 / (1.0 + jnp.exp(-gate)))
        o_ref[:, cols] = (gate * ((o - mu) * lax.rsqrt(var + LN_EPS))).astype(o_ref.dtype)


def _retention_output(o_f, o_b, p, gate_col0, n_heads):
    s_len = o_f.shape[0]
    hb = 4 if n_heads % 4 == 0 else 1
    bw = hb * HEAD_DIM
    gc = gate_col0 // bw
    tt = min(s_len, 512)
    spec = pl.BlockSpec((tt, bw), lambda i, j: (i, j))
    return pl.pallas_call(
        functools.partial(_ret_out_kernel, heads=hb),
        grid=(s_len // tt, n_heads // hb),
        in_specs=[spec, spec, pl.BlockSpec((tt, bw), lambda i, j: (i, gc + j))],
        out_specs=spec,
        out_shape=jax.ShapeDtypeStruct((s_len, n_heads * HEAD_DIM), BF16),
        compiler_params=_params("parallel", "parallel"),
    )(o_f, o_b, p)


def _residual_ln(x, y, gate, g, b, alpha):
    h = alpha * x + gate * y
    mu = jnp.mean(h, -1, keepdims=True)
    var = jnp.mean(jnp.square(h - mu), -1, keepdims=True)
    return (h - mu) * lax.rsqrt(var + LN_EPS) * g + b


def _outproj_kernel(ya_ref, yb_ref, w_ref, x_ref, gate_ref, g_ref, b_ref, o_ref, *, alpha, ka):
    k = pl.program_id(1)

    @pl.when(k == 0)
    def _():
        o_ref[...] = jnp.zeros_like(o_ref)

    @pl.when(k < ka)
    def _():
        o_ref[...] += jnp.dot(ya_ref[...], w_ref[...], preferred_element_type=F32)

    @pl.when(k >= ka)
    def _():
        o_ref[...] += jnp.dot(yb_ref[...], w_ref[...], preferred_element_type=F32)

    @pl.when(k == pl.num_programs(1) - 1)
    def _():
        for rows in _row_chunks(x_ref.shape[0]):
            o_ref[rows, :] = _residual_ln(x_ref[rows, :], o_ref[rows, :], gate_ref[...], g_ref[...], b_ref[...], alpha)


def _outproj_ln(ya, yb, w, x, gate, g, b, alpha, tm):
    m, wa = ya.shape
    wb = yb.shape[1]
    d = w.shape[1]
    tm = min(tm, m)
    tk = _pick(math.gcd(wa, wb), 512)
    ka, kb = wa // tk, wb // tk
    vec = pl.BlockSpec((1, d), lambda i, k: (0, 0))
    return pl.pallas_call(
        functools.partial(_outproj_kernel, alpha=alpha, ka=ka),
        grid=(m // tm, ka + kb),
        in_specs=[pl.BlockSpec((tm, tk), lambda i, k: (i, jnp.minimum(k, ka - 1))),
                  pl.BlockSpec((tm, tk), lambda i, k: (i, jnp.maximum(k - ka, 0))),
                  pl.BlockSpec((tk, d), lambda i, k: (k, 0)),
                  pl.BlockSpec((tm, d), lambda i, k: (i, 0)),
                  vec, vec, vec],
        out_specs=pl.BlockSpec((tm, d), lambda i, k: (i, 0)),
        out_shape=jax.ShapeDtypeStruct((m, d), F32),
        compiler_params=_params("parallel", "arbitrary"),
    )(ya, yb, w, x, gate, g.reshape(1, d), b.reshape(1, d))


def _add_ln_kernel(x_ref, f_ref, gate_ref, g_ref, b_ref, o_ref, *, alpha):
    o_ref[...] = _residual_ln(x_ref[...], f_ref[...].astype(F32), gate_ref[...], g_ref[...], b_ref[...], alpha)


def _add_ln(x, f, gate, g, b, alpha):
    m, d = x.shape
    tt = min(m, 256)
    row = pl.BlockSpec((tt, d), lambda i: (i, 0))
    vec = pl.BlockSpec((1, d), lambda i: (0, 0))
    return pl.pallas_call(
        functools.partial(_add_ln_kernel, alpha=alpha),
        grid=(m // tt,),
        in_specs=[row, row, vec, vec, vec],
        out_specs=row,
        out_shape=jax.ShapeDtypeStruct((m, d), F32),
        compiler_params=_params("parallel"),
    )(x, f, gate, g.reshape(1, d), b.reshape(1, d))


def _peer_gate_kernel(q_ref, keys_ref, s1_ref, f1_ref, s2_ref, e2_ref, tau_ref, a_ref, b_ref, cand_ref):
    half = keys_ref.shape[-1]
    k = PEER_TOPK
    s1 = lax.dot_general(keys_ref[0, 0], q_ref[:, :half].astype(BF16), _NT, preferred_element_type=F32)
    s2 = lax.dot_general(keys_ref[0, 1], q_ref[:, half:].astype(BF16), _NT, preferred_element_type=F32)

    def top_rows(s, dst_ref):
        cur = s
        for r in range(k):
            m = jnp.max(cur, 0, keepdims=True)
            dst_ref[r:r + 1, :] = m
            cur = jnp.where(cur == m, -jnp.inf, cur)

    top_rows(s1, a_ref)
    top_rows(s2, b_ref)
    half_k = k // 2
    cand_ref[0:k, :] = a_ref[0:1, :] + b_ref[...]
    for r in range(1, half_k):
        cand_ref[k + (r - 1) * half_k:k + r * half_k, :] = a_ref[r:r + 1, :] + b_ref[0:half_k, :]
    cand_ref[k + (half_k - 1) * half_k:, :] = a_ref[half_k:, :] + b_ref[0:1, :]
    best = a_ref[0:1, :] + b_ref[0:1, :]
    cur = cand_ref[...]
    z = jnp.zeros_like(best)
    m = best
    for r in range(k):
        m = jnp.max(cur, 0, keepdims=True)
        z = z + jnp.exp(m - best)
        cur = jnp.where(cur == m, -jnp.inf, cur)
    s1_ref[0] = s1
    f1_ref[0] = jnp.exp(s1 - a_ref[0:1, :]) / z
    s2_ref[0] = s2
    e2_ref[0] = jnp.exp(s2 - b_ref[0:1, :])
    tau_ref[0] = m


def _peer_gates(q, sub_keys, tm):
    t = q.shape[0]
    n_heads, _, n_keys, half = sub_keys.shape
    tm = min(tm, t)
    big = jax.ShapeDtypeStruct((n_heads, n_keys, t), F32)
    blk = pl.BlockSpec((1, n_keys, tm), lambda i, h: (h, 0, i))
    return pl.pallas_call(
        _peer_gate_kernel,
        grid=(t // tm, n_heads),
        in_specs=[pl.BlockSpec((tm, 2 * half), lambda i, h: (i, h)),
                  pl.BlockSpec((1, 2, n_keys, half), lambda i, h: (h, 0, 0, 0))],
        out_specs=[blk, blk, blk, blk, pl.BlockSpec((1, 1, tm), lambda i, h: (h, 0, i))],
        out_shape=[big, big, big, big, jax.ShapeDtypeStruct((n_heads, 1, t), F32)],
        scratch_shapes=[pltpu.VMEM((PEER_TOPK, tm), F32), pltpu.VMEM((PEER_TOPK, tm), F32),
                        pltpu.VMEM((PEER_TOPK + (PEER_TOPK // 2) ** 2, tm), F32)],
        compiler_params=_params("parallel", "parallel"),
    )(q, sub_keys)


def _gelu_tanh(x):
    c = math.sqrt(2.0 / math.pi)
    hx = 0.5 * x
    return hx + hx * jnp.tanh(x * (c + (c * 0.044715) * (x * x)))


def _peer_kernel(h_ref, u_ref, v_ref, s1_ref, f1_ref, s2_ref, e2_ref, tau_ref, o_ref, acc_ref, g_ref):
    cj, ci = pl.program_id(1), pl.program_id(2)
    n_heads, ni, _ = s1_ref.shape
    nj = s2_ref.shape[1]
    d = h_ref.shape[1]

    @pl.when((cj == 0) & (ci == 0))
    def _():
        acc_ref[...] = jnp.zeros_like(acc_ref)

    for ii in range(ni):
        g = None
        for h in range(n_heads):
            pair = s1_ref[h, ii:ii + 1, :] + s2_ref[h]
            w = f1_ref[h, ii:ii + 1, :] * e2_ref[h]
            sel = jnp.where(pair >= tau_ref[h], w, 0.0)
            g = sel if g is None else g + sel
        g_ref[ii * nj:(ii + 1) * nj, :] = g
    u = u_ref[...].reshape(ni * nj, d)
    act = _gelu_tanh(lax.dot_general(u, h_ref[...], _NT, preferred_element_type=F32))
    wt = (g_ref[...] * act).astype(BF16)
    acc_ref[...] += lax.dot_general(wt, v_ref[...].reshape(ni * nj, d), _TN, preferred_element_type=F32)

    @pl.when((cj == pl.num_programs(1) - 1) & (ci == pl.num_programs(2) - 1))
    def _():
        o_ref[...] = acc_ref[...].astype(o_ref.dtype)


def _peer_dense(hf, u3, v3, s1, f1, s2, e2, tau, tm):
    t, d = hf.shape
    n_keys = u3.shape[0]
    n_heads = s1.shape[0]
    tm = min(tm, t)
    ni, nj = 8, 64
    tab = pl.BlockSpec((ni, nj, d), lambda i, cj, ci: (ci, cj, 0))
    return pl.pallas_call(
        _peer_kernel,
        grid=(t // tm, n_keys // nj, n_keys // ni),
        in_specs=[pl.BlockSpec((tm, d), lambda i, cj, ci: (i, 0)),
                  tab, tab,
                  pl.BlockSpec((n_heads, ni, tm), lambda i, cj, ci: (0, ci, i)),
                  pl.BlockSpec((n_heads, ni, tm), lambda i, cj, ci: (0, ci, i)),
                  pl.BlockSpec((n_heads, nj, tm), lambda i, cj, ci: (0, cj, i)),
                  pl.BlockSpec((n_heads, nj, tm), lambda i, cj, ci: (0, cj, i)),
                  pl.BlockSpec((n_heads, 1, tm), lambda i, cj, ci: (0, 0, i))],
        out_specs=pl.BlockSpec((tm, d), lambda i, cj, ci: (i, 0)),
        out_shape=jax.ShapeDtypeStruct((t, d), BF16),
        scratch_shapes=[pltpu.VMEM((tm, d), F32), pltpu.VMEM((ni * nj, tm), F32)],
        compiler_params=_params("parallel", "arbitrary", "arbitrary"),
    )(hf, u3, v3, s1, f1, s2, e2, tau)


def _peer_block(x, shift, scale, gate, w_q, sub_keys, u3, v3, g, b, alpha, tm):
    q, hf = _proj(x, shift, scale, w_q, F32, tm, emit_h=True)
    s1, f1, s2, e2, tau = _peer_gates(q, sub_keys, tm)
    f = _peer_dense(hf, u3, v3, s1, f1, s2, e2, tau, tm)
    return _add_ln(x, f, gate, g, b, alpha)


def kernel(x, c, ctx, c_ctx, ada_w, ada_b, ln_g, ln_b, ev_w_in, ev_w_out, ev_na_rpb, ev_ga_q_gain, ev_ga_k_gain,
           od_w_in, od_w_out, od_ret_decay_exp, od_sw_sink, peer_w_q, peer_sub_keys, peer_u, peer_v):
    assert x.shape[0] == 1 and ada_w.shape[0] == 2
    depth = ada_w.shape[0]
    _, s_len, d = x.shape
    hd = HEAD_DIM
    mix_heads = d // (2 * hd)
    kv_heads = mix_heads // 4
    group = mix_heads // kv_heads
    alpha = (2 * depth) ** 0.25
    att_scale = hd ** -0.5
    sm_scale = att_scale * LOG2E
    n_keys = peer_sub_keys.shape[3]
    tm = 512

    xs, cs = x[0], ctx[0]
    mods = _ada_modulation(c, c_ctx, ada_w, ada_b)
    cos_i, sin_s = _rope_tables(s_len)
    ones = jnp.ones((hd,), F32)

    def mod6(layer, which):
        return [m.reshape(1, d) for m in jnp.split(mods[layer, which], 6)]

    def peer_args(layer):
        return (_to_bf16(peer_w_q, layer), peer_sub_keys[layer].astype(BF16),
                _to_bf16(peer_u, layer).reshape(n_keys, n_keys, d),
                _to_bf16(peer_v, layer).reshape(n_keys, n_keys, d),
                ln_g[layer, 1], ln_b[layer, 1], alpha)

    sh_a, sc_a, g_a, sh_f, sc_f, g_f = mod6(0, 0)
    csh_a, csc_a, cg_a, csh_f, csc_f, cg_f = mod6(0, 1)
    w_in = _to_bf16(ev_w_in, 0)
    p = _proj(xs, sh_a, sc_a, w_in, BF16, tm)
    pc = _proj(cs, csh_a, csc_a, w_in, BF16, tm)
    mh = mix_heads * hd
    kvw = kv_heads * hd
    na_q, na_k, na_v, ga_q, ga_k, ga_v = 0, mh, 2 * mh, 3 * mh, 4 * mh, 4 * mh + kvw
    qg = _prep(p, ga_q, mix_heads, ev_ga_q_gain[0], cos_i, sin_s, norm=True, rope=True, scale=sm_scale)
    kg = _prep(p, ga_k, kv_heads, ev_ga_k_gain[0], cos_i, sin_s, norm=True, rope=True)
    qgc = _prep(pc, ga_q, mix_heads, ev_ga_q_gain[0], cos_i, sin_s, norm=True, scale=sm_scale)
    kgc = _prep(pc, ga_k, kv_heads, ev_ga_k_gain[0], cos_i, sin_s, norm=True)
    k_all = jnp.concatenate([kg, kgc], 0)
    v_all = jnp.concatenate([p[:, ga_v:ga_v + kvw], pc[:, ga_v:ga_v + kvw]], 0)
    tk = _pick(k_all.shape[0], 1024, 256)
    y_ga = _flash(qg, 0, k_all, 0, v_all, 0, kv_heads=kv_heads, group=group, tq=512, tk=tk)
    yc_ga = _flash(qgc, 0, kgc, 0, pc, ga_v, kv_heads=kv_heads, group=group, tq=256, tk=256)
    bias = _na_bias_tables(ev_na_rpb[0], s_len // GRID_W)
    y_na = _neighbourhood_attention(p, pc, bias, mix_heads, na_q, na_k, na_v)
    yc_na = _flash(pc, na_q, pc, na_k, pc, na_v, kv_heads=mix_heads, group=1, tq=256, tk=256, scale=sm_scale)
    w_out = _to_bf16(ev_w_out, 0)
    xs = _outproj_ln(y_na, y_ga, w_out, xs, g_a, ln_g[0, 0], ln_b[0, 0], alpha, tm)
    cs = _outproj_ln(yc_na, yc_ga, w_out, cs, cg_a, ln_g[0, 0], ln_b[0, 0], alpha, tm)
    pa = peer_args(0)
    xs = _peer_block(xs, sh_f, sc_f, g_f, *pa, tm)
    cs = _peer_block(cs, csh_f, csc_f, cg_f, *pa, tm)

    sh_a, sc_a, g_a, sh_f, sc_f, g_f = mod6(1, 0)
    csh_a, csc_a = mod6(1, 1)[:2]
    w_in = _to_bf16(od_w_in, 0)
    p = _proj(xs, sh_a, sc_a, w_in, BF16, tm)
    pc = _proj(cs, csh_a, csc_a, w_in, BF16, tm)
    r_q, r_k, r_v, r_g, s_q, s_k, s_v = 0, mh, 2 * mh, 3 * mh, 4 * mh, 5 * mh, 5 * mh + kvw
    rq = _prep(p, r_q, mix_heads, ones, cos_i, sin_s, rope=True)
    rk = _prep(p, r_k, mix_heads, ones, cos_i, sin_s, rope=True, scale=att_scale)
    rkc = _prep(pc, r_k, mix_heads, ones, cos_i, sin_s, scale=att_scale)
    sq = _prep(p, s_q, mix_heads, ones, cos_i, sin_s, rope=True, scale=sm_scale)
    sk = _prep(p, s_k, kv_heads, ones, cos_i, sin_s, rope=True)
    log_gamma = jnp.log1p(-jnp.exp2(-od_ret_decay_exp[0].astype(F32)))
    o_f, o_b = _retention(rq, rk, p, r_v, rkc, pc, log_gamma, mix_heads)
    y_ret = _retention_output(o_f, o_b, p, r_g, mix_heads)
    y_sw = _sliding_window_attention(sq, sk, p, s_v, pc, s_k, pc, s_v, od_sw_sink[0],
                                     kv_heads=kv_heads, group=group)
    xs = _outproj_ln(y_ret, y_sw, _to_bf16(od_w_out, 0), xs, g_a,
                     ln_g[1, 0], ln_b[1, 0], alpha, tm)
    xs = _peer_block(xs, sh_f, sc_f, g_f, *peer_args(1), tm)
    return xs[None]
```

```python
import functools
import math

import numpy as np
import jax
import jax.numpy as jnp
from jax import lax
from jax.experimental import pallas as pl
from jax.experimental.pallas import tpu as pltpu

F32 = jnp.float32
BF16 = jnp.bfloat16

GRID_W = 64
HEAD_DIM = 128
NA_WIN_R = 8
NA_WIN_C = 16
NA_ROWS_PER_STEP = 4
SW_WINDOW = 128
SW_BLOCK = 256
RET_CHUNK = 128
ROPE_THETA = 10000.0
PEER_TOPK = 16
LN_EPS = 1e-6
NEG_INF = -1e30
LOG2E = math.log2(math.e)
VMEM_LIMIT_BYTES = 56 * 1024 * 1024
VMEM_LIMIT_RESIDENT_BYTES = 60 * 1024 * 1024

_NT = (((1,), (1,)), ((), ()))
_TN = (((0,), (0,)), ((), ()))


def _params(*sem, vmem=VMEM_LIMIT_BYTES):
    return pltpu.CompilerParams(dimension_semantics=sem, vmem_limit_bytes=vmem)


def _row_chunks(n, size=128):
    return [slice(r, min(r + size, n)) for r in range(0, n, size)]


def _pick(n, cap, mult=128):
    best = None
    for t in range(mult, min(n, cap) + 1, mult):
        if n % t == 0:
            best = t
    assert best is not None, (n, cap, mult)
    return best


def _cast_kernel(x_ref, o_ref):
    o_ref[...] = x_ref[...].astype(o_ref.dtype)


def _to_bf16(w, layer):
    _, rows, cols = w.shape
    tr = _pick(rows, max(8, (2 * 1024 * 1024) // cols), 8)
    return pl.pallas_call(
        _cast_kernel,
        grid=(rows // tr,),
        in_specs=[pl.BlockSpec((None, tr, cols), lambda i: (layer, i, 0))],
        out_specs=pl.BlockSpec((tr, cols), lambda i: (i, 0)),
        out_shape=jax.ShapeDtypeStruct((rows, cols), BF16),
        compiler_params=_params("parallel"),
    )(w)


def _ada_kernel(c_ref, w_ref, b_ref, o_ref):
    c = c_ref[...]
    s = c * (1.0 / (1.0 + jnp.exp(-c)))
    o_ref[0] = jnp.dot(s.astype(BF16), w_ref[0].astype(BF16), preferred_element_type=F32) + b_ref[0]


def _ada_modulation(c, c_ctx, ada_w, ada_b):
    depth, d, n = ada_w.shape
    cc = jnp.zeros((8, d), F32).at[0].set(c[0]).at[1].set(c_ctx)
    tn = _pick(n, 512)
    out = pl.pallas_call(
        _ada_kernel,
        grid=(depth, n // tn),
        in_specs=[pl.BlockSpec((8, d), lambda l, j: (0, 0)),
                  pl.BlockSpec((1, d, tn), lambda l, j: (l, 0, j)),
                  pl.BlockSpec((1, 1, tn), lambda l, j: (l, 0, j))],
        out_specs=pl.BlockSpec((1, 8, tn), lambda l, j: (l, 0, j)),
        out_shape=jax.ShapeDtypeStruct((depth, 8, n), F32),
        compiler_params=_params("parallel", "parallel"),
    )(cc, ada_w, ada_b.reshape(depth, 1, n))
    return out


def _proj_kernel(x_ref, sh_ref, sc_ref, w_ref, o_ref, *rest, emit_h):
    h_ref = rest[-1]

    @pl.when(pl.program_id(1) == 0)
    def _():
        for rows in _row_chunks(x_ref.shape[0]):
            h = (x_ref[rows, :] * (1.0 + sc_ref[...]) + sh_ref[...]).astype(BF16)
            h_ref[rows, :] = h
            if emit_h:
                rest[0][rows, :] = h

    o_ref[...] = jnp.dot(h_ref[...], w_ref[...], preferred_element_type=F32).astype(o_ref.dtype)


def _proj(x, shift, scale, w, out_dtype, tm, emit_h=False, resident_w=False):
    m, d = x.shape
    n = w.shape[1]
    tm = min(tm, m)
    tn = n if resident_w else _pick(n, 1024)
    w_spec = (pl.BlockSpec((d, tn), lambda i, j: (0, j), pipeline_mode=pl.Buffered(1)) if resident_w
              else pl.BlockSpec((d, tn), lambda i, j: (0, j)))
    out_shape = [jax.ShapeDtypeStruct((m, n), out_dtype)]
    out_specs = [pl.BlockSpec((tm, tn), lambda i, j: (i, j))]
    if emit_h:
        out_shape.append(jax.ShapeDtypeStruct((m, d), BF16))
        out_specs.append(pl.BlockSpec((tm, d), lambda i, j: (i, 0)))
    res = pl.pallas_call(
        functools.partial(_proj_kernel, emit_h=emit_h),
        grid=(m // tm, n // tn),
        in_specs=[pl.BlockSpec((tm, d), lambda i, j: (i, 0)),
                  pl.BlockSpec((1, d), lambda i, j: (0, 0)),
                  pl.BlockSpec((1, d), lambda i, j: (0, 0)),
                  w_spec],
        out_specs=out_specs,
        out_shape=out_shape,
        scratch_shapes=[pltpu.VMEM((tm, d), BF16)],
        compiler_params=_params("parallel", "arbitrary"),
    )(x, shift, scale, w)
    return res if emit_h else res[0]


def _prep_kernel(x_ref, g_ref, cos_ref, sin_ref, o_ref, *, heads, norm, rope, scale):
    for hh in range(heads):
        cols = slice(hh * HEAD_DIM, (hh + 1) * HEAD_DIM)
        x = x_ref[:, cols].astype(F32)
        if norm:
            x = x * lax.rsqrt(jnp.mean(x * x, -1, keepdims=True) + LN_EPS) * g_ref[...]
        if rope:
            lane = lax.broadcasted_iota(jnp.int32, x.shape, 1)
            partner = jnp.where(lane % 2 == 0, pltpu.roll(x, HEAD_DIM - 1, 1), pltpu.roll(x, 1, 1))
            x = x * cos_ref[...] + partner * sin_ref[...]
        if scale != 1.0:
            x = x * scale
        o_ref[:, cols] = x.astype(o_ref.dtype)


def _prep(p, col0, n_heads, gain, cos_i, sin_s, *, norm=False, rope=False, scale=1.0):
    t = p.shape[0]
    hb = 4 if n_heads % 4 == 0 else 1
    bw = hb * HEAD_DIM
    assert col0 % bw == 0
    tt = min(t, 1024)
    c0 = col0 // bw
    return pl.pallas_call(
        functools.partial(_prep_kernel, heads=hb, norm=norm, rope=rope, scale=scale),
        grid=(t // tt, n_heads // hb),
        in_specs=[pl.BlockSpec((tt, bw), lambda i, j: (i, c0 + j)),
                  pl.BlockSpec((1, HEAD_DIM), lambda i, j: (0, 0)),
                  pl.BlockSpec((tt, HEAD_DIM), lambda i, j: (i, 0)),
                  pl.BlockSpec((tt, HEAD_DIM), lambda i, j: (i, 0))],
        out_specs=pl.BlockSpec((tt, bw), lambda i, j: (i, j)),
        out_shape=jax.ShapeDtypeStruct((t, n_heads * HEAD_DIM), BF16),
        compiler_params=_params("parallel", "parallel"),
    )(p, gain.reshape(1, HEAD_DIM).astype(F32), cos_i[:t], sin_s[:t])


def _rope_tables(n_tokens):
    t = jnp.arange(n_tokens, dtype=jnp.int32)
    row = (t // GRID_W).astype(F32)
    col = (t % GRID_W).astype(F32)
    n_freq = HEAD_DIM // 4
    inv = ROPE_THETA ** (-jnp.arange(n_freq, dtype=F32) / n_freq)
    ang = jnp.concatenate([row[:, None] * inv, col[:, None] * inv], -1)
    cos, sin = jnp.cos(ang), jnp.sin(ang)
    cos_i = jnp.repeat(cos, 2, axis=-1)
    sin_s = jnp.stack([-sin, sin], -1).reshape(n_tokens, HEAD_DIM)
    return cos_i, sin_s


KV_BLOCKS_PER_ITER = 3


def _flash_kernel(q_ref, k_ref, v_ref, o_ref, m_ref, l_ref, acc_ref, *, group, tk, scale):
    n_blocks = k_ref.shape[0] // tk
    m_ref[...] = jnp.full_like(m_ref, NEG_INF)
    l_ref[...] = jnp.zeros_like(l_ref)
    acc_ref[...] = jnp.zeros_like(acc_ref)

    def block(start):
        k = k_ref[pl.ds(start, tk), :]
        v = v_ref[pl.ds(start, tk), :]
        for g in range(group):
            q = q_ref[:, g * HEAD_DIM:(g + 1) * HEAD_DIM]
            if scale != 1.0:
                q = (q.astype(F32) * scale).astype(BF16)
            s = lax.dot_general(q, k, _NT, preferred_element_type=F32)
            m_prev = m_ref[g]
            m_new = jnp.maximum(m_prev, jnp.max(s, -1, keepdims=True))
            alpha = jnp.exp2(m_prev - m_new)
            p = jnp.exp2(s - jnp.concatenate([m_new] * (tk // HEAD_DIM), 1))
            l_ref[g] = alpha * l_ref[g] + jnp.sum(p, -1, keepdims=True)
            acc_ref[g] = alpha * acc_ref[g] + jnp.dot(p.astype(BF16), v, preferred_element_type=F32)
            m_ref[g] = m_new

    def group_of_blocks(i, carry):
        for b in range(KV_BLOCKS_PER_ITER):
            block(pl.multiple_of((i * KV_BLOCKS_PER_ITER + b) * tk, tk))
        return carry

    n_iter = n_blocks // KV_BLOCKS_PER_ITER
    lax.fori_loop(0, n_iter, group_of_blocks, 0)
    for b in range(n_iter * KV_BLOCKS_PER_ITER, n_blocks):
        block(b * tk)
    for g in range(group):
        o_ref[:, g * HEAD_DIM:(g + 1) * HEAD_DIM] = (acc_ref[g] / l_ref[g]).astype(o_ref.dtype)


def _flash(q, q_col0, k, k_col0, v, v_col0, *, kv_heads, group, tq, tk, scale=1.0):
    n_q, n_k = q.shape[0], k.shape[0]
    tq, tk = min(tq, n_q), min(tk, n_k)
    assert n_q % tq == 0 and n_k % tk == 0 and tk % HEAD_DIM == 0
    qw = group * HEAD_DIM
    assert q_col0 % qw == 0 and k_col0 % HEAD_DIM == 0 and v_col0 % HEAD_DIM == 0
    qc, kc, vc = q_col0 // qw, k_col0 // HEAD_DIM, v_col0 // HEAD_DIM
    return pl.pallas_call(
        functools.partial(_flash_kernel, group=group, tk=tk, scale=scale),
        grid=(kv_heads, n_q // tq),
        in_specs=[pl.BlockSpec((tq, qw), lambda h, i: (i, qc + h)),
                  pl.BlockSpec((n_k, HEAD_DIM), lambda h, i: (0, kc + h)),
                  pl.BlockSpec((n_k, HEAD_DIM), lambda h, i: (0, vc + h))],
        out_specs=pl.BlockSpec((tq, qw), lambda h, i: (i, h)),
        out_shape=jax.ShapeDtypeStruct((n_q, kv_heads * qw), BF16),
        scratch_shapes=[pltpu.VMEM((group, tq, HEAD_DIM), F32),
                        pltpu.VMEM((group, tq, HEAD_DIM), F32),
                        pltpu.VMEM((group, tq, HEAD_DIM), F32)],
        compiler_params=_params("parallel", "parallel"),
    )(q, k, v)


def _na_bias_tables(rpb, rows):
    n_heads = rpb.shape[0]
    rg, w, kr, kc = NA_ROWS_PER_STEP, GRID_W, NA_WIN_R, NA_WIN_C
    n_steps = rows // rg
    assert rows % rg == 0 and n_steps >= 3 and kr == 2 * rg and rows >= kr
    c = np.arange(w)
    dc = np.clip(c[None, :] - c[:, None] + (kc - 1), 0, 2 * kc - 2)
    oh_c = (np.arange(2 * kc - 1)[:, None, None] == dc[None]).astype(np.float32)
    rl, krl = np.arange(rg), np.arange(3 * rg)
    da = krl[None, :] - rl[:, None] - rg + (kr - 1)
    assert da.min() >= 0 and da.max() <= 2 * kr - 2
    oh_a = (np.arange(2 * kr - 1)[:, None, None] == da[None]).astype(np.float32)
    t1 = jnp.einsum('hab,bck->hack', rpb.astype(F32), oh_c, precision=lax.Precision.HIGHEST)
    bias = jnp.einsum('hack,arl->hrclk', t1, oh_a, precision=lax.Precision.HIGHEST)
    bias = bias.reshape(n_heads, rg * w, 3 * rg * w)
    c0 = np.clip(c - kc // 2, 0, w - kc)
    col_ok = (c[None, :] >= c0[:, None]) & (c[None, :] < c0[:, None] + kc)
    masks = []
    for g in (0, 1, n_steps - 1):
        r = rg * g + rl
        r0 = np.clip(r - kr // 2, 0, rows - kr)
        key_row = rg * (g - 1) + krl
        row_ok = (key_row[None, :] >= r0[:, None]) & (key_row[None, :] < r0[:, None] + kr)
        ok = row_ok[:, None, :, None] & col_ok[None, :, None, :]
        masks.append(ok.reshape(rg * w, 3 * rg * w))
    masks = np.stack(masks)
    return jnp.where(masks[:, None], bias[None] * LOG2E, NEG_INF)


def _na_kernel(q_ref, kp_ref, kc_ref, kn_ref, vp_ref, vc_ref, vn_ref, kx_ref, vx_ref, b_ref, o_ref, *, heads, scale):
    tq = q_ref.shape[0]
    for hh in range(heads):
        cols = slice(hh * HEAD_DIM, (hh + 1) * HEAD_DIM)
        q = (q_ref[:, cols].astype(F32) * scale).astype(BF16)
        ss = []
        for idx, k_ref in enumerate((kp_ref, kc_ref, kn_ref)):
            s = lax.dot_general(q, k_ref[:, cols], _NT, preferred_element_type=F32)
            ss.append(s + b_ref[0, hh, :, idx * tq:(idx + 1) * tq])
        ss.append(lax.dot_general(q, kx_ref[:, cols], _NT, preferred_element_type=F32))
        m = functools.reduce(jnp.maximum, [jnp.max(s, -1, keepdims=True) for s in ss])
        ps = [jnp.exp2(s - m) for s in ss]
        l = functools.reduce(jnp.add, [jnp.sum(p, -1, keepdims=True) for p in ps])
        acc = None
        for p, v_ref in zip(ps, (vp_ref, vc_ref, vn_ref, vx_ref)):
            pv = jnp.dot(p.astype(BF16), v_ref[:, cols], preferred_element_type=F32)
            acc = pv if acc is None else acc + pv
        o_ref[:, cols] = (acc / l).astype(o_ref.dtype)


def _neighbourhood_attention(p, p_ctx, bias, n_heads, q_col0, k_col0, v_col0):
    s_len = p.shape[0]
    tq = NA_ROWS_PER_STEP * GRID_W
    n_steps = s_len // tq
    hb = 4 if n_heads % 4 == 0 else 1
    bw = hb * HEAD_DIM
    assert p_ctx.shape[0] == tq
    qc, kc, vc = q_col0 // bw, k_col0 // bw, v_col0 // bw
    prev = lambda g: jnp.maximum(g - 1, 0)
    nxt = lambda g: jnp.minimum(g + 1, n_steps - 1)
    variant = lambda g: jnp.where(g == 0, 0, jnp.where(g == n_steps - 1, 2, 1))
    lat = lambda col, row: pl.BlockSpec((tq, bw), lambda h, g: (row(g), col + h))
    ident = lambda g: g
    return pl.pallas_call(
        functools.partial(_na_kernel, heads=hb, scale=HEAD_DIM ** -0.5 * LOG2E),
        grid=(n_heads // hb, n_steps),
        in_specs=[lat(qc, ident),
                  lat(kc, prev), lat(kc, ident), lat(kc, nxt),
                  lat(vc, prev), lat(vc, ident), lat(vc, nxt),
                  pl.BlockSpec((tq, bw), lambda h, g: (0, kc + h)),
                  pl.BlockSpec((tq, bw), lambda h, g: (0, vc + h)),
                  pl.BlockSpec((1, hb, tq, 3 * tq), lambda h, g: (variant(g), h, 0, 0))],
        out_specs=pl.BlockSpec((tq, bw), lambda h, g: (g, h)),
        out_shape=jax.ShapeDtypeStruct((s_len, n_heads * HEAD_DIM), BF16),
        compiler_params=_params("parallel", "arbitrary"),
    )(p, p, p, p, p, p, p, p_ctx, p_ctx, bias)


def _swa_kernel(sink_ref, q_ref, kp_ref, kc_ref, kn_ref, vp_ref, vc_ref, vn_ref, kx_ref, vx_ref, o_ref, *, group, window):
    h, n = pl.program_id(0), pl.program_id(1)
    tq = q_ref.shape[0]
    q = jnp.concatenate([q_ref[:, g * HEAD_DIM:(g + 1) * HEAD_DIM] for g in range(group)], 0)
    ql = lax.broadcasted_iota(jnp.int32, (group * tq, tq), 0) % tq
    kl = lax.broadcasted_iota(jnp.int32, (group * tq, tq), 1)
    diff = kl - ql
    ok_p = (diff - tq >= -window) & (n > 0)
    ok_c = (diff <= window) & (diff >= -window)
    ok_n = (diff + tq <= window) & (n < pl.num_programs(1) - 1)
    ss = []
    for k_ref, ok in ((kp_ref, ok_p), (kc_ref, ok_c), (kn_ref, ok_n)):
        s = lax.dot_general(q, k_ref[...], _NT, preferred_element_type=F32)
        ss.append(jnp.where(ok, s, NEG_INF))
    ss.append(lax.dot_general(q, kx_ref[...], _NT, preferred_element_type=F32))
    row_g = lax.broadcasted_iota(jnp.int32, (group * tq, 1), 0) // tq
    sink = jnp.zeros((group * tq, 1), F32)
    for g in range(group):
        sink = jnp.where(row_g == g, sink_ref[h * group + g] * LOG2E, sink)
    m = functools.reduce(jnp.maximum, [jnp.max(s, -1, keepdims=True) for s in ss] + [sink])
    ps = [jnp.exp2(s - m) for s in ss]
    l = functools.reduce(jnp.add, [jnp.sum(p, -1, keepdims=True) for p in ps]) + jnp.exp2(sink - m)
    acc = None
    for p, v_ref in zip(ps, (vp_ref, vc_ref, vn_ref, vx_ref)):
        pv = jnp.dot(p.astype(BF16), v_ref[...], preferred_element_type=F32)
        acc = pv if acc is None else acc + pv
    out = acc / l
    for g in range(group):
        o_ref[:, g * HEAD_DIM:(g + 1) * HEAD_DIM] = out[g * tq:(g + 1) * tq].astype(o_ref.dtype)


def _sliding_window_attention(q, k, v, v_col0, kx, kx_col0, vx, vx_col0, sink, *, kv_heads, group):
    s_len = q.shape[0]
    tq = SW_BLOCK
    assert s_len % tq == 0 and tq >= SW_WINDOW and kx.shape[0] == vx.shape[0]
    n_blk = s_len // tq
    n_ctx = kx.shape[0]
    qw = group * HEAD_DIM
    vc, kxc, vxc = v_col0 // HEAD_DIM, kx_col0 // HEAD_DIM, vx_col0 // HEAD_DIM
    prev = lambda n: jnp.maximum(n - 1, 0)
    nxt = lambda n: jnp.minimum(n + 1, n_blk - 1)
    ident = lambda n: n
    blk = lambda col, row: pl.BlockSpec((tq, HEAD_DIM), lambda h, n: (row(n), col + h))
    return pl.pallas_call(
        functools.partial(_swa_kernel, group=group, window=SW_WINDOW),
        grid=(kv_heads, n_blk),
        in_specs=[pl.BlockSpec(memory_space=pltpu.SMEM),
                  pl.BlockSpec((tq, qw), lambda h, n: (n, h)),
                  blk(0, prev), blk(0, ident), blk(0, nxt),
                  blk(vc, prev), blk(vc, ident), blk(vc, nxt),
                  pl.BlockSpec((n_ctx, HEAD_DIM), lambda h, n: (0, kxc + h)),
                  pl.BlockSpec((n_ctx, HEAD_DIM), lambda h, n: (0, vxc + h))],
        out_specs=pl.BlockSpec((tq, qw), lambda h, n: (n, h)),
        out_shape=jax.ShapeDtypeStruct((s_len, kv_heads * qw), BF16),
        compiler_params=_params("parallel", "arbitrary"),
    )(sink.astype(F32), q, k, k, k, v, v, v, kx, vx)


def _ret_kernel(cd_ref, qf_ref, kf_ref, vf_ref, qb_ref, kb_ref, vb_ref, kx_ref, vx_ref,
                intra_ref, qd_ref, kd_ref, wx_ref, of_ref, ob_ref, st_ref, *, heads):
    hb, n = pl.program_id(0), pl.program_id(1)

    @pl.when(n == 0)
    def _():
        for hh in range(heads):
            cols = slice(hh * HEAD_DIM, (hh + 1) * HEAD_DIM)
            for d in range(2):
                kw = (kx_ref[:, cols].astype(F32) * wx_ref[d, hh]).astype(BF16)
                st_ref[d, hh] = lax.dot_general(kw, vx_ref[:, cols], _TN, preferred_element_type=F32)

    for hh in range(heads):
        cols = slice(hh * HEAD_DIM, (hh + 1) * HEAD_DIM)
        head = hb * heads + hh
        for d, (q_ref, k_ref, v_ref, o_ref) in enumerate(((qf_ref, kf_ref, vf_ref, of_ref),
                                                          (qb_ref, kb_ref, vb_ref, ob_ref))):
            q, k, v = q_ref[:, cols], k_ref[:, cols], v_ref[:, cols]
            state = st_ref[d, hh]
            inner = lax.dot_general(q, k, _NT, preferred_element_type=F32) * intra_ref[d, hh]
            qs = (q.astype(F32) * qd_ref[d, hh]).astype(BF16)
            o_ref[:, cols] = (jnp.dot(inner.astype(BF16), v, preferred_element_type=F32)
                              + jnp.dot(qs, state.astype(BF16), preferred_element_type=F32))
            ks = (k.astype(F32) * kd_ref[d, hh]).astype(BF16)
            st_ref[d, hh] = state * cd_ref[d, head] + lax.dot_general(ks, v, _TN, preferred_element_type=F32)


def _retention(rq, rk, p, v_col0, rkx, p_ctx, log_gamma, n_heads):
    s_len = rq.shape[0]
    c = RET_CHUNK
    n_chunks = s_len // c
    n_ctx = rkx.shape[0]
    hb = 4 if n_heads % 4 == 0 else 1
    bw = hb * HEAD_DIM
    vc = v_col0 // bw
    lg = log_gamma.astype(F32)
    pos = jnp.arange(c, dtype=F32)
    diff = pos[:, None] - pos[None, :]
    intra_f = jnp.where(diff >= 0, jnp.exp(lg[0][:, None, None] * jnp.maximum(diff, 0.0)), 0.0)
    intra_b = jnp.where(diff <= 0, jnp.exp(lg[1][:, None, None] * jnp.maximum(-diff, 0.0)), 0.0)
    intra = jnp.stack([intra_f, intra_b])
    lanes = lambda t: jnp.broadcast_to(t[..., None], t.shape + (HEAD_DIM,))
    qd = lanes(jnp.stack([jnp.exp(lg[0][:, None] * (pos + 1.0)), jnp.exp(lg[1][:, None] * (c - pos))]))
    kd = lanes(jnp.stack([jnp.exp(lg[0][:, None] * (c - 1.0 - pos)), jnp.exp(lg[1][:, None] * pos)]))
    cd = jnp.exp(lg * c)
    jx = jnp.arange(n_ctx, dtype=F32)
    wx = lanes(jnp.stack([jnp.exp(lg[0][:, None] * (n_ctx - 1.0 - jx)), jnp.exp(lg[1][:, None] * jx)]))
    fwd = lambda col: pl.BlockSpec((c, bw), lambda h, n: (n, col + h))
    bwd = lambda col: pl.BlockSpec((c, bw), lambda h, n: (n_chunks - 1 - n, col + h))
    tab = lambda rows: pl.BlockSpec((2, hb, rows, HEAD_DIM), lambda h, n: (0, h, 0, 0))
    out_sds = jax.ShapeDtypeStruct((s_len, n_heads * HEAD_DIM), F32)
    return pl.pallas_call(
        functools.partial(_ret_kernel, heads=hb),
        grid=(n_heads // hb, n_chunks),
        in_specs=[pl.BlockSpec(memory_space=pltpu.SMEM),
                  fwd(0), fwd(0), fwd(vc), bwd(0), bwd(0), bwd(vc),
                  pl.BlockSpec((n_ctx, bw), lambda h, n: (0, h)),
                  pl.BlockSpec((n_ctx, bw), lambda h, n: (0, vc + h)),
                  tab(c), tab(c), tab(c), tab(n_ctx)],
        out_specs=[fwd(0), bwd(0)],
        out_shape=[out_sds, out_sds],
        scratch_shapes=[pltpu.VMEM((2, hb, HEAD_DIM, HEAD_DIM), F32)],
        compiler_params=_params("parallel", "arbitrary"),
    )(cd, rq, rk, p, rq, rk, p, rkx, p_ctx, intra, qd, kd, wx)


def _ret_out_kernel(of_ref, ob_ref, g_ref, o_ref, *, heads):
    for hh in range(heads):
        cols = slice(hh * HEAD_DIM, (hh + 1) * HEAD_DIM)
        o = of_ref[:, cols] + ob_ref[:, cols]
        mu = jnp.mean(o, -1, keepdims=True)
        var = jnp.mean(jnp.square(o - mu), -1, keepdims=True)
        gate = g_ref[:, cols].astype(F32)
        gate = gate * (1.0 / (1.0 + jnp.exp(-gate)))
        o_ref[:, cols] = (gate * ((o - mu) * lax.rsqrt(var + LN_EPS))).astype(o_ref.dtype)


def _retention_output(o_f, o_b, p, gate_col0, n_heads):
    s_len = o_f.shape[0]
    hb = 4 if n_heads % 4 == 0 else 1
    bw = hb * HEAD_DIM
    gc = gate_col0 // bw
    tt = min(s_len, 512)
    spec = pl.BlockSpec((tt, bw), lambda i, j: (i, j))
    return pl.pallas_call(
        functools.partial(_ret_out_kernel, heads=hb),
        grid=(s_len // tt, n_heads // hb),
        in_specs=[spec, spec, pl.BlockSpec((tt, bw), lambda i, j: (i, gc + j))],
        out_specs=spec,
        out_shape=jax.ShapeDtypeStruct((s_len, n_heads * HEAD_DIM), BF16),
        compiler_params=_params("parallel", "parallel"),
    )(o_f, o_b, p)


def _residual_ln(x, y, gate, g, b, alpha):
    h = alpha * x + gate * y
    mu = jnp.mean(h, -1, keepdims=True)
    var = jnp.mean(jnp.square(h - mu), -1, keepdims=True)
    return (h - mu) * lax.rsqrt(var + LN_EPS) * g + b


def _outproj_kernel(ya_ref, yb_ref, w_ref, x_ref, gate_ref, g_ref, b_ref, o_ref, *, alpha):
    wa = ya_ref.shape[1]
    o_ref[...] = (jnp.dot(ya_ref[...], w_ref[:wa, :], preferred_element_type=F32)
                  + jnp.dot(yb_ref[...], w_ref[wa:, :], preferred_element_type=F32))
    for rows in _row_chunks(x_ref.shape[0], 64):
        o_ref[rows, :] = _residual_ln(x_ref[rows, :], o_ref[rows, :], gate_ref[...], g_ref[...], b_ref[...], alpha)


def _outproj_ln(ya, yb, w, x, gate, g, b, alpha, tm):
    m, wa = ya.shape
    wb = yb.shape[1]
    d = w.shape[1]
    tm = min(tm, m)
    vec = pl.BlockSpec((1, d), lambda i: (0, 0))
    return pl.pallas_call(
        functools.partial(_outproj_kernel, alpha=alpha),
        grid=(m // tm,),
        in_specs=[pl.BlockSpec((tm, wa), lambda i: (i, 0)),
                  pl.BlockSpec((tm, wb), lambda i: (i, 0)),
                  pl.BlockSpec((wa + wb, d), lambda i: (0, 0), pipeline_mode=pl.Buffered(1)),
                  pl.BlockSpec((tm, d), lambda i: (i, 0)),
                  vec, vec, vec],
        out_specs=pl.BlockSpec((tm, d), lambda i: (i, 0)),
        out_shape=jax.ShapeDtypeStruct((m, d), F32),
        compiler_params=_params("parallel", vmem=VMEM_LIMIT_RESIDENT_BYTES),
    )(ya, yb, w, x, gate, g.reshape(1, d), b.reshape(1, d))


def _add_ln_kernel(x_ref, f_ref, gate_ref, g_ref, b_ref, o_ref, *, alpha):
    o_ref[...] = _residual_ln(x_ref[...], f_ref[...].astype(F32), gate_ref[...], g_ref[...], b_ref[...], alpha)


def _add_ln(x, f, gate, g, b, alpha):
    m, d = x.shape
    tt = min(m, 256)
    row = pl.BlockSpec((tt, d), lambda i: (i, 0))
    vec = pl.BlockSpec((1, d), lambda i: (0, 0))
    return pl.pallas_call(
        functools.partial(_add_ln_kernel, alpha=alpha),
        grid=(m // tt,),
        in_specs=[row, row, vec, vec, vec],
        out_specs=row,
        out_shape=jax.ShapeDtypeStruct((m, d), F32),
        compiler_params=_params("parallel"),
    )(x, f, gate, g.reshape(1, d), b.reshape(1, d))


PEER_LIST = 24


def _peer_gate_kernel(q_ref, keys_ref, d1_ref, f1_ref, s2_ref, e2_ref, a_ref, b_ref, cand_ref):
    half = keys_ref.shape[-1]
    k = PEER_TOPK
    s1 = lax.dot_general(keys_ref[0, 0], q_ref[:, :half].astype(BF16), _NT, preferred_element_type=F32)
    s2 = lax.dot_general(keys_ref[0, 1], q_ref[:, half:].astype(BF16), _NT, preferred_element_type=F32)

    def top_rows(s, dst_ref):
        dst_ref[...] = jnp.full_like(dst_ref, -jnp.inf)
        cur = s
        for r in range(k + 1):
            m = jnp.max(cur, 0, keepdims=True)
            dst_ref[r:r + 1, :] = m
            cur = jnp.where(cur == m, -jnp.inf, cur)

    top_rows(s1, a_ref)
    top_rows(s2, b_ref)
    half_k = k // 2
    cand_ref[0:PEER_LIST, :] = a_ref[0:1, :] + b_ref[...]
    for r in range(1, half_k):
        cand_ref[PEER_LIST + (r - 1) * half_k:PEER_LIST + r * half_k, :] = a_ref[r:r + 1, :] + b_ref[0:half_k, :]
    cand_ref[PEER_LIST + (half_k - 1) * half_k:, :] = a_ref[half_k:, :] + b_ref[0:1, :]
    best = a_ref[0:1, :] + b_ref[0:1, :]
    cur = cand_ref[...]
    z = jnp.zeros_like(best)
    kth = best
    for r in range(k):
        kth = jnp.max(cur, 0, keepdims=True)
        z = z + jnp.exp(kth - best)
        cur = jnp.where(cur == kth, -jnp.inf, cur)
    tau = 0.5 * (kth + jnp.max(cur, 0, keepdims=True))
    d1_ref[0] = tau - s1
    f1_ref[0] = jnp.exp(s1 - a_ref[0:1, :]) / z
    s2_ref[0] = s2
    e2_ref[0] = jnp.exp(s2 - b_ref[0:1, :])


def _peer_gates(q, sub_keys, tm):
    t = q.shape[0]
    n_heads, _, n_keys, half = sub_keys.shape
    tm = min(tm, t)
    big = jax.ShapeDtypeStruct((n_heads, n_keys, t), F32)
    blk = pl.BlockSpec((1, n_keys, tm), lambda i, h: (h, 0, i))
    return pl.pallas_call(
        _peer_gate_kernel,
        grid=(t // tm, n_heads),
        in_specs=[pl.BlockSpec((tm, 2 * half), lambda i, h: (i, h)),
                  pl.BlockSpec((1, 2, n_keys, half), lambda i, h: (h, 0, 0, 0))],
        out_specs=[blk, blk, blk, blk],
        out_shape=[big, big, big, big],
        scratch_shapes=[pltpu.VMEM((PEER_LIST, tm), F32), pltpu.VMEM((PEER_LIST, tm), F32),
                        pltpu.VMEM((2 * PEER_LIST - PEER_TOPK // 2 + (PEER_TOPK // 2 - 1) * (PEER_TOPK // 2), tm),
                                   F32)],
        compiler_params=_params("parallel", "parallel"),
    )(q, sub_keys)


def _gelu_tanh(x):
    c = math.sqrt(2.0 / math.pi)
    hx = 0.5 * x
    return hx + hx * jnp.tanh(x * (c + (c * 0.044715) * (x * x)))


def _peer_kernel(h_ref, u_ref, v_ref, d1_ref, f1_ref, s2_ref, e2_ref, o_ref, acc_ref, g_ref):
    cj, ci = pl.program_id(1), pl.program_id(2)
    n_heads, ni, tm = d1_ref.shape
    nj = s2_ref.shape[1]
    d = h_ref.shape[1]

    @pl.when((cj == 0) & (ci == 0))
    def _():
        acc_ref[...] = jnp.zeros_like(acc_ref)

    rows = 32
    for ii in range(ni):
        for r0 in range(0, nj, rows):
            for t0 in range(0, tm, 128):
                tok = slice(t0, t0 + 128)
                g = None
                for h in range(n_heads):
                    w = f1_ref[h, ii:ii + 1, tok] * e2_ref[h, r0:r0 + rows, tok]
                    sel = jnp.where(s2_ref[h, r0:r0 + rows, tok] >= d1_ref[h, ii:ii + 1, tok], w, 0.0)
                    g = sel if g is None else g + sel
                g_ref[ii * nj + r0:ii * nj + r0 + rows, tok] = g
    u = u_ref[...].reshape(ni * nj, d)
    act = _gelu_tanh(lax.dot_general(u, h_ref[...], _NT, preferred_element_type=F32))
    wt = (g_ref[...] * act).astype(BF16)
    acc_ref[...] += lax.dot_general(wt, v_ref[...].reshape(ni * nj, d), _TN, preferred_element_type=F32)

    @pl.when((cj == pl.num_programs(1) - 1) & (ci == pl.num_programs(2) - 1))
    def _():
        o_ref[...] = acc_ref[...].astype(o_ref.dtype)


def _peer_dense(hf, u3, v3, d1, f1, s2, e2, tm):
    t, d = hf.shape
    n_keys = u3.shape[0]
    n_heads = d1.shape[0]
    tm = min(tm, t)
    ni, nj = 8, 64
    tab = pl.BlockSpec((ni, nj, d), lambda i, cj, ci: (ci, cj, 0))
    first = pl.BlockSpec((n_heads, ni, tm), lambda i, cj, ci: (0, ci, i))
    second = pl.BlockSpec((n_heads, nj, tm), lambda i, cj, ci: (0, cj, i))
    return pl.pallas_call(
        _peer_kernel,
        grid=(t // tm, n_keys // nj, n_keys // ni),
        in_specs=[pl.BlockSpec((tm, d), lambda i, cj, ci: (i, 0)), tab, tab, first, first, second, second],
        out_specs=pl.BlockSpec((tm, d), lambda i, cj, ci: (i, 0)),
        out_shape=jax.ShapeDtypeStruct((t, d), BF16),
        scratch_shapes=[pltpu.VMEM((tm, d), F32), pltpu.VMEM((ni * nj, tm), F32)],
        compiler_params=_params("parallel", "arbitrary", "arbitrary"),
    )(hf, u3, v3, d1, f1, s2, e2)


def _peer_block(x, shift, scale, gate, w_q, sub_keys, u3, v3, g, b, alpha, tm):
    q, hf = _proj(x, shift, scale, w_q, F32, tm, emit_h=True, resident_w=True)
    d1, f1, s2, e2 = _peer_gates(q, sub_keys, tm)
    f = _peer_dense(hf, u3, v3, d1, f1, s2, e2, tm)
    return _add_ln(x, f, gate, g, b, alpha)


def kernel(x, c, ctx, c_ctx, ada_w, ada_b, ln_g, ln_b, ev_w_in, ev_w_out, ev_na_rpb, ev_ga_q_gain, ev_ga_k_gain,
           od_w_in, od_w_out, od_ret_decay_exp, od_sw_sink, peer_w_q, peer_sub_keys, peer_u, peer_v):
    assert x.shape[0] == 1 and ada_w.shape[0] == 2
    depth = ada_w.shape[0]
    _, s_len, d = x.shape
    hd = HEAD_DIM
    mix_heads = d // (2 * hd)
    kv_heads = mix_heads // 4
    group = mix_heads // kv_heads
    alpha = (2 * depth) ** 0.25
    att_scale = hd ** -0.5
    sm_scale = att_scale * LOG2E
    n_keys = peer_sub_keys.shape[3]
    tm = 512
    tm_out = 256

    xs, cs = x[0], ctx[0]
    mods = _ada_modulation(c, c_ctx, ada_w, ada_b)
    cos_i, sin_s = _rope_tables(s_len)
    ones = jnp.ones((hd,), F32)

    def mod6(layer, which):
        return [m.reshape(1, d) for m in jnp.split(mods[layer, which], 6)]

    def peer_args(layer):
        return (_to_bf16(peer_w_q, layer), peer_sub_keys[layer].astype(BF16),
                _to_bf16(peer_u, layer).reshape(n_keys, n_keys, d),
                _to_bf16(peer_v, layer).reshape(n_keys, n_keys, d),
                ln_g[layer, 1], ln_b[layer, 1], alpha)

    sh_a, sc_a, g_a, sh_f, sc_f, g_f = mod6(0, 0)
    csh_a, csc_a, cg_a, csh_f, csc_f, cg_f = mod6(0, 1)
    w_in = _to_bf16(ev_w_in, 0)
    p = _proj(xs, sh_a, sc_a, w_in, BF16, tm)
    pc = _proj(cs, csh_a, csc_a, w_in, BF16, tm)
    mh = mix_heads * hd
    kvw = kv_heads * hd
    na_q, na_k, na_v, ga_q, ga_k, ga_v = 0, mh, 2 * mh, 3 * mh, 4 * mh, 4 * mh + kvw
    qg = _prep(p, ga_q, mix_heads, ev_ga_q_gain[0], cos_i, sin_s, norm=True, rope=True, scale=sm_scale)
    kg = _prep(p, ga_k, kv_heads, ev_ga_k_gain[0], cos_i, sin_s, norm=True, rope=True)
    qgc = _prep(pc, ga_q, mix_heads, ev_ga_q_gain[0], cos_i, sin_s, norm=True, scale=sm_scale)
    kgc = _prep(pc, ga_k, kv_heads, ev_ga_k_gain[0], cos_i, sin_s, norm=True)
    k_all = jnp.concatenate([kg, kgc], 0)
    v_all = jnp.concatenate([p[:, ga_v:ga_v + kvw], pc[:, ga_v:ga_v + kvw]], 0)
    tk = _pick(k_all.shape[0], 1024, 256)
    y_ga = _flash(qg, 0, k_all, 0, v_all, 0, kv_heads=kv_heads, group=group, tq=512, tk=tk)
    yc_ga = _flash(qgc, 0, kgc, 0, pc, ga_v, kv_heads=kv_heads, group=group, tq=256, tk=256)
    bias = _na_bias_tables(ev_na_rpb[0], s_len // GRID_W)
    y_na = _neighbourhood_attention(p, pc, bias, mix_heads, na_q, na_k, na_v)
    yc_na = _flash(pc, na_q, pc, na_k, pc, na_v, kv_heads=mix_heads, group=1, tq=256, tk=256, scale=sm_scale)
    w_out = _to_bf16(ev_w_out, 0)
    xs = _outproj_ln(y_na, y_ga, w_out, xs, g_a, ln_g[0, 0], ln_b[0, 0], alpha, tm_out)
    cs = _outproj_ln(yc_na, yc_ga, w_out, cs, cg_a, ln_g[0, 0], ln_b[0, 0], alpha, tm_out)
    pa = peer_args(0)
    xs = _peer_block(xs, sh_f, sc_f, g_f, *pa, tm)
    cs = _peer_block(cs, csh_f, csc_f, cg_f, *pa, tm)

    sh_a, sc_a, g_a, sh_f, sc_f, g_f = mod6(1, 0)
    csh_a, csc_a = mod6(1, 1)[:2]
    w_in = _to_bf16(od_w_in, 0)
    p = _proj(xs, sh_a, sc_a, w_in, BF16, tm)
    pc = _proj(cs, csh_a, csc_a, w_in, BF16, tm)
    r_q, r_k, r_v, r_g, s_q, s_k, s_v = 0, mh, 2 * mh, 3 * mh, 4 * mh, 5 * mh, 5 * mh + kvw
    rq = _prep(p, r_q, mix_heads, ones, cos_i, sin_s, rope=True)
    rk = _prep(p, r_k, mix_heads, ones, cos_i, sin_s, rope=True, scale=att_scale)
    rkc = _prep(pc, r_k, mix_heads, ones, cos_i, sin_s, scale=att_scale)
    sq = _prep(p, s_q, mix_heads, ones, cos_i, sin_s, rope=True, scale=sm_scale)
    sk = _prep(p, s_k, kv_heads, ones, cos_i, sin_s, rope=True)
    log_gamma = jnp.log1p(-jnp.exp2(-od_ret_decay_exp[0].astype(F32)))
    o_f, o_b = _retention(rq, rk, p, r_v, rkc, pc, log_gamma, mix_heads)
    y_ret = _retention_output(o_f, o_b, p, r_g, mix_heads)
    y_sw = _sliding_window_attention(sq, sk, p, s_v, pc, s_k, pc, s_v, od_sw_sink[0],
                                     kv_heads=kv_heads, group=group)
    xs = _outproj_ln(y_ret, y_sw, _to_bf16(od_w_out, 0), xs, g_a,
                     ln_g[1, 0], ln_b[1, 0], alpha, tm_out)
    xs = _peer_block(xs, sh_f, sc_f, g_f, *peer_args(1), tm)
    return xs[None]
```

```python
import functools
import math

import numpy as np
import jax
import jax.numpy as jnp
from jax import lax
from jax.experimental import pallas as pl
from jax.experimental.pallas import tpu as pltpu

F32 = jnp.float32
BF16 = jnp.bfloat16

GRID_W = 64
HEAD_DIM = 128
NA_WIN_R = 8
NA_WIN_C = 16
NA_ROWS_PER_STEP = 4
SW_WINDOW = 128
SW_BLOCK = 256
RET_CHUNK = 128
ROPE_THETA = 10000.0
PEER_TOPK = 16
LN_EPS = 1e-6
NEG_INF = -1e30
LOG2E = math.log2(math.e)
VMEM_LIMIT_BYTES = 56 * 1024 * 1024
VMEM_LIMIT_RESIDENT_BYTES = 60 * 1024 * 1024

_NT = (((1,), (1,)), ((), ()))
_TN = (((0,), (0,)), ((), ()))


def _params(*sem, vmem=VMEM_LIMIT_BYTES):
    return pltpu.CompilerParams(dimension_semantics=sem, vmem_limit_bytes=vmem)


def _row_chunks(n, size=128):
    return [slice(r, min(r + size, n)) for r in range(0, n, size)]


def _pick(n, cap, mult=128):
    best = None
    for t in range(mult, min(n, cap) + 1, mult):
        if n % t == 0:
            best = t
    assert best is not None, (n, cap, mult)
    return best


def _cast_kernel(x_ref, o_ref):
    o_ref[...] = x_ref[...].astype(o_ref.dtype)


def _to_bf16(w, layer):
    _, rows, cols = w.shape
    tr = _pick(rows, max(8, (2 * 1024 * 1024) // cols), 8)
    return pl.pallas_call(
        _cast_kernel,
        grid=(rows // tr,),
        in_specs=[pl.BlockSpec((None, tr, cols), lambda i: (layer, i, 0))],
        out_specs=pl.BlockSpec((tr, cols), lambda i: (i, 0)),
        out_shape=jax.ShapeDtypeStruct((rows, cols), BF16),
        compiler_params=_params("parallel"),
    )(w)


PEER_NI, PEER_NJ = 8, 64


def _chunk_cast_kernel(x_ref, o_ref, *, n_keys):
    n_cj, n_i, nj, _ = o_ref.shape
    for il in range(n_i):
        for cj in range(n_cj):
            r0 = il * n_keys + cj * nj
            o_ref[cj, il] = x_ref[r0:r0 + nj, :].astype(o_ref.dtype)


def _expert_table_bf16(w, layer, n_keys):
    _, rows, d = w.shape
    ni, nj = PEER_NI, PEER_NJ
    n_ci, n_cj = n_keys // ni, n_keys // nj
    n_i = max(1, (2 * 1024 * 1024) // (n_keys * d))
    while ni % n_i:
        n_i -= 1
    per_ci = ni // n_i
    return pl.pallas_call(
        functools.partial(_chunk_cast_kernel, n_keys=n_keys),
        grid=(n_keys // n_i,),
        in_specs=[pl.BlockSpec((None, n_i * n_keys, d), lambda s: (layer, s, 0))],
        out_specs=pl.BlockSpec((None, n_cj, n_i, nj, d), lambda s: (s // per_ci, 0, s % per_ci, 0, 0)),
        out_shape=jax.ShapeDtypeStruct((n_ci, n_cj, ni, nj, d), BF16),
        compiler_params=_params("parallel"),
    )(w)


def _ada_kernel(c_ref, w_ref, b_ref, o_ref):
    c = c_ref[...]
    s = c * (1.0 / (1.0 + jnp.exp(-c)))
    o_ref[0] = jnp.dot(s.astype(BF16), w_ref[0].astype(BF16), preferred_element_type=F32) + b_ref[0]


def _ada_modulation(c, c_ctx, ada_w, ada_b):
    depth, d, n = ada_w.shape
    cc = jnp.zeros((8, d), F32).at[0].set(c[0]).at[1].set(c_ctx)
    tn = _pick(n, 512)
    out = pl.pallas_call(
        _ada_kernel,
        grid=(depth, n // tn),
        in_specs=[pl.BlockSpec((8, d), lambda l, j: (0, 0)),
                  pl.BlockSpec((1, d, tn), lambda l, j: (l, 0, j)),
                  pl.BlockSpec((1, 1, tn), lambda l, j: (l, 0, j))],
        out_specs=pl.BlockSpec((1, 8, tn), lambda l, j: (l, 0, j)),
        out_shape=jax.ShapeDtypeStruct((depth, 8, n), F32),
        compiler_params=_params("parallel", "parallel"),
    )(cc, ada_w, ada_b.reshape(depth, 1, n))
    return out


def _proj_kernel(x_ref, sh_ref, sc_ref, w_ref, o_ref, *rest, emit_h):
    h_ref = rest[-1]

    @pl.when(pl.program_id(1) == 0)
    def _():
        for rows in _row_chunks(x_ref.shape[0]):
            h = (x_ref[rows, :] * (1.0 + sc_ref[...]) + sh_ref[...]).astype(BF16)
            h_ref[rows, :] = h
            if emit_h:
                rest[0][rows, :] = h

    o_ref[...] = jnp.dot(h_ref[...], w_ref[...], preferred_element_type=F32).astype(o_ref.dtype)


def _proj(x, shift, scale, w, out_dtype, tm, emit_h=False, resident_w=False):
    m, d = x.shape
    n = w.shape[1]
    tm = min(tm, m)
    tn = n if resident_w else _pick(n, 1024)
    w_spec = (pl.BlockSpec((d, tn), lambda i, j: (0, j), pipeline_mode=pl.Buffered(1)) if resident_w
              else pl.BlockSpec((d, tn), lambda i, j: (0, j)))
    out_shape = [jax.ShapeDtypeStruct((m, n), out_dtype)]
    out_specs = [pl.BlockSpec((tm, tn), lambda i, j: (i, j))]
    if emit_h:
        out_shape.append(jax.ShapeDtypeStruct((m, d), BF16))
        out_specs.append(pl.BlockSpec((tm, d), lambda i, j: (i, 0)))
    res = pl.pallas_call(
        functools.partial(_proj_kernel, emit_h=emit_h),
        grid=(m // tm, n // tn),
        in_specs=[pl.BlockSpec((tm, d), lambda i, j: (i, 0)),
                  pl.BlockSpec((1, d), lambda i, j: (0, 0)),
                  pl.BlockSpec((1, d), lambda i, j: (0, 0)),
                  w_spec],
        out_specs=out_specs,
        out_shape=out_shape,
        scratch_shapes=[pltpu.VMEM((tm, d), BF16)],
        compiler_params=_params("parallel", "arbitrary"),
    )(x, shift, scale, w)
    return res if emit_h else res[0]


def _prep_kernel(x_ref, g_ref, cos_ref, sin_ref, o_ref, *, heads, norm, rope, scale):
    for hh in range(heads):
        cols = slice(hh * HEAD_DIM, (hh + 1) * HEAD_DIM)
        x = x_ref[:, cols].astype(F32)
        if norm:
            x = x * lax.rsqrt(jnp.mean(x * x, -1, keepdims=True) + LN_EPS) * g_ref[...]
        if rope:
            lane = lax.broadcasted_iota(jnp.int32, x.shape, 1)
            partner = jnp.where(lane % 2 == 0, pltpu.roll(x, HEAD_DIM - 1, 1), pltpu.roll(x, 1, 1))
            x = x * cos_ref[...] + partner * sin_ref[...]
        if scale != 1.0:
            x = x * scale
        o_ref[:, cols] = x.astype(o_ref.dtype)


def _prep(p, col0, n_heads, gain, cos_i, sin_s, *, norm=False, rope=False, scale=1.0):
    t = p.shape[0]
    hb = 4 if n_heads % 4 == 0 else 1
    bw = hb * HEAD_DIM
    assert col0 % bw == 0
    tt = min(t, 1024)
    c0 = col0 // bw
    return pl.pallas_call(
        functools.partial(_prep_kernel, heads=hb, norm=norm, rope=rope, scale=scale),
        grid=(t // tt, n_heads // hb),
        in_specs=[pl.BlockSpec((tt, bw), lambda i, j: (i, c0 + j)),
                  pl.BlockSpec((1, HEAD_DIM), lambda i, j: (0, 0)),
                  pl.BlockSpec((tt, HEAD_DIM), lambda i, j: (i, 0)),
                  pl.BlockSpec((tt, HEAD_DIM), lambda i, j: (i, 0))],
        out_specs=pl.BlockSpec((tt, bw), lambda i, j: (i, j)),
        out_shape=jax.ShapeDtypeStruct((t, n_heads * HEAD_DIM), BF16),
        compiler_params=_params("parallel", "parallel"),
    )(p, gain.reshape(1, HEAD_DIM).astype(F32), cos_i[:t], sin_s[:t])


def _rope_tables(n_tokens):
    t = jnp.arange(n_tokens, dtype=jnp.int32)
    row = (t // GRID_W).astype(F32)
    col = (t % GRID_W).astype(F32)
    n_freq = HEAD_DIM // 4
    inv = ROPE_THETA ** (-jnp.arange(n_freq, dtype=F32) / n_freq)
    ang = jnp.concatenate([row[:, None] * inv, col[:, None] * inv], -1)
    cos, sin = jnp.cos(ang), jnp.sin(ang)
    cos_i = jnp.repeat(cos, 2, axis=-1)
    sin_s = jnp.stack([-sin, sin], -1).reshape(n_tokens, HEAD_DIM)
    return cos_i, sin_s


KV_BLOCKS_PER_ITER = 3


def _flash_kernel(q_ref, k_ref, v_ref, o_ref, m_ref, l_ref, acc_ref, *, group, tk, scale):
    n_blocks = k_ref.shape[0] // tk
    m_ref[...] = jnp.full_like(m_ref, NEG_INF)
    l_ref[...] = jnp.zeros_like(l_ref)
    acc_ref[...] = jnp.zeros_like(acc_ref)

    def block(start):
        k = k_ref[pl.ds(start, tk), :]
        v = v_ref[pl.ds(start, tk), :]
        for g in range(group):
            q = q_ref[:, g * HEAD_DIM:(g + 1) * HEAD_DIM]
            if scale != 1.0:
                q = (q.astype(F32) * scale).astype(BF16)
            s = lax.dot_general(q, k, _NT, preferred_element_type=F32)
            m_prev = m_ref[g]
            m_new = jnp.maximum(m_prev, jnp.max(s, -1, keepdims=True))
            alpha = jnp.exp2(m_prev - m_new)
            p = jnp.exp2(s - jnp.concatenate([m_new] * (tk // HEAD_DIM), 1))
            l_ref[g] = alpha * l_ref[g] + jnp.sum(p, -1, keepdims=True)
            acc_ref[g] = alpha * acc_ref[g] + jnp.dot(p.astype(BF16), v, preferred_element_type=F32)
            m_ref[g] = m_new

    def group_of_blocks(i, carry):
        for b in range(KV_BLOCKS_PER_ITER):
            block(pl.multiple_of((i * KV_BLOCKS_PER_ITER + b) * tk, tk))
        return carry

    n_iter = n_blocks // KV_BLOCKS_PER_ITER
    lax.fori_loop(0, n_iter, group_of_blocks, 0)
    for b in range(n_iter * KV_BLOCKS_PER_ITER, n_blocks):
        block(b * tk)
    for g in range(group):
        o_ref[:, g * HEAD_DIM:(g + 1) * HEAD_DIM] = (acc_ref[g] / l_ref[g]).astype(o_ref.dtype)


def _flash(q, q_col0, k, k_col0, v, v_col0, *, kv_heads, group, tq, tk, scale=1.0):
    n_q, n_k = q.shape[0], k.shape[0]
    tq, tk = min(tq, n_q), min(tk, n_k)
    assert n_q % tq == 0 and n_k % tk == 0 and tk % HEAD_DIM == 0
    qw = group * HEAD_DIM
    assert q_col0 % qw == 0 and k_col0 % HEAD_DIM == 0 and v_col0 % HEAD_DIM == 0
    qc, kc, vc = q_col0 // qw, k_col0 // HEAD_DIM, v_col0 // HEAD_DIM
    return pl.pallas_call(
        functools.partial(_flash_kernel, group=group, tk=tk, scale=scale),
        grid=(kv_heads, n_q // tq),
        in_specs=[pl.BlockSpec((tq, qw), lambda h, i: (i, qc + h)),
                  pl.BlockSpec((n_k, HEAD_DIM), lambda h, i: (0, kc + h)),
                  pl.BlockSpec((n_k, HEAD_DIM), lambda h, i: (0, vc + h))],
        out_specs=pl.BlockSpec((tq, qw), lambda h, i: (i, h)),
        out_shape=jax.ShapeDtypeStruct((n_q, kv_heads * qw), BF16),
        scratch_shapes=[pltpu.VMEM((group, tq, HEAD_DIM), F32),
                        pltpu.VMEM((group, tq, HEAD_DIM), F32),
                        pltpu.VMEM((group, tq, HEAD_DIM), F32)],
        compiler_params=_params("parallel", "parallel"),
    )(q, k, v)


def _na_bias_tables(rpb, rows):
    n_heads = rpb.shape[0]
    rg, w, kr, kc = NA_ROWS_PER_STEP, GRID_W, NA_WIN_R, NA_WIN_C
    n_steps = rows // rg
    assert rows % rg == 0 and n_steps >= 3 and kr == 2 * rg and rows >= kr
    c = np.arange(w)
    dc = np.clip(c[None, :] - c[:, None] + (kc - 1), 0, 2 * kc - 2)
    oh_c = (np.arange(2 * kc - 1)[:, None, None] == dc[None]).astype(np.float32)
    rl, krl = np.arange(rg), np.arange(3 * rg)
    da = krl[None, :] - rl[:, None] - rg + (kr - 1)
    assert da.min() >= 0 and da.max() <= 2 * kr - 2
    oh_a = (np.arange(2 * kr - 1)[:, None, None] == da[None]).astype(np.float32)
    t1 = jnp.einsum('hab,bck->hack', rpb.astype(F32), oh_c, precision=lax.Precision.HIGHEST)
    bias = jnp.einsum('hack,arl->hrclk', t1, oh_a, precision=lax.Precision.HIGHEST)
    bias = bias.reshape(n_heads, rg * w, 3 * rg * w)
    c0 = np.clip(c - kc // 2, 0, w - kc)
    col_ok = (c[None, :] >= c0[:, None]) & (c[None, :] < c0[:, None] + kc)
    masks = []
    for g in (0, 1, n_steps - 1):
        r = rg * g + rl
        r0 = np.clip(r - kr // 2, 0, rows - kr)
        key_row = rg * (g - 1) + krl
        row_ok = (key_row[None, :] >= r0[:, None]) & (key_row[None, :] < r0[:, None] + kr)
        ok = row_ok[:, None, :, None] & col_ok[None, :, None, :]
        masks.append(ok.reshape(rg * w, 3 * rg * w))
    masks = np.stack(masks)
    return jnp.where(masks[:, None], bias[None] * LOG2E, NEG_INF)


def _na_kernel(q_ref, kp_ref, kc_ref, kn_ref, vp_ref, vc_ref, vn_ref, kx_ref, vx_ref, b_ref, o_ref, *, heads, scale):
    tq = q_ref.shape[0]
    for hh in range(heads):
        cols = slice(hh * HEAD_DIM, (hh + 1) * HEAD_DIM)
        q = (q_ref[:, cols].astype(F32) * scale).astype(BF16)
        ss = []
        for idx, k_ref in enumerate((kp_ref, kc_ref, kn_ref)):
            s = lax.dot_general(q, k_ref[:, cols], _NT, preferred_element_type=F32)
            ss.append(s + b_ref[0, hh, :, idx * tq:(idx + 1) * tq])
        ss.append(lax.dot_general(q, kx_ref[:, cols], _NT, preferred_element_type=F32))
        m = functools.reduce(jnp.maximum, [jnp.max(s, -1, keepdims=True) for s in ss])
        ps = [jnp.exp2(s - m) for s in ss]
        l = functools.reduce(jnp.add, [jnp.sum(p, -1, keepdims=True) for p in ps])
        acc = None
        for p, v_ref in zip(ps, (vp_ref, vc_ref, vn_ref, vx_ref)):
            pv = jnp.dot(p.astype(BF16), v_ref[:, cols], preferred_element_type=F32)
            acc = pv if acc is None else acc + pv
        o_ref[:, cols] = (acc / l).astype(o_ref.dtype)


def _neighbourhood_attention(p, p_ctx, bias, n_heads, q_col0, k_col0, v_col0):
    s_len = p.shape[0]
    tq = NA_ROWS_PER_STEP * GRID_W
    n_steps = s_len // tq
    hb = 4 if n_heads % 4 == 0 else 1
    bw = hb * HEAD_DIM
    assert p_ctx.shape[0] == tq
    qc, kc, vc = q_col0 // bw, k_col0 // bw, v_col0 // bw
    prev = lambda g: jnp.maximum(g - 1, 0)
    nxt = lambda g: jnp.minimum(g + 1, n_steps - 1)
    variant = lambda g: jnp.where(g == 0, 0, jnp.where(g == n_steps - 1, 2, 1))
    lat = lambda col, row: pl.BlockSpec((tq, bw), lambda h, g: (row(g), col + h))
    ident = lambda g: g
    return pl.pallas_call(
        functools.partial(_na_kernel, heads=hb, scale=HEAD_DIM ** -0.5 * LOG2E),
        grid=(n_heads // hb, n_steps),
        in_specs=[lat(qc, ident),
                  lat(kc, prev), lat(kc, ident), lat(kc, nxt),
                  lat(vc, prev), lat(vc, ident), lat(vc, nxt),
                  pl.BlockSpec((tq, bw), lambda h, g: (0, kc + h)),
                  pl.BlockSpec((tq, bw), lambda h, g: (0, vc + h)),
                  pl.BlockSpec((1, hb, tq, 3 * tq), lambda h, g: (variant(g), h, 0, 0))],
        out_specs=pl.BlockSpec((tq, bw), lambda h, g: (g, h)),
        out_shape=jax.ShapeDtypeStruct((s_len, n_heads * HEAD_DIM), BF16),
        compiler_params=_params("parallel", "arbitrary"),
    )(p, p, p, p, p, p, p, p_ctx, p_ctx, bias)


def _swa_kernel(sink_ref, q_ref, kp_ref, kc_ref, kn_ref, vp_ref, vc_ref, vn_ref, kx_ref, vx_ref, o_ref, *, group, window):
    h, n = pl.program_id(0), pl.program_id(1)
    tq = q_ref.shape[0]
    q = jnp.concatenate([q_ref[:, g * HEAD_DIM:(g + 1) * HEAD_DIM] for g in range(group)], 0)
    ql = lax.broadcasted_iota(jnp.int32, (group * tq, tq), 0) % tq
    kl = lax.broadcasted_iota(jnp.int32, (group * tq, tq), 1)
    diff = kl - ql
    ok_p = (diff - tq >= -window) & (n > 0)
    ok_c = (diff <= window) & (diff >= -window)
    ok_n = (diff + tq <= window) & (n < pl.num_programs(1) - 1)
    ss = []
    for k_ref, ok in ((kp_ref, ok_p), (kc_ref, ok_c), (kn_ref, ok_n)):
        s = lax.dot_general(q, k_ref[...], _NT, preferred_element_type=F32)
        ss.append(jnp.where(ok, s, NEG_INF))
    ss.append(lax.dot_general(q, kx_ref[...], _NT, preferred_element_type=F32))
    row_g = lax.broadcasted_iota(jnp.int32, (group * tq, 1), 0) // tq
    sink = jnp.zeros((group * tq, 1), F32)
    for g in range(group):
        sink = jnp.where(row_g == g, sink_ref[h * group + g] * LOG2E, sink)
    m = functools.reduce(jnp.maximum, [jnp.max(s, -1, keepdims=True) for s in ss] + [sink])
    ps = [jnp.exp2(s - m) for s in ss]
    l = functools.reduce(jnp.add, [jnp.sum(p, -1, keepdims=True) for p in ps]) + jnp.exp2(sink - m)
    acc = None
    for p, v_ref in zip(ps, (vp_ref, vc_ref, vn_ref, vx_ref)):
        pv = jnp.dot(p.astype(BF16), v_ref[...], preferred_element_type=F32)
        acc = pv if acc is None else acc + pv
    out = acc / l
    for g in range(group):
        o_ref[:, g * HEAD_DIM:(g + 1) * HEAD_DIM] = out[g * tq:(g + 1) * tq].astype(o_ref.dtype)


def _sliding_window_attention(q, k, v, v_col0, kx, kx_col0, vx, vx_col0, sink, *, kv_heads, group):
    s_len = q.shape[0]
    tq = SW_BLOCK
    assert s_len % tq == 0 and tq >= SW_WINDOW and kx.shape[0] == vx.shape[0]
    n_blk = s_len // tq
    n_ctx = kx.shape[0]
    qw = group * HEAD_DIM
    vc, kxc, vxc = v_col0 // HEAD_DIM, kx_col0 // HEAD_DIM, vx_col0 // HEAD_DIM
    prev = lambda n: jnp.maximum(n - 1, 0)
    nxt = lambda n: jnp.minimum(n + 1, n_blk - 1)
    ident = lambda n: n
    blk = lambda col, row: pl.BlockSpec((tq, HEAD_DIM), lambda h, n: (row(n), col + h))
    return pl.pallas_call(
        functools.partial(_swa_kernel, group=group, window=SW_WINDOW),
        grid=(kv_heads, n_blk),
        in_specs=[pl.BlockSpec(memory_space=pltpu.SMEM),
                  pl.BlockSpec((tq, qw), lambda h, n: (n, h)),
                  blk(0, prev), blk(0, ident), blk(0, nxt),
                  blk(vc, prev), blk(vc, ident), blk(vc, nxt),
                  pl.BlockSpec((n_ctx, HEAD_DIM), lambda h, n: (0, kxc + h)),
                  pl.BlockSpec((n_ctx, HEAD_DIM), lambda h, n: (0, vxc + h))],
        out_specs=pl.BlockSpec((tq, qw), lambda h, n: (n, h)),
        out_shape=jax.ShapeDtypeStruct((s_len, kv_heads * qw), BF16),
        compiler_params=_params("parallel", "arbitrary"),
    )(sink.astype(F32), q, k, k, k, v, v, v, kx, vx)


def _ret_kernel(cd_ref, qf_ref, kf_ref, vf_ref, qb_ref, kb_ref, vb_ref, kx_ref, vx_ref,
                intra_ref, qd_ref, kd_ref, wx_ref, of_ref, ob_ref, st_ref, *, heads):
    hb, n = pl.program_id(0), pl.program_id(1)

    @pl.when(n == 0)
    def _():
        for hh in range(heads):
            cols = slice(hh * HEAD_DIM, (hh + 1) * HEAD_DIM)
            for d in range(2):
                kw = (kx_ref[:, cols].astype(F32) * wx_ref[d, hh]).astype(BF16)
                st_ref[d, hh] = lax.dot_general(kw, vx_ref[:, cols], _TN, preferred_element_type=F32)

    for hh in range(heads):
        cols = slice(hh * HEAD_DIM, (hh + 1) * HEAD_DIM)
        head = hb * heads + hh
        for d, (q_ref, k_ref, v_ref, o_ref) in enumerate(((qf_ref, kf_ref, vf_ref, of_ref),
                                                          (qb_ref, kb_ref, vb_ref, ob_ref))):
            q, k, v = q_ref[:, cols], k_ref[:, cols], v_ref[:, cols]
            state = st_ref[d, hh]
            inner = lax.dot_general(q, k, _NT, preferred_element_type=F32) * intra_ref[d, hh]
            qs = (q.astype(F32) * qd_ref[d, hh]).astype(BF16)
            o_ref[:, cols] = (jnp.dot(inner.astype(BF16), v, preferred_element_type=F32)
                              + jnp.dot(qs, state.astype(BF16), preferred_element_type=F32))
            ks = (k.astype(F32) * kd_ref[d, hh]).astype(BF16)
            st_ref[d, hh] = state * cd_ref[d, head] + lax.dot_general(ks, v, _TN, preferred_element_type=F32)


def _retention(rq, rk, p, v_col0, rkx, p_ctx, log_gamma, n_heads):
    s_len = rq.shape[0]
    c = RET_CHUNK
    n_chunks = s_len // c
    n_ctx = rkx.shape[0]
    hb = 8 if n_heads % 8 == 0 else (4 if n_heads % 4 == 0 else 1)
    bw = hb * HEAD_DIM
    assert v_col0 % bw == 0
    vc = v_col0 // bw
    lg = log_gamma.astype(F32)
    pos = jnp.arange(c, dtype=F32)
    diff = pos[:, None] - pos[None, :]
    intra_f = jnp.where(diff >= 0, jnp.exp(lg[0][:, None, None] * jnp.maximum(diff, 0.0)), 0.0)
    intra_b = jnp.where(diff <= 0, jnp.exp(lg[1][:, None, None] * jnp.maximum(-diff, 0.0)), 0.0)
    intra = jnp.stack([intra_f, intra_b])
    lanes = lambda t: jnp.broadcast_to(t[..., None], t.shape + (HEAD_DIM,))
    qd = lanes(jnp.stack([jnp.exp(lg[0][:, None] * (pos + 1.0)), jnp.exp(lg[1][:, None] * (c - pos))]))
    kd = lanes(jnp.stack([jnp.exp(lg[0][:, None] * (c - 1.0 - pos)), jnp.exp(lg[1][:, None] * pos)]))
    cd = jnp.exp(lg * c)
    jx = jnp.arange(n_ctx, dtype=F32)
    wx = lanes(jnp.stack([jnp.exp(lg[0][:, None] * (n_ctx - 1.0 - jx)), jnp.exp(lg[1][:, None] * jx)]))
    fwd = lambda col: pl.BlockSpec((c, bw), lambda h, n: (n, col + h))
    bwd = lambda col: pl.BlockSpec((c, bw), lambda h, n: (n_chunks - 1 - n, col + h))
    tab = lambda rows: pl.BlockSpec((2, hb, rows, HEAD_DIM), lambda h, n: (0, h, 0, 0))
    out_sds = jax.ShapeDtypeStruct((s_len, n_heads * HEAD_DIM), F32)
    return pl.pallas_call(
        functools.partial(_ret_kernel, heads=hb),
        grid=(n_heads // hb, n_chunks),
        in_specs=[pl.BlockSpec(memory_space=pltpu.SMEM),
                  fwd(0), fwd(0), fwd(vc), bwd(0), bwd(0), bwd(vc),
                  pl.BlockSpec((n_ctx, bw), lambda h, n: (0, h)),
                  pl.BlockSpec((n_ctx, bw), lambda h, n: (0, vc + h)),
                  tab(c), tab(c), tab(c), tab(n_ctx)],
        out_specs=[fwd(0), bwd(0)],
        out_shape=[out_sds, out_sds],
        scratch_shapes=[pltpu.VMEM((2, hb, HEAD_DIM, HEAD_DIM), F32)],
        compiler_params=_params("parallel", "arbitrary"),
    )(cd, rq, rk, p, rq, rk, p, rkx, p_ctx, intra, qd, kd, wx)


def _ret_out_kernel(of_ref, ob_ref, g_ref, o_ref, *, heads):
    for hh in range(heads):
        cols = slice(hh * HEAD_DIM, (hh + 1) * HEAD_DIM)
        o = of_ref[:, cols] + ob_ref[:, cols]
        mu = jnp.mean(o, -1, keepdims=True)
        var = jnp.mean(jnp.square(o - mu), -1, keepdims=True)
        gate = g_ref[:, cols].astype(F32)
        gate = gate * (1.0 / (1.0 + jnp.exp(-gate)))
        o_ref[:, cols] = (gate * ((o - mu) * lax.rsqrt(var + LN_EPS))).astype(o_ref.dtype)


def _retention_output(o_f, o_b, p, gate_col0, n_heads):
    s_len = o_f.shape[0]
    hb = 4 if n_heads % 4 == 0 else 1
    bw = hb * HEAD_DIM
    gc = gate_col0 // bw
    tt = min(s_len, 512)
    spec = pl.BlockSpec((tt, bw), lambda i, j: (i, j))
    return pl.pallas_call(
        functools.partial(_ret_out_kernel, heads=hb),
        grid=(s_len // tt, n_heads // hb),
        in_specs=[spec, spec, pl.BlockSpec((tt, bw), lambda i, j: (i, gc + j))],
        out_specs=spec,
        out_shape=jax.ShapeDtypeStruct((s_len, n_heads * HEAD_DIM), BF16),
        compiler_params=_params("parallel", "parallel"),
    )(o_f, o_b, p)


def _residual_ln(x, y, gate, g, b, alpha):
    h = alpha * x + gate * y
    mu = jnp.mean(h, -1, keepdims=True)
    var = jnp.mean(jnp.square(h - mu), -1, keepdims=True)
    return (h - mu) * lax.rsqrt(var + LN_EPS) * g + b


def _outproj_kernel(ya_ref, yb_ref, w_ref, x_ref, gate_ref, g_ref, b_ref, o_ref, *, alpha):
    wa = ya_ref.shape[1]
    o_ref[...] = (jnp.dot(ya_ref[...], w_ref[:wa, :], preferred_element_type=F32)
                  + jnp.dot(yb_ref[...], w_ref[wa:, :], preferred_element_type=F32))
    for rows in _row_chunks(x_ref.shape[0], 64):
        o_ref[rows, :] = _residual_ln(x_ref[rows, :], o_ref[rows, :], gate_ref[...], g_ref[...], b_ref[...], alpha)


def _outproj_ln(ya, yb, w, x, gate, g, b, alpha, tm):
    m, wa = ya.shape
    wb = yb.shape[1]
    d = w.shape[1]
    tm = min(tm, m)
    vec = pl.BlockSpec((1, d), lambda i: (0, 0))
    return pl.pallas_call(
        functools.partial(_outproj_kernel, alpha=alpha),
        grid=(m // tm,),
        in_specs=[pl.BlockSpec((tm, wa), lambda i: (i, 0)),
                  pl.BlockSpec((tm, wb), lambda i: (i, 0)),
                  pl.BlockSpec((wa + wb, d), lambda i: (0, 0), pipeline_mode=pl.Buffered(1)),
                  pl.BlockSpec((tm, d), lambda i: (i, 0)),
                  vec, vec, vec],
        out_specs=pl.BlockSpec((tm, d), lambda i: (i, 0)),
        out_shape=jax.ShapeDtypeStruct((m, d), F32),
        compiler_params=_params("parallel", vmem=VMEM_LIMIT_RESIDENT_BYTES),
    )(ya, yb, w, x, gate, g.reshape(1, d), b.reshape(1, d))


def _add_ln_kernel(x_ref, f_ref, gate_ref, g_ref, b_ref, o_ref, *, alpha):
    o_ref[...] = _residual_ln(x_ref[...], f_ref[...].astype(F32), gate_ref[...], g_ref[...], b_ref[...], alpha)


def _add_ln(x, f, gate, g, b, alpha):
    m, d = x.shape
    tt = min(m, 256)
    row = pl.BlockSpec((tt, d), lambda i: (i, 0))
    vec = pl.BlockSpec((1, d), lambda i: (0, 0))
    return pl.pallas_call(
        functools.partial(_add_ln_kernel, alpha=alpha),
        grid=(m // tt,),
        in_specs=[row, row, vec, vec, vec],
        out_specs=row,
        out_shape=jax.ShapeDtypeStruct((m, d), F32),
        compiler_params=_params("parallel"),
    )(x, f, gate, g.reshape(1, d), b.reshape(1, d))


PEER_LIST = 24


def _peer_gate_kernel(q_ref, keys_ref, d1_ref, f1_ref, s2_ref, e2_ref, a_ref, b_ref, cand_ref):
    half = keys_ref.shape[-1]
    k = PEER_TOPK
    s1 = lax.dot_general(keys_ref[0, 0], q_ref[:, :half].astype(BF16), _NT, preferred_element_type=F32)
    s2 = lax.dot_general(keys_ref[0, 1], q_ref[:, half:].astype(BF16), _NT, preferred_element_type=F32)

    def top_rows(s, dst_ref):
        dst_ref[...] = jnp.full_like(dst_ref, -jnp.inf)
        cur = s
        for r in range(k + 1):
            m = jnp.max(cur, 0, keepdims=True)
            dst_ref[r:r + 1, :] = m
            cur = jnp.where(cur == m, -jnp.inf, cur)

    top_rows(s1, a_ref)
    top_rows(s2, b_ref)
    half_k = k // 2
    cand_ref[0:PEER_LIST, :] = a_ref[0:1, :] + b_ref[...]
    for r in range(1, half_k):
        cand_ref[PEER_LIST + (r - 1) * half_k:PEER_LIST + r * half_k, :] = a_ref[r:r + 1, :] + b_ref[0:half_k, :]
    cand_ref[PEER_LIST + (half_k - 1) * half_k:, :] = a_ref[half_k:, :] + b_ref[0:1, :]
    best = a_ref[0:1, :] + b_ref[0:1, :]
    cur = cand_ref[...]
    z = jnp.zeros_like(best)
    kth = best
    for r in range(k):
        kth = jnp.max(cur, 0, keepdims=True)
        z = z + jnp.exp(kth - best)
        cur = jnp.where(cur == kth, -jnp.inf, cur)
    tau = 0.5 * (kth + jnp.max(cur, 0, keepdims=True))
    d1_ref[0] = tau - s1
    f1_ref[0] = jnp.exp(s1 - a_ref[0:1, :]) / z
    s2_ref[0] = s2
    e2_ref[0] = jnp.exp(s2 - b_ref[0:1, :])


def _peer_gates(q, sub_keys, tm):
    t = q.shape[0]
    n_heads, _, n_keys, half = sub_keys.shape
    tm = min(tm, t)
    big = jax.ShapeDtypeStruct((n_heads, n_keys, t), F32)
    blk = pl.BlockSpec((1, n_keys, tm), lambda i, h: (h, 0, i))
    return pl.pallas_call(
        _peer_gate_kernel,
        grid=(t // tm, n_heads),
        in_specs=[pl.BlockSpec((tm, 2 * half), lambda i, h: (i, h)),
                  pl.BlockSpec((1, 2, n_keys, half), lambda i, h: (h, 0, 0, 0))],
        out_specs=[blk, blk, blk, blk],
        out_shape=[big, big, big, big],
        scratch_shapes=[pltpu.VMEM((PEER_LIST, tm), F32), pltpu.VMEM((PEER_LIST, tm), F32),
                        pltpu.VMEM((2 * PEER_LIST - PEER_TOPK // 2 + (PEER_TOPK // 2 - 1) * (PEER_TOPK // 2), tm),
                                   F32)],
        compiler_params=_params("parallel", "parallel"),
    )(q, sub_keys)


def _gelu_tanh(x):
    c = math.sqrt(2.0 / math.pi)
    hx = 0.5 * x
    return hx + hx * jnp.tanh(x * (c + (c * 0.044715) * (x * x)))


def _peer_kernel(h_ref, u_ref, v_ref, d1_ref, f1_ref, s2_ref, e2_ref, o_ref, acc_ref, g_ref):
    cj, ci = pl.program_id(1), pl.program_id(2)
    n_heads, ni, tm = d1_ref.shape
    nj = s2_ref.shape[1]
    d = h_ref.shape[1]

    @pl.when((cj == 0) & (ci == 0))
    def _():
        acc_ref[...] = jnp.zeros_like(acc_ref)

    rows = 32
    for ii in range(ni):
        for r0 in range(0, nj, rows):
            for t0 in range(0, tm, 128):
                tok = slice(t0, t0 + 128)
                g = None
                for h in range(n_heads):
                    w = f1_ref[h, ii:ii + 1, tok] * e2_ref[h, r0:r0 + rows, tok]
                    sel = jnp.where(s2_ref[h, r0:r0 + rows, tok] >= d1_ref[h, ii:ii + 1, tok], w, 0.0)
                    g = sel if g is None else g + sel
                g_ref[ii * nj + r0:ii * nj + r0 + rows, tok] = g
    u = u_ref[...].reshape(ni * nj, d)
    act = _gelu_tanh(lax.dot_general(u, h_ref[...], _NT, preferred_element_type=F32))
    wt = (g_ref[...] * act).astype(BF16)
    acc_ref[...] += lax.dot_general(wt, v_ref[...].reshape(ni * nj, d), _TN, preferred_element_type=F32)

    @pl.when((cj == pl.num_programs(1) - 1) & (ci == pl.num_programs(2) - 1))
    def _():
        o_ref[...] = acc_ref[...].astype(o_ref.dtype)


def _peer_dense(hf, u3, v3, d1, f1, s2, e2, tm):
    t, d = hf.shape
    n_heads, n_keys = d1.shape[0], d1.shape[1]
    tm = min(tm, t)
    ni, nj = PEER_NI, PEER_NJ
    tab = pl.BlockSpec((None, None, ni, nj, d), lambda i, cj, ci: (ci, cj, 0, 0, 0))
    first = pl.BlockSpec((n_heads, ni, tm), lambda i, cj, ci: (0, ci, i))
    second = pl.BlockSpec((n_heads, nj, tm), lambda i, cj, ci: (0, cj, i))
    return pl.pallas_call(
        _peer_kernel,
        grid=(t // tm, n_keys // nj, n_keys // ni),
        in_specs=[pl.BlockSpec((tm, d), lambda i, cj, ci: (i, 0)), tab, tab, first, first, second, second],
        out_specs=pl.BlockSpec((tm, d), lambda i, cj, ci: (i, 0)),
        out_shape=jax.ShapeDtypeStruct((t, d), BF16),
        scratch_shapes=[pltpu.VMEM((tm, d), F32), pltpu.VMEM((ni * nj, tm), F32)],
        compiler_params=_params("parallel", "arbitrary", "arbitrary"),
    )(hf, u3, v3, d1, f1, s2, e2)


def _peer_block(x, shift, scale, gate, w_q, sub_keys, u3, v3, g, b, alpha, tm):
    q, hf = _proj(x, shift, scale, w_q, F32, tm, emit_h=True, resident_w=True)
    d1, f1, s2, e2 = _peer_gates(q, sub_keys, tm)
    f = _peer_dense(hf, u3, v3, d1, f1, s2, e2, tm)
    return _add_ln(x, f, gate, g, b, alpha)


def kernel(x, c, ctx, c_ctx, ada_w, ada_b, ln_g, ln_b, ev_w_in, ev_w_out, ev_na_rpb, ev_ga_q_gain, ev_ga_k_gain,
           od_w_in, od_w_out, od_ret_decay_exp, od_sw_sink, peer_w_q, peer_sub_keys, peer_u, peer_v):
    assert x.shape[0] == 1 and ada_w.shape[0] == 2
    depth = ada_w.shape[0]
    _, s_len, d = x.shape
    hd = HEAD_DIM
    mix_heads = d // (2 * hd)
    kv_heads = mix_heads // 4
    group = mix_heads // kv_heads
    alpha = (2 * depth) ** 0.25
    att_scale = hd ** -0.5
    sm_scale = att_scale * LOG2E
    n_keys = peer_sub_keys.shape[3]
    tm = 512
    tm_out = 256

    xs, cs = x[0], ctx[0]
    mods = _ada_modulation(c, c_ctx, ada_w, ada_b)
    cos_i, sin_s = _rope_tables(s_len)
    ones = jnp.ones((hd,), F32)

    def mod6(layer, which):
        return [m.reshape(1, d) for m in jnp.split(mods[layer, which], 6)]

    def peer_args(layer):
        return (_to_bf16(peer_w_q, layer), peer_sub_keys[layer].astype(BF16),
                _expert_table_bf16(peer_u, layer, n_keys), _expert_table_bf16(peer_v, layer, n_keys),
                ln_g[layer, 1], ln_b[layer, 1], alpha)

    sh_a, sc_a, g_a, sh_f, sc_f, g_f = mod6(0, 0)
    csh_a, csc_a, cg_a, csh_f, csc_f, cg_f = mod6(0, 1)
    w_in = _to_bf16(ev_w_in, 0)
    p = _proj(xs, sh_a, sc_a, w_in, BF16, tm)
    pc = _proj(cs, csh_a, csc_a, w_in, BF16, tm)
    mh = mix_heads * hd
    kvw = kv_heads * hd
    na_q, na_k, na_v, ga_q, ga_k, ga_v = 0, mh, 2 * mh, 3 * mh, 4 * mh, 4 * mh + kvw
    qg = _prep(p, ga_q, mix_heads, ev_ga_q_gain[0], cos_i, sin_s, norm=True, rope=True, scale=sm_scale)
    kg = _prep(p, ga_k, kv_heads, ev_ga_k_gain[0], cos_i, sin_s, norm=True, rope=True)
    qgc = _prep(pc, ga_q, mix_heads, ev_ga_q_gain[0], cos_i, sin_s, norm=True, scale=sm_scale)
    kgc = _prep(pc, ga_k, kv_heads, ev_ga_k_gain[0], cos_i, sin_s, norm=True)
    k_all = jnp.concatenate([kg, kgc], 0)
    v_all = jnp.concatenate([p[:, ga_v:ga_v + kvw], pc[:, ga_v:ga_v + kvw]], 0)
    tk = _pick(k_all.shape[0], 1024, 256)
    y_ga = _flash(qg, 0, k_all, 0, v_all, 0, kv_heads=kv_heads, group=group, tq=512, tk=tk)
    yc_ga = _flash(qgc, 0, kgc, 0, pc, ga_v, kv_heads=kv_heads, group=group, tq=256, tk=256)
    bias = _na_bias_tables(ev_na_rpb[0], s_len // GRID_W)
    y_na = _neighbourhood_attention(p, pc, bias, mix_heads, na_q, na_k, na_v)
    yc_na = _flash(pc, na_q, pc, na_k, pc, na_v, kv_heads=mix_heads, group=1, tq=256, tk=256, scale=sm_scale)
    w_out = _to_bf16(ev_w_out, 0)
    xs = _outproj_ln(y_na, y_ga, w_out, xs, g_a, ln_g[0, 0], ln_b[0, 0], alpha, tm_out)
    cs = _outproj_ln(yc_na, yc_ga, w_out, cs, cg_a, ln_g[0, 0], ln_b[0, 0], alpha, tm_out)
    pa = peer_args(0)
    xs = _peer_block(xs, sh_f, sc_f, g_f, *pa, tm)
    cs = _peer_block(cs, csh_f, csc_f, cg_f, *pa, tm)

    sh_a, sc_a, g_a, sh_f, sc_f, g_f = mod6(1, 0)
    csh_a, csc_a = mod6(1, 1)[:2]
    w_in = _to_bf16(od_w_in, 0)
    p = _proj(xs, sh_a, sc_a, w_in, BF16, tm)
    pc = _proj(cs, csh_a, csc_a, w_in, BF16, tm)
    r_q, r_k, r_v, r_g, s_q, s_k, s_v = 0, mh, 2 * mh, 3 * mh, 4 * mh, 5 * mh, 5 * mh + kvw
    rq = _prep(p, r_q, mix_heads, ones, cos_i, sin_s, rope=True)
    rk = _prep(p, r_k, mix_heads, ones, cos_i, sin_s, rope=True, scale=att_scale)
    rkc = _prep(pc, r_k, mix_heads, ones, cos_i, sin_s, scale=att_scale)
    sq = _prep(p, s_q, mix_heads, ones, cos_i, sin_s, rope=True, scale=sm_scale)
    sk = _prep(p, s_k, kv_heads, ones, cos_i, sin_s, rope=True)
    log_gamma = jnp.log1p(-jnp.exp2(-od_ret_decay_exp[0].astype(F32)))
    o_f, o_b = _retention(rq, rk, p, r_v, rkc, pc, log_gamma, mix_heads)
    y_ret = _retention_output(o_f, o_b, p, r_g, mix_heads)
    y_sw = _sliding_window_attention(sq, sk, p, s_v, pc, s_k, pc, s_v, od_sw_sink[0],
                                     kv_heads=kv_heads, group=group)
    xs = _outproj_ln(y_ret, y_sw, _to_bf16(od_w_out, 0), xs, g_a,
                     ln_g[1, 0], ln_b[1, 0], alpha, tm_out)
    xs = _peer_block(xs, sh_f, sc_f, g_f, *peer_args(1), tm)
    return xs[None]
```

```python
import functools
import math

import numpy as np
import jax
import jax.numpy as jnp
from jax import lax
from jax.experimental import pallas as pl
from jax.experimental.pallas import tpu as pltpu

F32 = jnp.float32
BF16 = jnp.bfloat16

GRID_W = 64
HEAD_DIM = 128
NA_WIN_R = 8
NA_WIN_C = 16
NA_ROWS_PER_STEP = 4
SW_WINDOW = 128
SW_BLOCK = 256
RET_CHUNK = 128
ROPE_THETA = 10000.0
PEER_TOPK = 16
LN_EPS = 1e-6
NEG_INF = -1e30
LOG2E = math.log2(math.e)
VMEM_LIMIT_BYTES = 56 * 1024 * 1024
VMEM_LIMIT_RESIDENT_BYTES = 60 * 1024 * 1024

_NT = (((1,), (1,)), ((), ()))
_TN = (((0,), (0,)), ((), ()))


def _params(*sem, vmem=VMEM_LIMIT_BYTES):
    return pltpu.CompilerParams(dimension_semantics=sem, vmem_limit_bytes=vmem)


def _row_chunks(n, size=128):
    return [slice(r, min(r + size, n)) for r in range(0, n, size)]


def _pick(n, cap, mult=128):
    best = None
    for t in range(mult, min(n, cap) + 1, mult):
        if n % t == 0:
            best = t
    assert best is not None, (n, cap, mult)
    return best


def _cast_kernel(x_ref, o_ref):
    o_ref[...] = x_ref[...].astype(o_ref.dtype)


def _to_bf16(w, layer):
    _, rows, cols = w.shape
    tr = _pick(rows, max(8, (2 * 1024 * 1024) // cols), 8)
    return pl.pallas_call(
        _cast_kernel,
        grid=(rows // tr,),
        in_specs=[pl.BlockSpec((None, tr, cols), lambda i: (layer, i, 0))],
        out_specs=pl.BlockSpec((tr, cols), lambda i: (i, 0)),
        out_shape=jax.ShapeDtypeStruct((rows, cols), BF16),
        compiler_params=_params("parallel"),
    )(w)


PEER_NI, PEER_NJ = 8, 64


class _SideCast:
    def __init__(self, tables, layer, n_steps, step_of):
        self.n = len(tables)
        self.tables = list(tables)
        self.in_specs, self.out_specs, self.out_shapes = [], [], []
        for w in tables:
            _, rows, d = w.shape
            assert rows % n_steps == 0 and (rows // n_steps) % 16 == 0
            r = rows // n_steps
            self.in_specs.append(pl.BlockSpec((None, r, d), lambda *ids: (layer, step_of(*ids), 0)))
            self.out_specs.append(pl.BlockSpec((r, d), lambda *ids: (step_of(*ids), 0)))
            self.out_shapes.append(jax.ShapeDtypeStruct((rows, d), BF16))

    @staticmethod
    def run(in_refs, out_refs):
        for src, dst in zip(in_refs, out_refs):
            for rows in _row_chunks(src.shape[0], 64):
                dst[rows, :] = src[rows, :].astype(dst.dtype)


def _ada_kernel(c_ref, w_ref, b_ref, o_ref):
    c = c_ref[...]
    s = c * (1.0 / (1.0 + jnp.exp(-c)))
    o_ref[0] = jnp.dot(s.astype(BF16), w_ref[0].astype(BF16), preferred_element_type=F32) + b_ref[0]


def _ada_modulation(c, c_ctx, ada_w, ada_b):
    depth, d, n = ada_w.shape
    cc = jnp.zeros((8, d), F32).at[0].set(c[0]).at[1].set(c_ctx)
    tn = _pick(n, 512)
    out = pl.pallas_call(
        _ada_kernel,
        grid=(depth, n // tn),
        in_specs=[pl.BlockSpec((8, d), lambda l, j: (0, 0)),
                  pl.BlockSpec((1, d, tn), lambda l, j: (l, 0, j)),
                  pl.BlockSpec((1, 1, tn), lambda l, j: (l, 0, j))],
        out_specs=pl.BlockSpec((1, 8, tn), lambda l, j: (l, 0, j)),
        out_shape=jax.ShapeDtypeStruct((depth, 8, n), F32),
        compiler_params=_params("parallel", "parallel"),
    )(cc, ada_w, ada_b.reshape(depth, 1, n))
    return out


def _proj_kernel(x_ref, sh_ref, sc_ref, w_ref, o_ref, *rest, emit_h):
    h_ref = rest[-1]

    @pl.when(pl.program_id(1) == 0)
    def _():
        for rows in _row_chunks(x_ref.shape[0]):
            h = (x_ref[rows, :] * (1.0 + sc_ref[...]) + sh_ref[...]).astype(BF16)
            h_ref[rows, :] = h
            if emit_h:
                rest[0][rows, :] = h

    o_ref[...] = jnp.dot(h_ref[...], w_ref[...], preferred_element_type=F32).astype(o_ref.dtype)


def _proj(x, shift, scale, w, out_dtype, tm, emit_h=False, resident_w=False):
    m, d = x.shape
    n = w.shape[1]
    tm = min(tm, m)
    tn = n if resident_w else _pick(n, 1024)
    w_spec = (pl.BlockSpec((d, tn), lambda i, j: (0, j), pipeline_mode=pl.Buffered(1)) if resident_w
              else pl.BlockSpec((d, tn), lambda i, j: (0, j)))
    out_shape = [jax.ShapeDtypeStruct((m, n), out_dtype)]
    out_specs = [pl.BlockSpec((tm, tn), lambda i, j: (i, j))]
    if emit_h:
        out_shape.append(jax.ShapeDtypeStruct((m, d), BF16))
        out_specs.append(pl.BlockSpec((tm, d), lambda i, j: (i, 0)))
    res = pl.pallas_call(
        functools.partial(_proj_kernel, emit_h=emit_h),
        grid=(m // tm, n // tn),
        in_specs=[pl.BlockSpec((tm, d), lambda i, j: (i, 0)),
                  pl.BlockSpec((1, d), lambda i, j: (0, 0)),
                  pl.BlockSpec((1, d), lambda i, j: (0, 0)),
                  w_spec],
        out_specs=out_specs,
        out_shape=out_shape,
        scratch_shapes=[pltpu.VMEM((tm, d), BF16)],
        compiler_params=_params("parallel", "arbitrary"),
    )(x, shift, scale, w)
    return res if emit_h else res[0]


def _prep_kernel(x_ref, g_ref, cos_ref, sin_ref, o_ref, *, heads, norm, rope, scale):
    for hh in range(heads):
        cols = slice(hh * HEAD_DIM, (hh + 1) * HEAD_DIM)
        x = x_ref[:, cols].astype(F32)
        if norm:
            x = x * lax.rsqrt(jnp.mean(x * x, -1, keepdims=True) + LN_EPS) * g_ref[...]
        if rope:
            lane = lax.broadcasted_iota(jnp.int32, x.shape, 1)
            partner = jnp.where(lane % 2 == 0, pltpu.roll(x, HEAD_DIM - 1, 1), pltpu.roll(x, 1, 1))
            x = x * cos_ref[...] + partner * sin_ref[...]
        if scale != 1.0:
            x = x * scale
        o_ref[:, cols] = x.astype(o_ref.dtype)


def _prep(p, col0, n_heads, gain, cos_i, sin_s, *, norm=False, rope=False, scale=1.0):
    t = p.shape[0]
    hb = 4 if n_heads % 4 == 0 else 1
    bw = hb * HEAD_DIM
    assert col0 % bw == 0
    tt = min(t, 1024)
    c0 = col0 // bw
    return pl.pallas_call(
        functools.partial(_prep_kernel, heads=hb, norm=norm, rope=rope, scale=scale),
        grid=(t // tt, n_heads // hb),
        in_specs=[pl.BlockSpec((tt, bw), lambda i, j: (i, c0 + j)),
                  pl.BlockSpec((1, HEAD_DIM), lambda i, j: (0, 0)),
                  pl.BlockSpec((tt, HEAD_DIM), lambda i, j: (i, 0)),
                  pl.BlockSpec((tt, HEAD_DIM), lambda i, j: (i, 0))],
        out_specs=pl.BlockSpec((tt, bw), lambda i, j: (i, j)),
        out_shape=jax.ShapeDtypeStruct((t, n_heads * HEAD_DIM), BF16),
        compiler_params=_params("parallel", "parallel"),
    )(p, gain.reshape(1, HEAD_DIM).astype(F32), cos_i[:t], sin_s[:t])


def _rope_tables(n_tokens):
    t = jnp.arange(n_tokens, dtype=jnp.int32)
    row = (t // GRID_W).astype(F32)
    col = (t % GRID_W).astype(F32)
    n_freq = HEAD_DIM // 4
    inv = ROPE_THETA ** (-jnp.arange(n_freq, dtype=F32) / n_freq)
    ang = jnp.concatenate([row[:, None] * inv, col[:, None] * inv], -1)
    cos, sin = jnp.cos(ang), jnp.sin(ang)
    cos_i = jnp.repeat(cos, 2, axis=-1)
    sin_s = jnp.stack([-sin, sin], -1).reshape(n_tokens, HEAD_DIM)
    return cos_i, sin_s


KV_BLOCKS_PER_ITER = 2


def _flash_kernel(q_ref, k_ref, v_ref, *rest, group, tk, scale, n_side):
    side_in, o_ref, side_out = rest[:n_side], rest[n_side], rest[n_side + 1:2 * n_side + 1]
    m_ref, l_ref, acc_ref = rest[2 * n_side + 1:]
    _SideCast.run(side_in, side_out)
    n_blocks = k_ref.shape[0] // tk
    m_ref[...] = jnp.full_like(m_ref, NEG_INF)
    l_ref[...] = jnp.zeros_like(l_ref)
    acc_ref[...] = jnp.zeros_like(acc_ref)

    def block(start):
        k = k_ref[pl.ds(start, tk), :]
        v = v_ref[pl.ds(start, tk), :]
        for g in range(group):
            q = q_ref[:, g * HEAD_DIM:(g + 1) * HEAD_DIM]
            if scale != 1.0:
                q = (q.astype(F32) * scale).astype(BF16)
            s = lax.dot_general(q, k, _NT, preferred_element_type=F32)
            m_prev = m_ref[g]
            m_new = jnp.maximum(m_prev, jnp.max(s, -1, keepdims=True))
            alpha = jnp.exp2(m_prev - m_new)
            p = jnp.exp2(s - jnp.concatenate([m_new] * (tk // HEAD_DIM), 1))
            l_ref[g] = alpha * l_ref[g] + jnp.sum(p, -1, keepdims=True)
            acc_ref[g] = alpha * acc_ref[g] + jnp.dot(p.astype(BF16), v, preferred_element_type=F32)
            m_ref[g] = m_new

    def group_of_blocks(i, carry):
        for b in range(KV_BLOCKS_PER_ITER):
            block(pl.multiple_of((i * KV_BLOCKS_PER_ITER + b) * tk, tk))
        return carry

    n_iter = n_blocks // KV_BLOCKS_PER_ITER
    lax.fori_loop(0, n_iter, group_of_blocks, 0)
    for b in range(n_iter * KV_BLOCKS_PER_ITER, n_blocks):
        block(b * tk)
    for g in range(group):
        o_ref[:, g * HEAD_DIM:(g + 1) * HEAD_DIM] = (acc_ref[g] / l_ref[g]).astype(o_ref.dtype)


def _flash(q, q_col0, k, k_col0, v, v_col0, *, kv_heads, group, tq, tk, scale=1.0, cast_tables=(), cast_layer=0):
    n_q, n_k = q.shape[0], k.shape[0]
    tq, tk = min(tq, n_q), min(tk, n_k)
    assert n_q % tq == 0 and n_k % tk == 0 and tk % HEAD_DIM == 0
    qw = group * HEAD_DIM
    assert q_col0 % qw == 0 and k_col0 % HEAD_DIM == 0 and v_col0 % HEAD_DIM == 0
    qc, kc, vc = q_col0 // qw, k_col0 // HEAD_DIM, v_col0 // HEAD_DIM
    n_qt = n_q // tq
    side = _SideCast(cast_tables, cast_layer, kv_heads * n_qt, lambda h, i: h * n_qt + i)
    res = pl.pallas_call(
        functools.partial(_flash_kernel, group=group, tk=tk, scale=scale, n_side=side.n),
        grid=(kv_heads, n_qt),
        in_specs=[pl.BlockSpec((tq, qw), lambda h, i: (i, qc + h)),
                  pl.BlockSpec((n_k, HEAD_DIM), lambda h, i: (0, kc + h)),
                  pl.BlockSpec((n_k, HEAD_DIM), lambda h, i: (0, vc + h))] + side.in_specs,
        out_specs=[pl.BlockSpec((tq, qw), lambda h, i: (i, h))] + side.out_specs,
        out_shape=[jax.ShapeDtypeStruct((n_q, kv_heads * qw), BF16)] + side.out_shapes,
        scratch_shapes=[pltpu.VMEM((group, tq, HEAD_DIM), F32),
                        pltpu.VMEM((group, tq, HEAD_DIM), F32),
                        pltpu.VMEM((group, tq, HEAD_DIM), F32)],
        compiler_params=_params("parallel", "parallel"),
    )(q, k, v, *side.tables)
    return res if side.n else res[0]


def _na_bias_tables(rpb, rows):
    n_heads = rpb.shape[0]
    rg, w, kr, kc = NA_ROWS_PER_STEP, GRID_W, NA_WIN_R, NA_WIN_C
    n_steps = rows // rg
    assert rows % rg == 0 and n_steps >= 3 and kr == 2 * rg and rows >= kr
    c = np.arange(w)
    dc = np.clip(c[None, :] - c[:, None] + (kc - 1), 0, 2 * kc - 2)
    oh_c = (np.arange(2 * kc - 1)[:, None, None] == dc[None]).astype(np.float32)
    rl, krl = np.arange(rg), np.arange(3 * rg)
    da = krl[None, :] - rl[:, None] - rg + (kr - 1)
    assert da.min() >= 0 and da.max() <= 2 * kr - 2
    oh_a = (np.arange(2 * kr - 1)[:, None, None] == da[None]).astype(np.float32)
    t1 = jnp.einsum('hab,bck->hack', rpb.astype(F32), oh_c, precision=lax.Precision.HIGHEST)
    bias = jnp.einsum('hack,arl->hrclk', t1, oh_a, precision=lax.Precision.HIGHEST)
    bias = bias.reshape(n_heads, rg * w, 3 * rg * w)
    c0 = np.clip(c - kc // 2, 0, w - kc)
    col_ok = (c[None, :] >= c0[:, None]) & (c[None, :] < c0[:, None] + kc)
    masks = []
    for g in (0, 1, n_steps - 1):
        r = rg * g + rl
        r0 = np.clip(r - kr // 2, 0, rows - kr)
        key_row = rg * (g - 1) + krl
        row_ok = (key_row[None, :] >= r0[:, None]) & (key_row[None, :] < r0[:, None] + kr)
        ok = row_ok[:, None, :, None] & col_ok[None, :, None, :]
        masks.append(ok.reshape(rg * w, 3 * rg * w))
    masks = np.stack(masks)
    return jnp.where(masks[:, None], bias[None] * LOG2E, NEG_INF)


def _na_kernel(q_ref, kp_ref, kc_ref, kn_ref, vp_ref, vc_ref, vn_ref, kx_ref, vx_ref, b_ref, o_ref, *, heads, scale):
    tq = q_ref.shape[0]
    for hh in range(heads):
        cols = slice(hh * HEAD_DIM, (hh + 1) * HEAD_DIM)
        q = (q_ref[:, cols].astype(F32) * scale).astype(BF16)
        ss = []
        for idx, k_ref in enumerate((kp_ref, kc_ref, kn_ref)):
            s = lax.dot_general(q, k_ref[:, cols], _NT, preferred_element_type=F32)
            ss.append(s + b_ref[0, hh, :, idx * tq:(idx + 1) * tq])
        ss.append(lax.dot_general(q, kx_ref[:, cols], _NT, preferred_element_type=F32))
        m = functools.reduce(jnp.maximum, [jnp.max(s, -1, keepdims=True) for s in ss])
        ps = [jnp.exp2(s - m) for s in ss]
        l = functools.reduce(jnp.add, [jnp.sum(p, -1, keepdims=True) for p in ps])
        acc = None
        for p, v_ref in zip(ps, (vp_ref, vc_ref, vn_ref, vx_ref)):
            pv = jnp.dot(p.astype(BF16), v_ref[:, cols], preferred_element_type=F32)
            acc = pv if acc is None else acc + pv
        o_ref[:, cols] = (acc / l).astype(o_ref.dtype)


def _neighbourhood_attention(p, p_ctx, bias, n_heads, q_col0, k_col0, v_col0):
    s_len = p.shape[0]
    tq = NA_ROWS_PER_STEP * GRID_W
    n_steps = s_len // tq
    hb = 4 if n_heads % 4 == 0 else 1
    bw = hb * HEAD_DIM
    assert p_ctx.shape[0] == tq
    qc, kc, vc = q_col0 // bw, k_col0 // bw, v_col0 // bw
    prev = lambda g: jnp.maximum(g - 1, 0)
    nxt = lambda g: jnp.minimum(g + 1, n_steps - 1)
    variant = lambda g: jnp.where(g == 0, 0, jnp.where(g == n_steps - 1, 2, 1))
    lat = lambda col, row: pl.BlockSpec((tq, bw), lambda h, g: (row(g), col + h))
    ident = lambda g: g
    return pl.pallas_call(
        functools.partial(_na_kernel, heads=hb, scale=HEAD_DIM ** -0.5 * LOG2E),
        grid=(n_heads // hb, n_steps),
        in_specs=[lat(qc, ident),
                  lat(kc, prev), lat(kc, ident), lat(kc, nxt),
                  lat(vc, prev), lat(vc, ident), lat(vc, nxt),
                  pl.BlockSpec((tq, bw), lambda h, g: (0, kc + h)),
                  pl.BlockSpec((tq, bw), lambda h, g: (0, vc + h)),
                  pl.BlockSpec((1, hb, tq, 3 * tq), lambda h, g: (variant(g), h, 0, 0))],
        out_specs=pl.BlockSpec((tq, bw), lambda h, g: (g, h)),
        out_shape=jax.ShapeDtypeStruct((s_len, n_heads * HEAD_DIM), BF16),
        compiler_params=_params("parallel", "arbitrary"),
    )(p, p, p, p, p, p, p, p_ctx, p_ctx, bias)


def _swa_kernel(sink_ref, q_ref, kp_ref, kc_ref, kn_ref, vp_ref, vc_ref, vn_ref, kx_ref, vx_ref, o_ref, *, group, window):
    h, n = pl.program_id(0), pl.program_id(1)
    tq = q_ref.shape[0]
    q = jnp.concatenate([q_ref[:, g * HEAD_DIM:(g + 1) * HEAD_DIM] for g in range(group)], 0)
    ql = lax.broadcasted_iota(jnp.int32, (group * tq, tq), 0) % tq
    kl = lax.broadcasted_iota(jnp.int32, (group * tq, tq), 1)
    diff = kl - ql
    ok_p = (diff - tq >= -window) & (n > 0)
    ok_c = (diff <= window) & (diff >= -window)
    ok_n = (diff + tq <= window) & (n < pl.num_programs(1) - 1)
    ss = []
    for k_ref, ok in ((kp_ref, ok_p), (kc_ref, ok_c), (kn_ref, ok_n)):
        s = lax.dot_general(q, k_ref[...], _NT, preferred_element_type=F32)
        ss.append(jnp.where(ok, s, NEG_INF))
    ss.append(lax.dot_general(q, kx_ref[...], _NT, preferred_element_type=F32))
    row_g = lax.broadcasted_iota(jnp.int32, (group * tq, 1), 0) // tq
    sink = jnp.zeros((group * tq, 1), F32)
    for g in range(group):
        sink = jnp.where(row_g == g, sink_ref[h * group + g] * LOG2E, sink)
    m = functools.reduce(jnp.maximum, [jnp.max(s, -1, keepdims=True) for s in ss] + [sink])
    ps = [jnp.exp2(s - m) for s in ss]
    l = functools.reduce(jnp.add, [jnp.sum(p, -1, keepdims=True) for p in ps]) + jnp.exp2(sink - m)
    acc = None
    for p, v_ref in zip(ps, (vp_ref, vc_ref, vn_ref, vx_ref)):
        pv = jnp.dot(p.astype(BF16), v_ref[...], preferred_element_type=F32)
        acc = pv if acc is None else acc + pv
    out = acc / l
    for g in range(group):
        o_ref[:, g * HEAD_DIM:(g + 1) * HEAD_DIM] = out[g * tq:(g + 1) * tq].astype(o_ref.dtype)


def _sliding_window_attention(q, k, v, v_col0, kx, kx_col0, vx, vx_col0, sink, *, kv_heads, group):
    s_len = q.shape[0]
    tq = SW_BLOCK
    assert s_len % tq == 0 and tq >= SW_WINDOW and kx.shape[0] == vx.shape[0]
    n_blk = s_len // tq
    n_ctx = kx.shape[0]
    qw = group * HEAD_DIM
    vc, kxc, vxc = v_col0 // HEAD_DIM, kx_col0 // HEAD_DIM, vx_col0 // HEAD_DIM
    prev = lambda n: jnp.maximum(n - 1, 0)
    nxt = lambda n: jnp.minimum(n + 1, n_blk - 1)
    ident = lambda n: n
    blk = lambda col, row: pl.BlockSpec((tq, HEAD_DIM), lambda h, n: (row(n), col + h))
    return pl.pallas_call(
        functools.partial(_swa_kernel, group=group, window=SW_WINDOW),
        grid=(kv_heads, n_blk),
        in_specs=[pl.BlockSpec(memory_space=pltpu.SMEM),
                  pl.BlockSpec((tq, qw), lambda h, n: (n, h)),
                  blk(0, prev), blk(0, ident), blk(0, nxt),
                  blk(vc, prev), blk(vc, ident), blk(vc, nxt),
                  pl.BlockSpec((n_ctx, HEAD_DIM), lambda h, n: (0, kxc + h)),
                  pl.BlockSpec((n_ctx, HEAD_DIM), lambda h, n: (0, vxc + h))],
        out_specs=pl.BlockSpec((tq, qw), lambda h, n: (n, h)),
        out_shape=jax.ShapeDtypeStruct((s_len, kv_heads * qw), BF16),
        compiler_params=_params("parallel", "arbitrary"),
    )(sink.astype(F32), q, k, k, k, v, v, v, kx, vx)


def _ret_kernel(cd_ref, qf_ref, kf_ref, vf_ref, qb_ref, kb_ref, vb_ref, kx_ref, vx_ref,
                intra_ref, qd_ref, kd_ref, wx_ref, of_ref, ob_ref, st_ref, *, heads):
    hb, n = pl.program_id(0), pl.program_id(1)

    @pl.when(n == 0)
    def _():
        for hh in range(heads):
            cols = slice(hh * HEAD_DIM, (hh + 1) * HEAD_DIM)
            for d in range(2):
                kw = (kx_ref[:, cols].astype(F32) * wx_ref[d, hh]).astype(BF16)
                st_ref[d, hh] = lax.dot_general(kw, vx_ref[:, cols], _TN, preferred_element_type=F32)

    for hh in range(heads):
        cols = slice(hh * HEAD_DIM, (hh + 1) * HEAD_DIM)
        head = hb * heads + hh
        for d, (q_ref, k_ref, v_ref, o_ref) in enumerate(((qf_ref, kf_ref, vf_ref, of_ref),
                                                          (qb_ref, kb_ref, vb_ref, ob_ref))):
            q, k, v = q_ref[:, cols], k_ref[:, cols], v_ref[:, cols]
            state = st_ref[d, hh]
            inner = lax.dot_general(q, k, _NT, preferred_element_type=F32) * intra_ref[d, hh]
            qs = (q.astype(F32) * qd_ref[d, hh]).astype(BF16)
            o_ref[:, cols] = (jnp.dot(inner.astype(BF16), v, preferred_element_type=F32)
                              + jnp.dot(qs, state.astype(BF16), preferred_element_type=F32))
            ks = (k.astype(F32) * kd_ref[d, hh]).astype(BF16)
            st_ref[d, hh] = state * cd_ref[d, head] + lax.dot_general(ks, v, _TN, preferred_element_type=F32)


def _retention(rq, rk, p, v_col0, rkx, p_ctx, log_gamma, n_heads):
    s_len = rq.shape[0]
    c = RET_CHUNK
    n_chunks = s_len // c
    n_ctx = rkx.shape[0]
    hb = 8 if n_heads % 8 == 0 else (4 if n_heads % 4 == 0 else 1)
    bw = hb * HEAD_DIM
    assert v_col0 % bw == 0
    vc = v_col0 // bw
    lg = log_gamma.astype(F32)
    pos = jnp.arange(c, dtype=F32)
    diff = pos[:, None] - pos[None, :]
    intra_f = jnp.where(diff >= 0, jnp.exp(lg[0][:, None, None] * jnp.maximum(diff, 0.0)), 0.0)
    intra_b = jnp.where(diff <= 0, jnp.exp(lg[1][:, None, None] * jnp.maximum(-diff, 0.0)), 0.0)
    intra = jnp.stack([intra_f, intra_b])
    lanes = lambda t: jnp.broadcast_to(t[..., None], t.shape + (HEAD_DIM,))
    qd = lanes(jnp.stack([jnp.exp(lg[0][:, None] * (pos + 1.0)), jnp.exp(lg[1][:, None] * (c - pos))]))
    kd = lanes(jnp.stack([jnp.exp(lg[0][:, None] * (c - 1.0 - pos)), jnp.exp(lg[1][:, None] * pos)]))
    cd = jnp.exp(lg * c)
    jx = jnp.arange(n_ctx, dtype=F32)
    wx = lanes(jnp.stack([jnp.exp(lg[0][:, None] * (n_ctx - 1.0 - jx)), jnp.exp(lg[1][:, None] * jx)]))
    fwd = lambda col: pl.BlockSpec((c, bw), lambda h, n: (n, col + h))
    bwd = lambda col: pl.BlockSpec((c, bw), lambda h, n: (n_chunks - 1 - n, col + h))
    tab = lambda rows: pl.BlockSpec((2, hb, rows, HEAD_DIM), lambda h, n: (0, h, 0, 0))
    out_sds = jax.ShapeDtypeStruct((s_len, n_heads * HEAD_DIM), F32)
    return pl.pallas_call(
        functools.partial(_ret_kernel, heads=hb),
        grid=(n_heads // hb, n_chunks),
        in_specs=[pl.BlockSpec(memory_space=pltpu.SMEM),
                  fwd(0), fwd(0), fwd(vc), bwd(0), bwd(0), bwd(vc),
                  pl.BlockSpec((n_ctx, bw), lambda h, n: (0, h)),
                  pl.BlockSpec((n_ctx, bw), lambda h, n: (0, vc + h)),
                  tab(c), tab(c), tab(c), tab(n_ctx)],
        out_specs=[fwd(0), bwd(0)],
        out_shape=[out_sds, out_sds],
        scratch_shapes=[pltpu.VMEM((2, hb, HEAD_DIM, HEAD_DIM), F32)],
        compiler_params=_params("parallel", "arbitrary"),
    )(cd, rq, rk, p, rq, rk, p, rkx, p_ctx, intra, qd, kd, wx)


def _ret_out_kernel(of_ref, ob_ref, g_ref, o_ref, *, heads):
    for hh in range(heads):
        cols = slice(hh * HEAD_DIM, (hh + 1) * HEAD_DIM)
        o = of_ref[:, cols] + ob_ref[:, cols]
        mu = jnp.mean(o, -1, keepdims=True)
        var = jnp.mean(jnp.square(o - mu), -1, keepdims=True)
        gate = g_ref[:, cols].astype(F32)
        gate = gate * (1.0 / (1.0 + jnp.exp(-gate)))
        o_ref[:, cols] = (gate * ((o - mu) * lax.rsqrt(var + LN_EPS))).astype(o_ref.dtype)


def _retention_output(o_f, o_b, p, gate_col0, n_heads):
    s_len = o_f.shape[0]
    hb = 4 if n_heads % 4 == 0 else 1
    bw = hb * HEAD_DIM
    gc = gate_col0 // bw
    tt = min(s_len, 512)
    spec = pl.BlockSpec((tt, bw), lambda i, j: (i, j))
    return pl.pallas_call(
        functools.partial(_ret_out_kernel, heads=hb),
        grid=(s_len // tt, n_heads // hb),
        in_specs=[spec, spec, pl.BlockSpec((tt, bw), lambda i, j: (i, gc + j))],
        out_specs=spec,
        out_shape=jax.ShapeDtypeStruct((s_len, n_heads * HEAD_DIM), BF16),
        compiler_params=_params("parallel", "parallel"),
    )(o_f, o_b, p)


def _residual_ln(x, y, gate, g, b, alpha):
    h = alpha * x + gate * y
    mu = jnp.mean(h, -1, keepdims=True)
    var = jnp.mean(jnp.square(h - mu), -1, keepdims=True)
    return (h - mu) * lax.rsqrt(var + LN_EPS) * g + b


def _outproj_kernel(ya_ref, yb_ref, w_ref, x_ref, gate_ref, g_ref, b_ref, o_ref, *, alpha):
    wa = ya_ref.shape[1]
    o_ref[...] = (jnp.dot(ya_ref[...], w_ref[:wa, :], preferred_element_type=F32)
                  + jnp.dot(yb_ref[...], w_ref[wa:, :], preferred_element_type=F32))
    for rows in _row_chunks(x_ref.shape[0], 64):
        o_ref[rows, :] = _residual_ln(x_ref[rows, :], o_ref[rows, :], gate_ref[...], g_ref[...], b_ref[...], alpha)


def _outproj_ln(ya, yb, w, x, gate, g, b, alpha, tm):
    m, wa = ya.shape
    wb = yb.shape[1]
    d = w.shape[1]
    tm = min(tm, m)
    vec = pl.BlockSpec((1, d), lambda i: (0, 0))
    return pl.pallas_call(
        functools.partial(_outproj_kernel, alpha=alpha),
        grid=(m // tm,),
        in_specs=[pl.BlockSpec((tm, wa), lambda i: (i, 0)),
                  pl.BlockSpec((tm, wb), lambda i: (i, 0)),
                  pl.BlockSpec((wa + wb, d), lambda i: (0, 0), pipeline_mode=pl.Buffered(1)),
                  pl.BlockSpec((tm, d), lambda i: (i, 0)),
                  vec, vec, vec],
        out_specs=pl.BlockSpec((tm, d), lambda i: (i, 0)),
        out_shape=jax.ShapeDtypeStruct((m, d), F32),
        compiler_params=_params("parallel", vmem=VMEM_LIMIT_RESIDENT_BYTES),
    )(ya, yb, w, x, gate, g.reshape(1, d), b.reshape(1, d))


def _add_ln_kernel(x_ref, f_ref, gate_ref, g_ref, b_ref, o_ref, *, alpha):
    o_ref[...] = _residual_ln(x_ref[...], f_ref[...].astype(F32), gate_ref[...], g_ref[...], b_ref[...], alpha)


def _add_ln(x, f, gate, g, b, alpha):
    m, d = x.shape
    tt = min(m, 256)
    row = pl.BlockSpec((tt, d), lambda i: (i, 0))
    vec = pl.BlockSpec((1, d), lambda i: (0, 0))
    return pl.pallas_call(
        functools.partial(_add_ln_kernel, alpha=alpha),
        grid=(m // tt,),
        in_specs=[row, row, vec, vec, vec],
        out_specs=row,
        out_shape=jax.ShapeDtypeStruct((m, d), F32),
        compiler_params=_params("parallel"),
    )(x, f, gate, g.reshape(1, d), b.reshape(1, d))


PEER_LIST = 24


def _peer_gate_kernel(q_ref, keys_ref, d1_ref, f1_ref, s2_ref, e2_ref, a_ref, b_ref, cand_ref):
    half = keys_ref.shape[-1]
    k = PEER_TOPK
    s1 = lax.dot_general(keys_ref[0, 0], q_ref[:, :half].astype(BF16), _NT, preferred_element_type=F32)
    s2 = lax.dot_general(keys_ref[0, 1], q_ref[:, half:].astype(BF16), _NT, preferred_element_type=F32)

    def top_rows(s, dst_ref):
        dst_ref[...] = jnp.full_like(dst_ref, -jnp.inf)
        cur = s
        for r in range(k + 1):
            m = jnp.max(cur, 0, keepdims=True)
            dst_ref[r:r + 1, :] = m
            cur = jnp.where(cur == m, -jnp.inf, cur)

    top_rows(s1, a_ref)
    top_rows(s2, b_ref)
    half_k = k // 2
    cand_ref[0:PEER_LIST, :] = a_ref[0:1, :] + b_ref[...]
    for r in range(1, half_k):
        cand_ref[PEER_LIST + (r - 1) * half_k:PEER_LIST + r * half_k, :] = a_ref[r:r + 1, :] + b_ref[0:half_k, :]
    cand_ref[PEER_LIST + (half_k - 1) * half_k:, :] = a_ref[half_k:, :] + b_ref[0:1, :]
    best = a_ref[0:1, :] + b_ref[0:1, :]
    cur = cand_ref[...]
    z = jnp.zeros_like(best)
    kth = best
    for r in range(k):
        kth = jnp.max(cur, 0, keepdims=True)
        z = z + jnp.exp(kth - best)
        cur = jnp.where(cur == kth, -jnp.inf, cur)
    tau = 0.5 * (kth + jnp.max(cur, 0, keepdims=True))
    d1_ref[0] = tau - s1
    f1_ref[0] = jnp.exp(s1 - a_ref[0:1, :]) / z
    s2_ref[0] = s2
    e2_ref[0] = jnp.exp(s2 - b_ref[0:1, :])


def _peer_gates(q, sub_keys, tm):
    t = q.shape[0]
    n_heads, _, n_keys, half = sub_keys.shape
    tm = min(tm, t)
    big = jax.ShapeDtypeStruct((n_heads, n_keys, t), F32)
    blk = pl.BlockSpec((1, n_keys, tm), lambda i, h: (h, 0, i))
    return pl.pallas_call(
        _peer_gate_kernel,
        grid=(t // tm, n_heads),
        in_specs=[pl.BlockSpec((tm, 2 * half), lambda i, h: (i, h)),
                  pl.BlockSpec((1, 2, n_keys, half), lambda i, h: (h, 0, 0, 0))],
        out_specs=[blk, blk, blk, blk],
        out_shape=[big, big, big, big],
        scratch_shapes=[pltpu.VMEM((PEER_LIST, tm), F32), pltpu.VMEM((PEER_LIST, tm), F32),
                        pltpu.VMEM((2 * PEER_LIST - PEER_TOPK // 2 + (PEER_TOPK // 2 - 1) * (PEER_TOPK // 2), tm),
                                   F32)],
        compiler_params=_params("parallel", "parallel"),
    )(q, sub_keys)


def _gelu_tanh(x):
    c = math.sqrt(2.0 / math.pi)
    hx = 0.5 * x
    return hx + hx * jnp.tanh(x * (c + (c * 0.044715) * (x * x)))


def _peer_kernel(h_ref, u_ref, v_ref, d1_ref, f1_ref, s2_ref, e2_ref, *rest, n_side):
    side_in, o_ref, side_out = rest[:n_side], rest[n_side], rest[n_side + 1:2 * n_side + 1]
    acc_ref, g_ref = rest[2 * n_side + 1:]
    _SideCast.run(side_in, side_out)
    cj, ci = pl.program_id(1), pl.program_id(2)
    n_heads, ni, tm = d1_ref.shape
    nj = s2_ref.shape[1]
    d = h_ref.shape[1]

    @pl.when((cj == 0) & (ci == 0))
    def _():
        acc_ref[...] = jnp.zeros_like(acc_ref)

    rows = 32
    for ii in range(ni):
        for r0 in range(0, nj, rows):
            for t0 in range(0, tm, 128):
                tok = slice(t0, t0 + 128)
                g = None
                for h in range(n_heads):
                    w = f1_ref[h, ii:ii + 1, tok] * e2_ref[h, r0:r0 + rows, tok]
                    sel = jnp.where(s2_ref[h, r0:r0 + rows, tok] >= d1_ref[h, ii:ii + 1, tok], w, 0.0)
                    g = sel if g is None else g + sel
                g_ref[ii * nj + r0:ii * nj + r0 + rows, tok] = g
    u = u_ref[...].reshape(ni * nj, d)
    act = _gelu_tanh(lax.dot_general(u, h_ref[...], _NT, preferred_element_type=F32))
    wt = (g_ref[...] * act).astype(BF16)
    acc_ref[...] += lax.dot_general(wt, v_ref[...].reshape(ni * nj, d), _TN, preferred_element_type=F32)

    @pl.when((cj == pl.num_programs(1) - 1) & (ci == pl.num_programs(2) - 1))
    def _():
        o_ref[...] = acc_ref[...].astype(o_ref.dtype)


def _peer_dense(hf, u3, v3, d1, f1, s2, e2, tm, cast_tables=(), cast_layer=0):
    t, d = hf.shape
    n_heads, n_keys = d1.shape[0], d1.shape[1]
    tm = min(tm, t)
    ni, nj = PEER_NI, PEER_NJ
    n_cj, n_ci = n_keys // nj, n_keys // ni
    tab = pl.BlockSpec((ni, nj, d), lambda i, cj, ci: (ci, cj, 0))
    first = pl.BlockSpec((n_heads, ni, tm), lambda i, cj, ci: (0, ci, i))
    second = pl.BlockSpec((n_heads, nj, tm), lambda i, cj, ci: (0, cj, i))
    side = _SideCast(cast_tables, cast_layer, (t // tm) * n_cj * n_ci, lambda i, cj, ci: (i * n_cj + cj) * n_ci + ci)
    res = pl.pallas_call(
        functools.partial(_peer_kernel, n_side=side.n),
        grid=(t // tm, n_cj, n_ci),
        in_specs=[pl.BlockSpec((tm, d), lambda i, cj, ci: (i, 0)), tab, tab, first, first, second, second]
        + side.in_specs,
        out_specs=[pl.BlockSpec((tm, d), lambda i, cj, ci: (i, 0))] + side.out_specs,
        out_shape=[jax.ShapeDtypeStruct((t, d), BF16)] + side.out_shapes,
        scratch_shapes=[pltpu.VMEM((tm, d), F32), pltpu.VMEM((ni * nj, tm), F32)],
        compiler_params=_params("parallel", "arbitrary", "arbitrary"),
    )(hf, u3, v3, d1, f1, s2, e2, *side.tables)
    return res if side.n else res[0]


def _peer_block(x, shift, scale, gate, w_q, sub_keys, u3, v3, g, b, alpha, tm, cast_tables=(), cast_layer=0):
    q, hf = _proj(x, shift, scale, w_q, F32, tm, emit_h=True, resident_w=True)
    d1, f1, s2, e2 = _peer_gates(q, sub_keys, tm)
    res = _peer_dense(hf, u3, v3, d1, f1, s2, e2, tm, cast_tables, cast_layer)
    f, casts = (res[0], tuple(res[1:])) if cast_tables else (res, ())
    return _add_ln(x, f, gate, g, b, alpha), casts


def kernel(x, c, ctx, c_ctx, ada_w, ada_b, ln_g, ln_b, ev_w_in, ev_w_out, ev_na_rpb, ev_ga_q_gain, ev_ga_k_gain,
           od_w_in, od_w_out, od_ret_decay_exp, od_sw_sink, peer_w_q, peer_sub_keys, peer_u, peer_v):
    assert x.shape[0] == 1 and ada_w.shape[0] == 2
    depth = ada_w.shape[0]
    _, s_len, d = x.shape
    hd = HEAD_DIM
    mix_heads = d // (2 * hd)
    kv_heads = mix_heads // 4
    group = mix_heads // kv_heads
    alpha = (2 * depth) ** 0.25
    att_scale = hd ** -0.5
    sm_scale = att_scale * LOG2E
    n_keys = peer_sub_keys.shape[3]
    tm = 512
    tm_out = 256

    xs, cs = x[0], ctx[0]
    mods = _ada_modulation(c, c_ctx, ada_w, ada_b)
    cos_i, sin_s = _rope_tables(s_len)
    ones = jnp.ones((hd,), F32)

    def mod6(layer, which):
        return [m.reshape(1, d) for m in jnp.split(mods[layer, which], 6)]

    def peer_args(layer, u_bf16, v_bf16):
        return (_to_bf16(peer_w_q, layer), peer_sub_keys[layer].astype(BF16),
                u_bf16.reshape(n_keys, n_keys, d), v_bf16.reshape(n_keys, n_keys, d),
                ln_g[layer, 1], ln_b[layer, 1], alpha)

    expert_tables = (peer_u, peer_v)

    sh_a, sc_a, g_a, sh_f, sc_f, g_f = mod6(0, 0)
    csh_a, csc_a, cg_a, csh_f, csc_f, cg_f = mod6(0, 1)
    w_in = _to_bf16(ev_w_in, 0)
    p = _proj(xs, sh_a, sc_a, w_in, BF16, tm)
    pc = _proj(cs, csh_a, csc_a, w_in, BF16, tm)
    mh = mix_heads * hd
    kvw = kv_heads * hd
    na_q, na_k, na_v, ga_q, ga_k, ga_v = 0, mh, 2 * mh, 3 * mh, 4 * mh, 4 * mh + kvw
    qg = _prep(p, ga_q, mix_heads, ev_ga_q_gain[0], cos_i, sin_s, norm=True, rope=True, scale=sm_scale)
    kg = _prep(p, ga_k, kv_heads, ev_ga_k_gain[0], cos_i, sin_s, norm=True, rope=True)
    qgc = _prep(pc, ga_q, mix_heads, ev_ga_q_gain[0], cos_i, sin_s, norm=True, scale=sm_scale)
    kgc = _prep(pc, ga_k, kv_heads, ev_ga_k_gain[0], cos_i, sin_s, norm=True)
    k_all = jnp.concatenate([kg, kgc], 0)
    v_all = jnp.concatenate([p[:, ga_v:ga_v + kvw], pc[:, ga_v:ga_v + kvw]], 0)
    tk = _pick(k_all.shape[0], 1024, 256)
    y_ga, u0, v0 = _flash(qg, 0, k_all, 0, v_all, 0, kv_heads=kv_heads, group=group, tq=512, tk=tk,
                          cast_tables=expert_tables, cast_layer=0)
    yc_ga = _flash(qgc, 0, kgc, 0, pc, ga_v, kv_heads=kv_heads, group=group, tq=256, tk=256)
    bias = _na_bias_tables(ev_na_rpb[0], s_len // GRID_W)
    y_na = _neighbourhood_attention(p, pc, bias, mix_heads, na_q, na_k, na_v)
    yc_na = _flash(pc, na_q, pc, na_k, pc, na_v, kv_heads=mix_heads, group=1, tq=256, tk=256, scale=sm_scale)
    w_out = _to_bf16(ev_w_out, 0)
    xs = _outproj_ln(y_na, y_ga, w_out, xs, g_a, ln_g[0, 0], ln_b[0, 0], alpha, tm_out)
    cs = _outproj_ln(yc_na, yc_ga, w_out, cs, cg_a, ln_g[0, 0], ln_b[0, 0], alpha, tm_out)
    pa = peer_args(0, u0, v0)
    xs, (u1, v1) = _peer_block(xs, sh_f, sc_f, g_f, *pa, tm, cast_tables=expert_tables, cast_layer=1)
    cs, _ = _peer_block(cs, csh_f, csc_f, cg_f, *pa, tm)

    sh_a, sc_a, g_a, sh_f, sc_f, g_f = mod6(1, 0)
    csh_a, csc_a = mod6(1, 1)[:2]
    w_in = _to_bf16(od_w_in, 0)
    p = _proj(xs, sh_a, sc_a, w_in, BF16, tm)
    pc = _proj(cs, csh_a, csc_a, w_in, BF16, tm)
    r_q, r_k, r_v, r_g, s_q, s_k, s_v = 0, mh, 2 * mh, 3 * mh, 4 * mh, 5 * mh, 5 * mh + kvw
    rq = _prep(p, r_q, mix_heads, ones, cos_i, sin_s, rope=True)
    rk = _prep(p, r_k, mix_heads, ones, cos_i, sin_s, rope=True, scale=att_scale)
    rkc = _prep(pc, r_k, mix_heads, ones, cos_i, sin_s, scale=att_scale)
    sq = _prep(p, s_q, mix_heads, ones, cos_i, sin_s, rope=True, scale=sm_scale)
    sk = _prep(p, s_k, kv_heads, ones, cos_i, sin_s, rope=True)
    log_gamma = jnp.log1p(-jnp.exp2(-od_ret_decay_exp[0].astype(F32)))
    o_f, o_b = _retention(rq, rk, p, r_v, rkc, pc, log_gamma, mix_heads)
    y_ret = _retention_output(o_f, o_b, p, r_g, mix_heads)
    y_sw = _sliding_window_attention(sq, sk, p, s_v, pc, s_k, pc, s_v, od_sw_sink[0],
                                     kv_heads=kv_heads, group=group)
    xs = _outproj_ln(y_ret, y_sw, _to_bf16(od_w_out, 0), xs, g_a,
                     ln_g[1, 0], ln_b[1, 0], alpha, tm_out)
    xs, _ = _peer_block(xs, sh_f, sc_f, g_f, *peer_args(1, u1, v1), tm)
    return xs[None]
```

```python
import functools
import math

import numpy as np
import jax
import jax.numpy as jnp
from jax import lax
from jax.experimental import pallas as pl
from jax.experimental.pallas import tpu as pltpu

F32 = jnp.float32
BF16 = jnp.bfloat16

GRID_W = 64
HEAD_DIM = 128
NA_WIN_R = 8
NA_WIN_C = 16
NA_ROWS_PER_STEP = 4
SW_WINDOW = 128
SW_BLOCK = 256
RET_CHUNK = 128
ROPE_THETA = 10000.0
PEER_TOPK = 16
LN_EPS = 1e-6
NEG_INF = -1e30
LOG2E = math.log2(math.e)
VMEM_LIMIT_BYTES = 56 * 1024 * 1024
VMEM_LIMIT_RESIDENT_BYTES = 60 * 1024 * 1024

_NT = (((1,), (1,)), ((), ()))
_TN = (((0,), (0,)), ((), ()))


def _params(*sem, vmem=VMEM_LIMIT_BYTES):
    return pltpu.CompilerParams(dimension_semantics=sem, vmem_limit_bytes=vmem)


def _row_chunks(n, size=128):
    return [slice(r, min(r + size, n)) for r in range(0, n, size)]


def _pick(n, cap, mult=128):
    best = None
    for t in range(mult, min(n, cap) + 1, mult):
        if n % t == 0:
            best = t
    assert best is not None, (n, cap, mult)
    return best


def _cast_kernel(x_ref, o_ref):
    o_ref[...] = x_ref[...].astype(o_ref.dtype)


def _to_bf16(w, layer):
    _, rows, cols = w.shape
    tr = _pick(rows, max(8, (2 * 1024 * 1024) // cols), 8)
    return pl.pallas_call(
        _cast_kernel,
        grid=(rows // tr,),
        in_specs=[pl.BlockSpec((None, tr, cols), lambda i: (layer, i, 0))],
        out_specs=pl.BlockSpec((tr, cols), lambda i: (i, 0)),
        out_shape=jax.ShapeDtypeStruct((rows, cols), BF16),
        compiler_params=_params("parallel"),
    )(w)


PEER_NI, PEER_NJ = 8, 64


class _SideCast:
    def __init__(self, tables, n_steps, step_of):
        self.n = len(tables)
        self.tables = [w for w, _ in tables]
        self.in_specs, self.out_specs, self.out_shapes = [], [], []
        for w, layer in tables:
            _, rows, d = w.shape
            assert rows % n_steps == 0 and (rows // n_steps) % 16 == 0, (w.shape, n_steps)
            r = rows // n_steps
            self.in_specs.append(pl.BlockSpec((None, r, d), lambda *ids, layer=layer: (layer, step_of(*ids), 0)))
            self.out_specs.append(pl.BlockSpec((r, d), lambda *ids: (step_of(*ids), 0)))
            self.out_shapes.append(jax.ShapeDtypeStruct((rows, d), BF16))

    @staticmethod
    def run(in_refs, out_refs):
        for src, dst in zip(in_refs, out_refs):
            for rows in _row_chunks(src.shape[0], 64):
                dst[rows, :] = src[rows, :].astype(dst.dtype)


def _ada_kernel(c_ref, w_ref, b_ref, o_ref):
    c = c_ref[...]
    s = c * (1.0 / (1.0 + jnp.exp(-c)))
    o_ref[0] = jnp.dot(s.astype(BF16), w_ref[0].astype(BF16), preferred_element_type=F32) + b_ref[0]


def _ada_modulation(c, c_ctx, ada_w, ada_b):
    depth, d, n = ada_w.shape
    cc = jnp.zeros((8, d), F32).at[0].set(c[0]).at[1].set(c_ctx)
    tn = _pick(n, 512)
    out = pl.pallas_call(
        _ada_kernel,
        grid=(depth, n // tn),
        in_specs=[pl.BlockSpec((8, d), lambda l, j: (0, 0)),
                  pl.BlockSpec((1, d, tn), lambda l, j: (l, 0, j)),
                  pl.BlockSpec((1, 1, tn), lambda l, j: (l, 0, j))],
        out_specs=pl.BlockSpec((1, 8, tn), lambda l, j: (l, 0, j)),
        out_shape=jax.ShapeDtypeStruct((depth, 8, n), F32),
        compiler_params=_params("parallel", "parallel"),
    )(cc, ada_w, ada_b.reshape(depth, 1, n))
    return out


def _proj_kernel(x_ref, sh_ref, sc_ref, w_ref, o_ref, *rest, emit_h):
    h_ref = rest[-1]

    @pl.when(pl.program_id(1) == 0)
    def _():
        for rows in _row_chunks(x_ref.shape[0]):
            h = (x_ref[rows, :] * (1.0 + sc_ref[...]) + sh_ref[...]).astype(BF16)
            h_ref[rows, :] = h
            if emit_h:
                rest[0][rows, :] = h

    o_ref[...] = jnp.dot(h_ref[...], w_ref[...], preferred_element_type=F32).astype(o_ref.dtype)


def _proj(x, shift, scale, w, out_dtype, tm, emit_h=False, resident_w=False):
    m, d = x.shape
    n = w.shape[1]
    tm = min(tm, m)
    tn = n if resident_w else _pick(n, 1024)
    w_spec = (pl.BlockSpec((d, tn), lambda i, j: (0, j), pipeline_mode=pl.Buffered(1)) if resident_w
              else pl.BlockSpec((d, tn), lambda i, j: (0, j)))
    out_shape = [jax.ShapeDtypeStruct((m, n), out_dtype)]
    out_specs = [pl.BlockSpec((tm, tn), lambda i, j: (i, j))]
    if emit_h:
        out_shape.append(jax.ShapeDtypeStruct((m, d), BF16))
        out_specs.append(pl.BlockSpec((tm, d), lambda i, j: (i, 0)))
    res = pl.pallas_call(
        functools.partial(_proj_kernel, emit_h=emit_h),
        grid=(m // tm, n // tn),
        in_specs=[pl.BlockSpec((tm, d), lambda i, j: (i, 0)),
                  pl.BlockSpec((1, d), lambda i, j: (0, 0)),
                  pl.BlockSpec((1, d), lambda i, j: (0, 0)),
                  w_spec],
        out_specs=out_specs,
        out_shape=out_shape,
        scratch_shapes=[pltpu.VMEM((tm, d), BF16)],
        compiler_params=_params("parallel", "arbitrary"),
    )(x, shift, scale, w)
    return res if emit_h else res[0]


def _prep_kernel(x_ref, g_ref, cos_ref, sin_ref, o_ref, *, heads, norm, rope, scale):
    for hh in range(heads):
        cols = slice(hh * HEAD_DIM, (hh + 1) * HEAD_DIM)
        x = x_ref[:, cols].astype(F32)
        if norm:
            x = x * lax.rsqrt(jnp.mean(x * x, -1, keepdims=True) + LN_EPS) * g_ref[...]
        if rope:
            lane = lax.broadcasted_iota(jnp.int32, x.shape, 1)
            partner = jnp.where(lane % 2 == 0, pltpu.roll(x, HEAD_DIM - 1, 1), pltpu.roll(x, 1, 1))
            x = x * cos_ref[...] + partner * sin_ref[...]
        if scale != 1.0:
            x = x * scale
        o_ref[:, cols] = x.astype(o_ref.dtype)


def _prep(p, col0, n_heads, gain, cos_i, sin_s, *, norm=False, rope=False, scale=1.0):
    t = p.shape[0]
    hb = 4 if n_heads % 4 == 0 else 1
    bw = hb * HEAD_DIM
    assert col0 % bw == 0
    tt = min(t, 1024)
    c0 = col0 // bw
    return pl.pallas_call(
        functools.partial(_prep_kernel, heads=hb, norm=norm, rope=rope, scale=scale),
        grid=(t // tt, n_heads // hb),
        in_specs=[pl.BlockSpec((tt, bw), lambda i, j: (i, c0 + j)),
                  pl.BlockSpec((1, HEAD_DIM), lambda i, j: (0, 0)),
                  pl.BlockSpec((tt, HEAD_DIM), lambda i, j: (i, 0)),
                  pl.BlockSpec((tt, HEAD_DIM), lambda i, j: (i, 0))],
        out_specs=pl.BlockSpec((tt, bw), lambda i, j: (i, j)),
        out_shape=jax.ShapeDtypeStruct((t, n_heads * HEAD_DIM), BF16),
        compiler_params=_params("parallel", "parallel"),
    )(p, gain.reshape(1, HEAD_DIM).astype(F32), cos_i[:t], sin_s[:t])


def _rope_tables(n_tokens):
    t = jnp.arange(n_tokens, dtype=jnp.int32)
    row = (t // GRID_W).astype(F32)
    col = (t % GRID_W).astype(F32)
    n_freq = HEAD_DIM // 4
    inv = ROPE_THETA ** (-jnp.arange(n_freq, dtype=F32) / n_freq)
    ang = jnp.concatenate([row[:, None] * inv, col[:, None] * inv], -1)
    cos, sin = jnp.cos(ang), jnp.sin(ang)
    cos_i = jnp.repeat(cos, 2, axis=-1)
    sin_s = jnp.stack([-sin, sin], -1).reshape(n_tokens, HEAD_DIM)
    return cos_i, sin_s


KV_BLOCKS_PER_ITER = 2


def _flash_kernel(q_ref, k_ref, v_ref, *rest, group, tk, scale, n_side):
    side_in, o_ref, side_out = rest[:n_side], rest[n_side], rest[n_side + 1:2 * n_side + 1]
    m_ref, l_ref, acc_ref = rest[2 * n_side + 1:]
    _SideCast.run(side_in, side_out)
    n_blocks = k_ref.shape[0] // tk
    m_ref[...] = jnp.full_like(m_ref, NEG_INF)
    l_ref[...] = jnp.zeros_like(l_ref)
    acc_ref[...] = jnp.zeros_like(acc_ref)

    def block(start):
        k = k_ref[pl.ds(start, tk), :]
        v = v_ref[pl.ds(start, tk), :]
        for g in range(group):
            q = q_ref[:, g * HEAD_DIM:(g + 1) * HEAD_DIM]
            if scale != 1.0:
                q = (q.astype(F32) * scale).astype(BF16)
            s = lax.dot_general(q, k, _NT, preferred_element_type=F32)
            m_prev = m_ref[g]
            m_new = jnp.maximum(m_prev, jnp.max(s, -1, keepdims=True))
            alpha = jnp.exp2(m_prev - m_new)
            p = jnp.exp2(s - jnp.concatenate([m_new] * (tk // HEAD_DIM), 1))
            l_ref[g] = alpha * l_ref[g] + jnp.sum(p, -1, keepdims=True)
            acc_ref[g] = alpha * acc_ref[g] + jnp.dot(p.astype(BF16), v, preferred_element_type=F32)
            m_ref[g] = m_new

    def group_of_blocks(i, carry):
        for b in range(KV_BLOCKS_PER_ITER):
            block(pl.multiple_of((i * KV_BLOCKS_PER_ITER + b) * tk, tk))
        return carry

    n_iter = n_blocks // KV_BLOCKS_PER_ITER
    lax.fori_loop(0, n_iter, group_of_blocks, 0)
    for b in range(n_iter * KV_BLOCKS_PER_ITER, n_blocks):
        block(b * tk)
    for g in range(group):
        o_ref[:, g * HEAD_DIM:(g + 1) * HEAD_DIM] = (acc_ref[g] / l_ref[g]).astype(o_ref.dtype)


def _flash(q, q_col0, k, k_col0, v, v_col0, *, kv_heads, group, tq, tk, scale=1.0, cast_tables=()):
    n_q, n_k = q.shape[0], k.shape[0]
    tq, tk = min(tq, n_q), min(tk, n_k)
    assert n_q % tq == 0 and n_k % tk == 0 and tk % HEAD_DIM == 0
    qw = group * HEAD_DIM
    assert q_col0 % qw == 0 and k_col0 % HEAD_DIM == 0 and v_col0 % HEAD_DIM == 0
    qc, kc, vc = q_col0 // qw, k_col0 // HEAD_DIM, v_col0 // HEAD_DIM
    n_qt = n_q // tq
    side = _SideCast(cast_tables, kv_heads * n_qt, lambda h, i: h * n_qt + i)
    res = pl.pallas_call(
        functools.partial(_flash_kernel, group=group, tk=tk, scale=scale, n_side=side.n),
        grid=(kv_heads, n_qt),
        in_specs=[pl.BlockSpec((tq, qw), lambda h, i: (i, qc + h)),
                  pl.BlockSpec((n_k, HEAD_DIM), lambda h, i: (0, kc + h)),
                  pl.BlockSpec((n_k, HEAD_DIM), lambda h, i: (0, vc + h))] + side.in_specs,
        out_specs=[pl.BlockSpec((tq, qw), lambda h, i: (i, h))] + side.out_specs,
        out_shape=[jax.ShapeDtypeStruct((n_q, kv_heads * qw), BF16)] + side.out_shapes,
        scratch_shapes=[pltpu.VMEM((group, tq, HEAD_DIM), F32),
                        pltpu.VMEM((group, tq, HEAD_DIM), F32),
                        pltpu.VMEM((group, tq, HEAD_DIM), F32)],
        compiler_params=_params("parallel", "parallel"),
    )(q, k, v, *side.tables)
    return res if side.n else res[0]


def _na_bias_tables(rpb, rows):
    n_heads = rpb.shape[0]
    rg, w, kr, kc = NA_ROWS_PER_STEP, GRID_W, NA_WIN_R, NA_WIN_C
    n_steps = rows // rg
    assert rows % rg == 0 and n_steps >= 3 and kr == 2 * rg and rows >= kr
    c = np.arange(w)
    dc = np.clip(c[None, :] - c[:, None] + (kc - 1), 0, 2 * kc - 2)
    oh_c = (np.arange(2 * kc - 1)[:, None, None] == dc[None]).astype(np.float32)
    rl, krl = np.arange(rg), np.arange(3 * rg)
    da = krl[None, :] - rl[:, None] - rg + (kr - 1)
    assert da.min() >= 0 and da.max() <= 2 * kr - 2
    oh_a = (np.arange(2 * kr - 1)[:, None, None] == da[None]).astype(np.float32)
    t1 = jnp.einsum('hab,bck->hack', rpb.astype(F32), oh_c, precision=lax.Precision.HIGHEST)
    bias = jnp.einsum('hack,arl->hrclk', t1, oh_a, precision=lax.Precision.HIGHEST)
    bias = bias.reshape(n_heads, rg * w, 3 * rg * w)
    c0 = np.clip(c - kc // 2, 0, w - kc)
    col_ok = (c[None, :] >= c0[:, None]) & (c[None, :] < c0[:, None] + kc)
    masks = []
    for g in (0, 1, n_steps - 1):
        r = rg * g + rl
        r0 = np.clip(r - kr // 2, 0, rows - kr)
        key_row = rg * (g - 1) + krl
        row_ok = (key_row[None, :] >= r0[:, None]) & (key_row[None, :] < r0[:, None] + kr)
        ok = row_ok[:, None, :, None] & col_ok[None, :, None, :]
        masks.append(ok.reshape(rg * w, 3 * rg * w))
    masks = np.stack(masks)
    return jnp.where(masks[:, None], bias[None] * LOG2E, NEG_INF)


def _na_kernel(q_ref, kp_ref, kc_ref, kn_ref, vp_ref, vc_ref, vn_ref, kx_ref, vx_ref, b_ref, *rest,
               heads, scale, n_side):
    side_in, o_ref, side_out = rest[:n_side], rest[n_side], rest[n_side + 1:]
    _SideCast.run(side_in, side_out)
    tq = q_ref.shape[0]
    for hh in range(heads):
        cols = slice(hh * HEAD_DIM, (hh + 1) * HEAD_DIM)
        q = (q_ref[:, cols].astype(F32) * scale).astype(BF16)
        ss = []
        for idx, k_ref in enumerate((kp_ref, kc_ref, kn_ref)):
            s = lax.dot_general(q, k_ref[:, cols], _NT, preferred_element_type=F32)
            ss.append(s + b_ref[0, hh, :, idx * tq:(idx + 1) * tq])
        ss.append(lax.dot_general(q, kx_ref[:, cols], _NT, preferred_element_type=F32))
        m = functools.reduce(jnp.maximum, [jnp.max(s, -1, keepdims=True) for s in ss])
        ps = [jnp.exp2(s - m) for s in ss]
        l = functools.reduce(jnp.add, [jnp.sum(p, -1, keepdims=True) for p in ps])
        acc = None
        for p, v_ref in zip(ps, (vp_ref, vc_ref, vn_ref, vx_ref)):
            pv = jnp.dot(p.astype(BF16), v_ref[:, cols], preferred_element_type=F32)
            acc = pv if acc is None else acc + pv
        o_ref[:, cols] = (acc / l).astype(o_ref.dtype)


def _neighbourhood_attention(p, p_ctx, bias, n_heads, q_col0, k_col0, v_col0, cast_tables=()):
    s_len = p.shape[0]
    tq = NA_ROWS_PER_STEP * GRID_W
    n_steps = s_len // tq
    hb = 4 if n_heads % 4 == 0 else 1
    bw = hb * HEAD_DIM
    assert p_ctx.shape[0] == tq
    qc, kc, vc = q_col0 // bw, k_col0 // bw, v_col0 // bw
    prev = lambda g: jnp.maximum(g - 1, 0)
    nxt = lambda g: jnp.minimum(g + 1, n_steps - 1)
    variant = lambda g: jnp.where(g == 0, 0, jnp.where(g == n_steps - 1, 2, 1))
    lat = lambda col, row: pl.BlockSpec((tq, bw), lambda h, g: (row(g), col + h))
    ident = lambda g: g
    side = _SideCast(cast_tables, (n_heads // hb) * n_steps, lambda h, g: h * n_steps + g)
    res = pl.pallas_call(
        functools.partial(_na_kernel, heads=hb, scale=HEAD_DIM ** -0.5 * LOG2E, n_side=side.n),
        grid=(n_heads // hb, n_steps),
        in_specs=[lat(qc, ident),
                  lat(kc, prev), lat(kc, ident), lat(kc, nxt),
                  lat(vc, prev), lat(vc, ident), lat(vc, nxt),
                  pl.BlockSpec((tq, bw), lambda h, g: (0, kc + h)),
                  pl.BlockSpec((tq, bw), lambda h, g: (0, vc + h)),
                  pl.BlockSpec((1, hb, tq, 3 * tq), lambda h, g: (variant(g), h, 0, 0))] + side.in_specs,
        out_specs=[pl.BlockSpec((tq, bw), lambda h, g: (g, h))] + side.out_specs,
        out_shape=[jax.ShapeDtypeStruct((s_len, n_heads * HEAD_DIM), BF16)] + side.out_shapes,
        compiler_params=_params("parallel", "arbitrary"),
    )(p, p, p, p, p, p, p, p_ctx, p_ctx, bias, *side.tables)
    return res if side.n else res[0]


def _swa_kernel(sink_ref, q_ref, kp_ref, kc_ref, kn_ref, vp_ref, vc_ref, vn_ref, kx_ref, vx_ref, mask_ref, o_ref,
                *, group):
    h = pl.program_id(0)
    tq = q_ref.shape[0]
    q = jnp.concatenate([q_ref[:, g * HEAD_DIM:(g + 1) * HEAD_DIM] for g in range(group)], 0)
    ss = []
    for idx, k_ref in enumerate((kp_ref, kc_ref, kn_ref)):
        s = lax.dot_general(q, k_ref[...], _NT, preferred_element_type=F32)
        ss.append(s + mask_ref[0, idx])
    ss.append(lax.dot_general(q, kx_ref[...], _NT, preferred_element_type=F32))
    row_g = lax.broadcasted_iota(jnp.int32, (group * tq, 1), 0) // tq
    sink = jnp.zeros((group * tq, 1), F32)
    for g in range(group):
        sink = jnp.where(row_g == g, sink_ref[h * group + g] * LOG2E, sink)
    m = functools.reduce(jnp.maximum, [jnp.max(s, -1, keepdims=True) for s in ss] + [sink])
    ps = [jnp.exp2(s - m) for s in ss]
    l = functools.reduce(jnp.add, [jnp.sum(p, -1, keepdims=True) for p in ps]) + jnp.exp2(sink - m)
    acc = None
    for p, v_ref in zip(ps, (vp_ref, vc_ref, vn_ref, vx_ref)):
        pv = jnp.dot(p.astype(BF16), v_ref[...], preferred_element_type=F32)
        acc = pv if acc is None else acc + pv
    out = acc / l
    for g in range(group):
        o_ref[:, g * HEAD_DIM:(g + 1) * HEAD_DIM] = out[g * tq:(g + 1) * tq].astype(o_ref.dtype)


def _sliding_window_attention(q, k, v, v_col0, kx, kx_col0, vx, vx_col0, sink, *, kv_heads, group):
    s_len = q.shape[0]
    tq = SW_BLOCK
    assert s_len % tq == 0 and tq >= SW_WINDOW and kx.shape[0] == vx.shape[0]
    n_blk = s_len // tq
    n_ctx = kx.shape[0]
    qw = group * HEAD_DIM
    vc, kxc, vxc = v_col0 // HEAD_DIM, kx_col0 // HEAD_DIM, vx_col0 // HEAD_DIM
    prev = lambda n: jnp.maximum(n - 1, 0)
    nxt = lambda n: jnp.minimum(n + 1, n_blk - 1)
    ident = lambda n: n
    blk = lambda col, row: pl.BlockSpec((tq, HEAD_DIM), lambda h, n: (row(n), col + h))
    assert n_blk >= 3
    ql = np.arange(group * tq)[:, None] % tq
    kl = np.arange(tq)[None, :]
    diff = kl - ql
    ok = np.stack([diff - tq >= -SW_WINDOW, np.abs(diff) <= SW_WINDOW, diff + tq <= SW_WINDOW])
    none = np.zeros_like(ok[0])
    variants = np.stack([np.stack([none, ok[1], ok[2]]), ok, np.stack([ok[0], ok[1], none])])
    mask = jnp.where(variants, 0.0, NEG_INF).astype(F32)
    variant = lambda n: jnp.where(n == 0, 0, jnp.where(n == n_blk - 1, 2, 1))
    return pl.pallas_call(
        functools.partial(_swa_kernel, group=group),
        grid=(kv_heads, n_blk),
        in_specs=[pl.BlockSpec(memory_space=pltpu.SMEM),
                  pl.BlockSpec((tq, qw), lambda h, n: (n, h)),
                  blk(0, prev), blk(0, ident), blk(0, nxt),
                  blk(vc, prev), blk(vc, ident), blk(vc, nxt),
                  pl.BlockSpec((n_ctx, HEAD_DIM), lambda h, n: (0, kxc + h)),
                  pl.BlockSpec((n_ctx, HEAD_DIM), lambda h, n: (0, vxc + h)),
                  pl.BlockSpec((1, 3, group * tq, tq), lambda h, n: (variant(n), 0, 0, 0))],
        out_specs=pl.BlockSpec((tq, qw), lambda h, n: (n, h)),
        out_shape=jax.ShapeDtypeStruct((s_len, kv_heads * qw), BF16),
        compiler_params=_params("parallel", "arbitrary"),
    )(sink.astype(F32), q, k, k, k, v, v, v, kx, vx, mask)


def _ret_kernel(cd_ref, qf_ref, kf_ref, vf_ref, qb_ref, kb_ref, vb_ref, kx_ref, vx_ref,
                intra_ref, qd_ref, kd_ref, wx_ref, of_ref, ob_ref, st_ref, *, heads):
    hb, n = pl.program_id(0), pl.program_id(1)

    @pl.when(n == 0)
    def _():
        for hh in range(heads):
            cols = slice(hh * HEAD_DIM, (hh + 1) * HEAD_DIM)
            for d in range(2):
                kw = (kx_ref[:, cols].astype(F32) * wx_ref[d, hh]).astype(BF16)
                st_ref[d, hh] = lax.dot_general(kw, vx_ref[:, cols], _TN, preferred_element_type=F32)

    for hh in range(heads):
        cols = slice(hh * HEAD_DIM, (hh + 1) * HEAD_DIM)
        head = hb * heads + hh
        for d, (q_ref, k_ref, v_ref, o_ref) in enumerate(((qf_ref, kf_ref, vf_ref, of_ref),
                                                          (qb_ref, kb_ref, vb_ref, ob_ref))):
            q, k, v = q_ref[:, cols], k_ref[:, cols], v_ref[:, cols]
            state = st_ref[d, hh]
            inner = lax.dot_general(q, k, _NT, preferred_element_type=F32) * intra_ref[d, hh]
            qs = (q.astype(F32) * qd_ref[d, hh]).astype(BF16)
            o_ref[:, cols] = (jnp.dot(inner.astype(BF16), v, preferred_element_type=F32)
                              + jnp.dot(qs, state.astype(BF16), preferred_element_type=F32))
            ks = (k.astype(F32) * kd_ref[d, hh]).astype(BF16)
            st_ref[d, hh] = state * cd_ref[d, head] + lax.dot_general(ks, v, _TN, preferred_element_type=F32)


def _retention(rq, rk, p, v_col0, rkx, p_ctx, log_gamma, n_heads):
    s_len = rq.shape[0]
    c = RET_CHUNK
    n_chunks = s_len // c
    n_ctx = rkx.shape[0]
    hb = 8 if n_heads % 8 == 0 else (4 if n_heads % 4 == 0 else 1)
    bw = hb * HEAD_DIM
    assert v_col0 % bw == 0
    vc = v_col0 // bw
    lg = log_gamma.astype(F32)
    pos = jnp.arange(c, dtype=F32)
    diff = pos[:, None] - pos[None, :]
    intra_f = jnp.where(diff >= 0, jnp.exp(lg[0][:, None, None] * jnp.maximum(diff, 0.0)), 0.0)
    intra_b = jnp.where(diff <= 0, jnp.exp(lg[1][:, None, None] * jnp.maximum(-diff, 0.0)), 0.0)
    intra = jnp.stack([intra_f, intra_b])
    lanes = lambda t: jnp.broadcast_to(t[..., None], t.shape + (HEAD_DIM,))
    qd = lanes(jnp.stack([jnp.exp(lg[0][:, None] * (pos + 1.0)), jnp.exp(lg[1][:, None] * (c - pos))]))
    kd = lanes(jnp.stack([jnp.exp(lg[0][:, None] * (c - 1.0 - pos)), jnp.exp(lg[1][:, None] * pos)]))
    cd = jnp.exp(lg * c)
    jx = jnp.arange(n_ctx, dtype=F32)
    wx = lanes(jnp.stack([jnp.exp(lg[0][:, None] * (n_ctx - 1.0 - jx)), jnp.exp(lg[1][:, None] * jx)]))
    fwd = lambda col: pl.BlockSpec((c, bw), lambda h, n: (n, col + h))
    bwd = lambda col: pl.BlockSpec((c, bw), lambda h, n: (n_chunks - 1 - n, col + h))
    tab = lambda rows: pl.BlockSpec((2, hb, rows, HEAD_DIM), lambda h, n: (0, h, 0, 0))
    out_sds = jax.ShapeDtypeStruct((s_len, n_heads * HEAD_DIM), F32)
    return pl.pallas_call(
        functools.partial(_ret_kernel, heads=hb),
        grid=(n_heads // hb, n_chunks),
        in_specs=[pl.BlockSpec(memory_space=pltpu.SMEM),
                  fwd(0), fwd(0), fwd(vc), bwd(0), bwd(0), bwd(vc),
                  pl.BlockSpec((n_ctx, bw), lambda h, n: (0, h)),
                  pl.BlockSpec((n_ctx, bw), lambda h, n: (0, vc + h)),
                  tab(c), tab(c), tab(c), tab(n_ctx)],
        out_specs=[fwd(0), bwd(0)],
        out_shape=[out_sds, out_sds],
        scratch_shapes=[pltpu.VMEM((2, hb, HEAD_DIM, HEAD_DIM), F32)],
        compiler_params=_params("parallel", "arbitrary"),
    )(cd, rq, rk, p, rq, rk, p, rkx, p_ctx, intra, qd, kd, wx)


def _ret_out_kernel(of_ref, ob_ref, g_ref, o_ref, *, heads):
    for hh in range(heads):
        cols = slice(hh * HEAD_DIM, (hh + 1) * HEAD_DIM)
        o = of_ref[:, cols] + ob_ref[:, cols]
        mu = jnp.mean(o, -1, keepdims=True)
        var = jnp.mean(jnp.square(o - mu), -1, keepdims=True)
        gate = g_ref[:, cols].astype(F32)
        gate = gate * (1.0 / (1.0 + jnp.exp(-gate)))
        o_ref[:, cols] = (gate * ((o - mu) * lax.rsqrt(var + LN_EPS))).astype(o_ref.dtype)


def _retention_output(o_f, o_b, p, gate_col0, n_heads):
    s_len = o_f.shape[0]
    hb = 4 if n_heads % 4 == 0 else 1
    bw = hb * HEAD_DIM
    gc = gate_col0 // bw
    tt = min(s_len, 512)
    spec = pl.BlockSpec((tt, bw), lambda i, j: (i, j))
    return pl.pallas_call(
        functools.partial(_ret_out_kernel, heads=hb),
        grid=(s_len // tt, n_heads // hb),
        in_specs=[spec, spec, pl.BlockSpec((tt, bw), lambda i, j: (i, gc + j))],
        out_specs=spec,
        out_shape=jax.ShapeDtypeStruct((s_len, n_heads * HEAD_DIM), BF16),
        compiler_params=_params("parallel", "parallel"),
    )(o_f, o_b, p)


def _residual_ln(x, y, gate, g, b, alpha):
    h = alpha * x + gate * y
    mu = jnp.mean(h, -1, keepdims=True)
    var = jnp.mean(jnp.square(h - mu), -1, keepdims=True)
    return (h - mu) * lax.rsqrt(var + LN_EPS) * g + b


def _outproj_kernel(ya_ref, yb_ref, w_ref, x_ref, gate_ref, g_ref, b_ref, o_ref, *, alpha):
    wa = ya_ref.shape[1]
    o_ref[...] = (jnp.dot(ya_ref[...], w_ref[:wa, :], preferred_element_type=F32)
                  + jnp.dot(yb_ref[...], w_ref[wa:, :], preferred_element_type=F32))
    for rows in _row_chunks(x_ref.shape[0], 64):
        o_ref[rows, :] = _residual_ln(x_ref[rows, :], o_ref[rows, :], gate_ref[...], g_ref[...], b_ref[...], alpha)


def _outproj_ln(ya, yb, w, x, gate, g, b, alpha, tm):
    m, wa = ya.shape
    wb = yb.shape[1]
    d = w.shape[1]
    tm = min(tm, m)
    vec = pl.BlockSpec((1, d), lambda i: (0, 0))
    return pl.pallas_call(
        functools.partial(_outproj_kernel, alpha=alpha),
        grid=(m // tm,),
        in_specs=[pl.BlockSpec((tm, wa), lambda i: (i, 0)),
                  pl.BlockSpec((tm, wb), lambda i: (i, 0)),
                  pl.BlockSpec((wa + wb, d), lambda i: (0, 0), pipeline_mode=pl.Buffered(1)),
                  pl.BlockSpec((tm, d), lambda i: (i, 0)),
                  vec, vec, vec],
        out_specs=pl.BlockSpec((tm, d), lambda i: (i, 0)),
        out_shape=jax.ShapeDtypeStruct((m, d), F32),
        compiler_params=_params("parallel", vmem=VMEM_LIMIT_RESIDENT_BYTES),
    )(ya, yb, w, x, gate, g.reshape(1, d), b.reshape(1, d))


def _add_ln_kernel(x_ref, f_ref, gate_ref, g_ref, b_ref, o_ref, *, alpha):
    o_ref[...] = _residual_ln(x_ref[...], f_ref[...].astype(F32), gate_ref[...], g_ref[...], b_ref[...], alpha)


def _add_ln(x, f, gate, g, b, alpha):
    m, d = x.shape
    tt = min(m, 256)
    row = pl.BlockSpec((tt, d), lambda i: (i, 0))
    vec = pl.BlockSpec((1, d), lambda i: (0, 0))
    return pl.pallas_call(
        functools.partial(_add_ln_kernel, alpha=alpha),
        grid=(m // tt,),
        in_specs=[row, row, vec, vec, vec],
        out_specs=row,
        out_shape=jax.ShapeDtypeStruct((m, d), F32),
        compiler_params=_params("parallel"),
    )(x, f, gate, g.reshape(1, d), b.reshape(1, d))


PEER_LIST = 24


def _peer_gate_kernel(q_ref, keys_ref, d1_ref, f1_ref, s2_ref, e2_ref, a_ref, b_ref, cand_ref):
    half = keys_ref.shape[-1]
    k = PEER_TOPK
    s1 = lax.dot_general(keys_ref[0, 0], q_ref[:, :half].astype(BF16), _NT, preferred_element_type=F32)
    s2 = lax.dot_general(keys_ref[0, 1], q_ref[:, half:].astype(BF16), _NT, preferred_element_type=F32)

    def top_rows(s, dst_ref):
        dst_ref[...] = jnp.full_like(dst_ref, -jnp.inf)
        n_slabs = s.shape[0] // 8
        if n_slabs & (n_slabs - 1):
            cur = s
            for r in range(k + 1):
                m = jnp.max(cur, 0, keepdims=True)
                dst_ref[r:r + 1, :] = m
                cur = jnp.where(cur == m, -jnp.inf, cur)
            return
        v = [s[8 * i:8 * i + 8, :] for i in range(n_slabs)]
        size = 2
        while size <= n_slabs:
            stride = size // 2
            while stride >= 1:
                for i in range(n_slabs):
                    j = i ^ stride
                    if j > i:
                        hi, lo = jnp.maximum(v[i], v[j]), jnp.minimum(v[i], v[j])
                        v[i], v[j] = (hi, lo) if (i & size) == 0 else (lo, hi)
                stride //= 2
            size *= 2
        depth = min(n_slabs, k + 1)
        v = v[:depth] + [jnp.full_like(v[0], -jnp.inf)]
        for r in range(k + 1):
            m = jnp.max(v[0], 0, keepdims=True)
            dst_ref[r:r + 1, :] = m
            win = v[0] == m
            for i in range(depth):
                v[i] = jnp.where(win, v[i + 1], v[i])

    top_rows(s1, a_ref)
    top_rows(s2, b_ref)
    half_k = k // 2
    cand_ref[0:PEER_LIST, :] = a_ref[0:1, :] + b_ref[...]
    for r in range(1, half_k):
        cand_ref[PEER_LIST + (r - 1) * half_k:PEER_LIST + r * half_k, :] = a_ref[r:r + 1, :] + b_ref[0:half_k, :]
    cand_ref[PEER_LIST + (half_k - 1) * half_k:, :] = a_ref[half_k:, :] + b_ref[0:1, :]
    best = a_ref[0:1, :] + b_ref[0:1, :]
    cur = cand_ref[...]
    z = jnp.zeros_like(best)
    kth = best
    for r in range(k):
        kth = jnp.max(cur, 0, keepdims=True)
        z = z + jnp.exp(kth - best)
        cur = jnp.where(cur == kth, -jnp.inf, cur)
    tau = 0.5 * (kth + jnp.max(cur, 0, keepdims=True))
    d1_ref[0] = tau - s1
    f1_ref[0] = jnp.exp(s1 - a_ref[0:1, :]) / z
    s2_ref[0] = s2
    e2_ref[0] = jnp.exp(s2 - b_ref[0:1, :])


def _peer_gates(q, sub_keys, tm):
    t = q.shape[0]
    n_heads, _, n_keys, half = sub_keys.shape
    tm = min(tm, t)
    big = jax.ShapeDtypeStruct((n_heads, n_keys, t), F32)
    blk = pl.BlockSpec((1, n_keys, tm), lambda i, h: (h, 0, i))
    return pl.pallas_call(
        _peer_gate_kernel,
        grid=(t // tm, n_heads),
        in_specs=[pl.BlockSpec((tm, 2 * half), lambda i, h: (i, h)),
                  pl.BlockSpec((1, 2, n_keys, half), lambda i, h: (h, 0, 0, 0))],
        out_specs=[blk, blk, blk, blk],
        out_shape=[big, big, big, big],
        scratch_shapes=[pltpu.VMEM((PEER_LIST, tm), F32), pltpu.VMEM((PEER_LIST, tm), F32),
                        pltpu.VMEM((2 * PEER_LIST - PEER_TOPK // 2 + (PEER_TOPK // 2 - 1) * (PEER_TOPK // 2), tm),
                                   F32)],
        compiler_params=_params("parallel", "parallel"),
    )(q, sub_keys)


def _gelu_tanh(x):
    c = math.sqrt(2.0 / math.pi)
    hx = 0.5 * x
    return hx + hx * jnp.tanh(x * (c + (c * 0.044715) * (x * x)))


def _peer_kernel(h_ref, u_ref, v_ref, d1_ref, f1_ref, s2_ref, e2_ref, *rest, n_side):
    side_in, o_ref, side_out = rest[:n_side], rest[n_side], rest[n_side + 1:2 * n_side + 1]
    acc_ref, g_ref = rest[2 * n_side + 1:]
    _SideCast.run(side_in, side_out)
    cj, ci = pl.program_id(1), pl.program_id(2)
    n_heads, ni, tm = d1_ref.shape
    nj = s2_ref.shape[1]
    d = h_ref.shape[1]

    @pl.when((cj == 0) & (ci == 0))
    def _():
        acc_ref[...] = jnp.zeros_like(acc_ref)

    rows = 32
    for ii in range(ni):
        for r0 in range(0, nj, rows):
            for t0 in range(0, tm, 128):
                tok = slice(t0, t0 + 128)
                g = None
                for h in range(n_heads):
                    w = f1_ref[h, ii:ii + 1, tok] * e2_ref[h, r0:r0 + rows, tok]
                    sel = jnp.where(s2_ref[h, r0:r0 + rows, tok] >= d1_ref[h, ii:ii + 1, tok], w, 0.0)
                    g = sel if g is None else g + sel
                g_ref[ii * nj + r0:ii * nj + r0 + rows, tok] = g
    u = u_ref[...].reshape(ni * nj, d)
    act = _gelu_tanh(lax.dot_general(u, h_ref[...], _NT, preferred_element_type=F32))
    wt = (g_ref[...] * act).astype(BF16)
    acc_ref[...] += lax.dot_general(wt, v_ref[...].reshape(ni * nj, d), _TN, preferred_element_type=F32)

    @pl.when((cj == pl.num_programs(1) - 1) & (ci == pl.num_programs(2) - 1))
    def _():
        o_ref[...] = acc_ref[...].astype(o_ref.dtype)


def _peer_dense(hf, u3, v3, d1, f1, s2, e2, tm, cast_tables=()):
    t, d = hf.shape
    n_heads, n_keys = d1.shape[0], d1.shape[1]
    tm = min(tm, t)
    ni, nj = PEER_NI, PEER_NJ
    n_cj, n_ci = n_keys // nj, n_keys // ni
    tab = pl.BlockSpec((ni, nj, d), lambda i, cj, ci: (ci, cj, 0))
    first = pl.BlockSpec((n_heads, ni, tm), lambda i, cj, ci: (0, ci, i))
    second = pl.BlockSpec((n_heads, nj, tm), lambda i, cj, ci: (0, cj, i))
    side = _SideCast(cast_tables, (t // tm) * n_cj * n_ci, lambda i, cj, ci: (i * n_cj + cj) * n_ci + ci)
    res = pl.pallas_call(
        functools.partial(_peer_kernel, n_side=side.n),
        grid=(t // tm, n_cj, n_ci),
        in_specs=[pl.BlockSpec((tm, d), lambda i, cj, ci: (i, 0)), tab, tab, first, first, second, second]
        + side.in_specs,
        out_specs=[pl.BlockSpec((tm, d), lambda i, cj, ci: (i, 0))] + side.out_specs,
        out_shape=[jax.ShapeDtypeStruct((t, d), BF16)] + side.out_shapes,
        scratch_shapes=[pltpu.VMEM((tm, d), F32), pltpu.VMEM((ni * nj, tm), F32)],
        compiler_params=_params("parallel", "arbitrary", "arbitrary"),
    )(hf, u3, v3, d1, f1, s2, e2, *side.tables)
    return res if side.n else res[0]


def _peer_block(x, shift, scale, gate, w_q, sub_keys, u3, v3, g, b, alpha, tm, cast_tables=()):
    q, hf = _proj(x, shift, scale, w_q, F32, tm, emit_h=True, resident_w=True)
    d1, f1, s2, e2 = _peer_gates(q, sub_keys, tm)
    res = _peer_dense(hf, u3, v3, d1, f1, s2, e2, tm, cast_tables)
    f, casts = (res[0], tuple(res[1:])) if cast_tables else (res, ())
    return _add_ln(x, f, gate, g, b, alpha), casts


def kernel(x, c, ctx, c_ctx, ada_w, ada_b, ln_g, ln_b, ev_w_in, ev_w_out, ev_na_rpb, ev_ga_q_gain, ev_ga_k_gain,
           od_w_in, od_w_out, od_ret_decay_exp, od_sw_sink, peer_w_q, peer_sub_keys, peer_u, peer_v):
    assert x.shape[0] == 1 and ada_w.shape[0] == 2
    depth = ada_w.shape[0]
    _, s_len, d = x.shape
    hd = HEAD_DIM
    mix_heads = d // (2 * hd)
    kv_heads = mix_heads // 4
    group = mix_heads // kv_heads
    alpha = (2 * depth) ** 0.25
    att_scale = hd ** -0.5
    sm_scale = att_scale * LOG2E
    n_keys = peer_sub_keys.shape[3]
    tm = 512
    tm_out = 256

    xs, cs = x[0], ctx[0]
    mods = _ada_modulation(c, c_ctx, ada_w, ada_b)
    cos_i, sin_s = _rope_tables(s_len)
    ones = jnp.ones((hd,), F32)

    def mod6(layer, which):
        return [m.reshape(1, d) for m in jnp.split(mods[layer, which], 6)]

    def peer_args(layer, wq_bf16, u_bf16, v_bf16):
        return (wq_bf16, peer_sub_keys[layer].astype(BF16),
                u_bf16.reshape(n_keys, n_keys, d), v_bf16.reshape(n_keys, n_keys, d),
                ln_g[layer, 1], ln_b[layer, 1], alpha)

    later_weights = ((ev_w_out, 0), (od_w_in, 0), (od_w_out, 0), (peer_w_q, 0), (peer_w_q, 1))

    sh_a, sc_a, g_a, sh_f, sc_f, g_f = mod6(0, 0)
    csh_a, csc_a, cg_a, csh_f, csc_f, cg_f = mod6(0, 1)
    w_in = _to_bf16(ev_w_in, 0)
    p = _proj(xs, sh_a, sc_a, w_in, BF16, tm)
    pc = _proj(cs, csh_a, csc_a, w_in, BF16, tm)
    mh = mix_heads * hd
    kvw = kv_heads * hd
    na_q, na_k, na_v, ga_q, ga_k, ga_v = 0, mh, 2 * mh, 3 * mh, 4 * mh, 4 * mh + kvw
    qg = _prep(p, ga_q, mix_heads, ev_ga_q_gain[0], cos_i, sin_s, norm=True, rope=True, scale=sm_scale)
    kg = _prep(p, ga_k, kv_heads, ev_ga_k_gain[0], cos_i, sin_s, norm=True, rope=True)
    qgc = _prep(pc, ga_q, mix_heads, ev_ga_q_gain[0], cos_i, sin_s, norm=True, scale=sm_scale)
    kgc = _prep(pc, ga_k, kv_heads, ev_ga_k_gain[0], cos_i, sin_s, norm=True)
    k_all = jnp.concatenate([kg, kgc], 0)
    v_all = jnp.concatenate([p[:, ga_v:ga_v + kvw], pc[:, ga_v:ga_v + kvw]], 0)
    tk = _pick(k_all.shape[0], 1024, 256)
    y_ga, u0, v0 = _flash(qg, 0, k_all, 0, v_all, 0, kv_heads=kv_heads, group=group, tq=512, tk=tk,
                          cast_tables=((peer_u, 0), (peer_v, 0)))
    yc_ga = _flash(qgc, 0, kgc, 0, pc, ga_v, kv_heads=kv_heads, group=group, tq=256, tk=256)
    bias = _na_bias_tables(ev_na_rpb[0], s_len // GRID_W)
    y_na, w_out, w_in1, w_out1, wq0, wq1 = _neighbourhood_attention(p, pc, bias, mix_heads, na_q, na_k, na_v,
                                                                    cast_tables=later_weights)
    yc_na = _flash(pc, na_q, pc, na_k, pc, na_v, kv_heads=mix_heads, group=1, tq=256, tk=256, scale=sm_scale)
    xs = _outproj_ln(y_na, y_ga, w_out, xs, g_a, ln_g[0, 0], ln_b[0, 0], alpha, tm_out)
    cs = _outproj_ln(yc_na, yc_ga, w_out, cs, cg_a, ln_g[0, 0], ln_b[0, 0], alpha, tm_out)
    pa = peer_args(0, wq0, u0, v0)
    xs, (u1, v1) = _peer_block(xs, sh_f, sc_f, g_f, *pa, tm, cast_tables=((peer_u, 1), (peer_v, 1)))
    cs, _ = _peer_block(cs, csh_f, csc_f, cg_f, *pa, tm)

    sh_a, sc_a, g_a, sh_f, sc_f, g_f = mod6(1, 0)
    csh_a, csc_a = mod6(1, 1)[:2]
    p = _proj(xs, sh_a, sc_a, w_in1, BF16, tm)
    pc = _proj(cs, csh_a, csc_a, w_in1, BF16, tm)
    r_q, r_k, r_v, r_g, s_q, s_k, s_v = 0, mh, 2 * mh, 3 * mh, 4 * mh, 5 * mh, 5 * mh + kvw
    rq = _prep(p, r_q, mix_heads, ones, cos_i, sin_s, rope=True)
    rk = _prep(p, r_k, mix_heads, ones, cos_i, sin_s, rope=True, scale=att_scale)
    rkc = _prep(pc, r_k, mix_heads, ones, cos_i, sin_s, scale=att_scale)
    sq = _prep(p, s_q, mix_heads, ones, cos_i, sin_s, rope=True, scale=sm_scale)
    sk = _prep(p, s_k, kv_heads, ones, cos_i, sin_s, rope=True)
    log_gamma = jnp.log1p(-jnp.exp2(-od_ret_decay_exp[0].astype(F32)))
    o_f, o_b = _retention(rq, rk, p, r_v, rkc, pc, log_gamma, mix_heads)
    y_ret = _retention_output(o_f, o_b, p, r_g, mix_heads)
    y_sw = _sliding_window_attention(sq, sk, p, s_v, pc, s_k, pc, s_v, od_sw_sink[0],
                                     kv_heads=kv_heads, group=group)
    xs = _outproj_ln(y_ret, y_sw, w_out1, xs, g_a, ln_g[1, 0], ln_b[1, 0], alpha, tm_out)
    xs, _ = _peer_block(xs, sh_f, sc_f, g_f, *peer_args(1, wq1, u1, v1), tm)
    return xs[None]
```

```python
import functools
import math

import numpy as np
import jax
import jax.numpy as jnp
from jax import lax
from jax.experimental import pallas as pl
from jax.experimental.pallas import tpu as pltpu

F32 = jnp.float32
BF16 = jnp.bfloat16

GRID_W = 64
HEAD_DIM = 128
NA_WIN_R = 8
NA_WIN_C = 16
NA_ROWS_PER_STEP = 4
SW_WINDOW = 128
SW_BLOCK = 256
RET_CHUNK = 128
ROPE_THETA = 10000.0
PEER_TOPK = 16
LN_EPS = 1e-6
NEG_INF = -1e30
LOG2E = math.log2(math.e)
VMEM_LIMIT_BYTES = 56 * 1024 * 1024
VMEM_LIMIT_RESIDENT_BYTES = 60 * 1024 * 1024

_NT = (((1,), (1,)), ((), ()))
_TN = (((0,), (0,)), ((), ()))


def _params(*sem, vmem=VMEM_LIMIT_BYTES):
    return pltpu.CompilerParams(dimension_semantics=sem, vmem_limit_bytes=vmem)


def _row_chunks(n, size=128):
    return [slice(r, min(r + size, n)) for r in range(0, n, size)]


def _pick(n, cap, mult=128):
    best = None
    for t in range(mult, min(n, cap) + 1, mult):
        if n % t == 0:
            best = t
    assert best is not None, (n, cap, mult)
    return best


def _cast_kernel(x_ref, o_ref):
    o_ref[...] = x_ref[...].astype(o_ref.dtype)


def _to_bf16(w, layer):
    _, rows, cols = w.shape
    tr = _pick(rows, max(8, (2 * 1024 * 1024) // cols), 8)
    return pl.pallas_call(
        _cast_kernel,
        grid=(rows // tr,),
        in_specs=[pl.BlockSpec((None, tr, cols), lambda i: (layer, i, 0))],
        out_specs=pl.BlockSpec((tr, cols), lambda i: (i, 0)),
        out_shape=jax.ShapeDtypeStruct((rows, cols), BF16),
        compiler_params=_params("parallel"),
    )(w)


PEER_NI, PEER_NJ = 8, 64


class _SideCast:
    def __init__(self, tables, n_steps, step_of):
        self.n = len(tables)
        self.tables = [w for w, _ in tables]
        self.in_specs, self.out_specs, self.out_shapes = [], [], []
        for w, layer in tables:
            _, rows, d = w.shape
            every = 1
            while rows * every % n_steps or (rows * every // n_steps) % 16:
                every *= 2
                assert every <= n_steps, (w.shape, n_steps)
            r = rows * every // n_steps
            self.in_specs.append(pl.BlockSpec(
                (None, r, d), lambda *ids, layer=layer, every=every: (layer, step_of(*ids) // every, 0)))
            self.out_specs.append(pl.BlockSpec((r, d), lambda *ids, every=every: (step_of(*ids) // every, 0)))
            self.out_shapes.append(jax.ShapeDtypeStruct((rows, d), BF16))

    @staticmethod
    def run(in_refs, out_refs):
        for src, dst in zip(in_refs, out_refs):
            for rows in _row_chunks(src.shape[0], 64):
                dst[rows, :] = src[rows, :].astype(dst.dtype)


def _ada_kernel(c_ref, w_ref, b_ref, o_ref):
    c = c_ref[...]
    s = c * (1.0 / (1.0 + jnp.exp(-c)))
    o_ref[0] = jnp.dot(s.astype(BF16), w_ref[0].astype(BF16), preferred_element_type=F32) + b_ref[0]


def _ada_modulation(c, c_ctx, ada_w, ada_b):
    depth, d, n = ada_w.shape
    cc = jnp.zeros((8, d), F32).at[0].set(c[0]).at[1].set(c_ctx)
    tn = _pick(n, 512)
    out = pl.pallas_call(
        _ada_kernel,
        grid=(depth, n // tn),
        in_specs=[pl.BlockSpec((8, d), lambda l, j: (0, 0)),
                  pl.BlockSpec((1, d, tn), lambda l, j: (l, 0, j)),
                  pl.BlockSpec((1, 1, tn), lambda l, j: (l, 0, j))],
        out_specs=pl.BlockSpec((1, 8, tn), lambda l, j: (l, 0, j)),
        out_shape=jax.ShapeDtypeStruct((depth, 8, n), F32),
        compiler_params=_params("parallel", "parallel"),
    )(cc, ada_w, ada_b.reshape(depth, 1, n))
    return out


def _proj_kernel(x_ref, sh_ref, sc_ref, w_ref, o_ref, *rest, emit_h):
    h_ref = rest[-1]

    @pl.when(pl.program_id(1) == 0)
    def _():
        for rows in _row_chunks(x_ref.shape[0]):
            h = (x_ref[rows, :] * (1.0 + sc_ref[...]) + sh_ref[...]).astype(BF16)
            h_ref[rows, :] = h
            if emit_h:
                rest[0][rows, :] = h

    o_ref[...] = jnp.dot(h_ref[...], w_ref[...], preferred_element_type=F32).astype(o_ref.dtype)


def _proj(x, shift, scale, w, out_dtype, tm, emit_h=False, resident_w=False):
    m, d = x.shape
    n = w.shape[1]
    tm = min(tm, m)
    tn = n if resident_w else _pick(n, 1024)
    w_spec = (pl.BlockSpec((d, tn), lambda i, j: (0, j), pipeline_mode=pl.Buffered(1)) if resident_w
              else pl.BlockSpec((d, tn), lambda i, j: (0, j)))
    out_shape = [jax.ShapeDtypeStruct((m, n), out_dtype)]
    out_specs = [pl.BlockSpec((tm, tn), lambda i, j: (i, j))]
    if emit_h:
        out_shape.append(jax.ShapeDtypeStruct((m, d), BF16))
        out_specs.append(pl.BlockSpec((tm, d), lambda i, j: (i, 0)))
    res = pl.pallas_call(
        functools.partial(_proj_kernel, emit_h=emit_h),
        grid=(m // tm, n // tn),
        in_specs=[pl.BlockSpec((tm, d), lambda i, j: (i, 0)),
                  pl.BlockSpec((1, d), lambda i, j: (0, 0)),
                  pl.BlockSpec((1, d), lambda i, j: (0, 0)),
                  w_spec],
        out_specs=out_specs,
        out_shape=out_shape,
        scratch_shapes=[pltpu.VMEM((tm, d), BF16)],
        compiler_params=_params("parallel", "arbitrary"),
    )(x, shift, scale, w)
    return res if emit_h else res[0]


def _prep_kernel(x_ref, g_ref, cos_ref, sin_ref, o_ref, *, heads, norm, rope, scale):
    for hh in range(heads):
        cols = slice(hh * HEAD_DIM, (hh + 1) * HEAD_DIM)
        x = x_ref[:, cols].astype(F32)
        if norm:
            x = x * lax.rsqrt(jnp.mean(x * x, -1, keepdims=True) + LN_EPS) * g_ref[...]
        if rope:
            lane = lax.broadcasted_iota(jnp.int32, x.shape, 1)
            partner = jnp.where(lane % 2 == 0, pltpu.roll(x, HEAD_DIM - 1, 1), pltpu.roll(x, 1, 1))
            x = x * cos_ref[...] + partner * sin_ref[...]
        if scale != 1.0:
            x = x * scale
        o_ref[:, cols] = x.astype(o_ref.dtype)


def _prep(p, col0, n_heads, gain, cos_i, sin_s, *, norm=False, rope=False, scale=1.0):
    t = p.shape[0]
    hb = 4 if n_heads % 4 == 0 else 1
    bw = hb * HEAD_DIM
    assert col0 % bw == 0
    tt = min(t, 1024)
    c0 = col0 // bw
    return pl.pallas_call(
        functools.partial(_prep_kernel, heads=hb, norm=norm, rope=rope, scale=scale),
        grid=(t // tt, n_heads // hb),
        in_specs=[pl.BlockSpec((tt, bw), lambda i, j: (i, c0 + j)),
                  pl.BlockSpec((1, HEAD_DIM), lambda i, j: (0, 0)),
                  pl.BlockSpec((tt, HEAD_DIM), lambda i, j: (i, 0)),
                  pl.BlockSpec((tt, HEAD_DIM), lambda i, j: (i, 0))],
        out_specs=pl.BlockSpec((tt, bw), lambda i, j: (i, j)),
        out_shape=jax.ShapeDtypeStruct((t, n_heads * HEAD_DIM), BF16),
        compiler_params=_params("parallel", "parallel"),
    )(p, gain.reshape(1, HEAD_DIM).astype(F32), cos_i[:t], sin_s[:t])


def _rope_tables(n_tokens):
    t = jnp.arange(n_tokens, dtype=jnp.int32)
    row = (t // GRID_W).astype(F32)
    col = (t % GRID_W).astype(F32)
    n_freq = HEAD_DIM // 4
    inv = ROPE_THETA ** (-jnp.arange(n_freq, dtype=F32) / n_freq)
    ang = jnp.concatenate([row[:, None] * inv, col[:, None] * inv], -1)
    cos, sin = jnp.cos(ang), jnp.sin(ang)
    cos_i = jnp.repeat(cos, 2, axis=-1)
    sin_s = jnp.stack([-sin, sin], -1).reshape(n_tokens, HEAD_DIM)
    return cos_i, sin_s


KV_BLOCKS_PER_ITER = 2


def _flash_kernel(q_ref, k_ref, v_ref, *rest, group, tk, scale, n_side):
    side_in, o_ref, side_out = rest[:n_side], rest[n_side], rest[n_side + 1:2 * n_side + 1]
    m_ref, l_ref, acc_ref = rest[2 * n_side + 1:]
    _SideCast.run(side_in, side_out)
    n_blocks = k_ref.shape[0] // tk
    m_ref[...] = jnp.full_like(m_ref, NEG_INF)
    l_ref[...] = jnp.zeros_like(l_ref)
    acc_ref[...] = jnp.zeros_like(acc_ref)

    def block(start):
        k = k_ref[pl.ds(start, tk), :]
        v = v_ref[pl.ds(start, tk), :]
        for g in range(group):
            q = q_ref[:, g * HEAD_DIM:(g + 1) * HEAD_DIM]
            if scale != 1.0:
                q = (q.astype(F32) * scale).astype(BF16)
            s = lax.dot_general(q, k, _NT, preferred_element_type=F32)
            m_prev = m_ref[g]
            m_new = jnp.maximum(m_prev, jnp.max(s, -1, keepdims=True))
            alpha = jnp.exp2(m_prev - m_new)
            p = jnp.exp2(s - jnp.concatenate([m_new] * (tk // HEAD_DIM), 1))
            l_ref[g] = alpha * l_ref[g] + jnp.sum(p, -1, keepdims=True)
            acc_ref[g] = alpha * acc_ref[g] + jnp.dot(p.astype(BF16), v, preferred_element_type=F32)
            m_ref[g] = m_new

    def group_of_blocks(i, carry):
        for b in range(KV_BLOCKS_PER_ITER):
            block(pl.multiple_of((i * KV_BLOCKS_PER_ITER + b) * tk, tk))
        return carry

    n_iter = n_blocks // KV_BLOCKS_PER_ITER
    lax.fori_loop(0, n_iter, group_of_blocks, 0)
    for b in range(n_iter * KV_BLOCKS_PER_ITER, n_blocks):
        block(b * tk)
    for g in range(group):
        o_ref[:, g * HEAD_DIM:(g + 1) * HEAD_DIM] = (acc_ref[g] / l_ref[g]).astype(o_ref.dtype)


def _flash(q, q_col0, k, k_col0, v, v_col0, *, kv_heads, group, tq, tk, scale=1.0, cast_tables=()):
    n_q, n_k = q.shape[0], k.shape[0]
    tq, tk = min(tq, n_q), min(tk, n_k)
    assert n_q % tq == 0 and n_k % tk == 0 and tk % HEAD_DIM == 0
    qw = group * HEAD_DIM
    assert q_col0 % qw == 0 and k_col0 % HEAD_DIM == 0 and v_col0 % HEAD_DIM == 0
    qc, kc, vc = q_col0 // qw, k_col0 // HEAD_DIM, v_col0 // HEAD_DIM
    n_qt = n_q // tq
    side = _SideCast(cast_tables, kv_heads * n_qt, lambda h, i: h * n_qt + i)
    res = pl.pallas_call(
        functools.partial(_flash_kernel, group=group, tk=tk, scale=scale, n_side=side.n),
        grid=(kv_heads, n_qt),
        in_specs=[pl.BlockSpec((tq, qw), lambda h, i: (i, qc + h)),
                  pl.BlockSpec((n_k, HEAD_DIM), lambda h, i: (0, kc + h)),
                  pl.BlockSpec((n_k, HEAD_DIM), lambda h, i: (0, vc + h))] + side.in_specs,
        out_specs=[pl.BlockSpec((tq, qw), lambda h, i: (i, h))] + side.out_specs,
        out_shape=[jax.ShapeDtypeStruct((n_q, kv_heads * qw), BF16)] + side.out_shapes,
        scratch_shapes=[pltpu.VMEM((group, tq, HEAD_DIM), F32),
                        pltpu.VMEM((group, tq, HEAD_DIM), F32),
                        pltpu.VMEM((group, tq, HEAD_DIM), F32)],
        compiler_params=_params("parallel", "parallel"),
    )(q, k, v, *side.tables)
    return res if side.n else res[0]


def _na_bias_tables(rpb, rows):
    n_heads = rpb.shape[0]
    rg, w, kr, kc = NA_ROWS_PER_STEP, GRID_W, NA_WIN_R, NA_WIN_C
    n_steps = rows // rg
    assert rows % rg == 0 and n_steps >= 3 and kr == 2 * rg and rows >= kr
    c = np.arange(w)
    dc = np.clip(c[None, :] - c[:, None] + (kc - 1), 0, 2 * kc - 2)
    oh_c = (np.arange(2 * kc - 1)[:, None, None] == dc[None]).astype(np.float32)
    rl, krl = np.arange(rg), np.arange(3 * rg)
    da = krl[None, :] - rl[:, None] - rg + (kr - 1)
    assert da.min() >= 0 and da.max() <= 2 * kr - 2
    oh_a = (np.arange(2 * kr - 1)[:, None, None] == da[None]).astype(np.float32)
    t1 = jnp.einsum('hab,bck->hack', rpb.astype(F32), oh_c, precision=lax.Precision.HIGHEST)
    bias = jnp.einsum('hack,arl->hrclk', t1, oh_a, precision=lax.Precision.HIGHEST)
    bias = bias.reshape(n_heads, rg * w, 3 * rg * w)
    c0 = np.clip(c - kc // 2, 0, w - kc)
    col_ok = (c[None, :] >= c0[:, None]) & (c[None, :] < c0[:, None] + kc)
    masks = []
    for g in (0, 1, n_steps - 1):
        r = rg * g + rl
        r0 = np.clip(r - kr // 2, 0, rows - kr)
        key_row = rg * (g - 1) + krl
        row_ok = (key_row[None, :] >= r0[:, None]) & (key_row[None, :] < r0[:, None] + kr)
        ok = row_ok[:, None, :, None] & col_ok[None, :, None, :]
        masks.append(ok.reshape(rg * w, 3 * rg * w))
    masks = np.stack(masks)
    return jnp.where(masks[:, None], bias[None] * LOG2E, NEG_INF)


def _na_kernel(q_ref, kp_ref, kc_ref, kn_ref, vp_ref, vc_ref, vn_ref, kx_ref, vx_ref, b_ref, *rest,
               heads, scale, n_side):
    side_in, o_ref, side_out = rest[:n_side], rest[n_side], rest[n_side + 1:]
    _SideCast.run(side_in, side_out)
    tq = q_ref.shape[0]
    for hh in range(heads):
        cols = slice(hh * HEAD_DIM, (hh + 1) * HEAD_DIM)
        q = (q_ref[:, cols].astype(F32) * scale).astype(BF16)
        ss = []
        for idx, k_ref in enumerate((kp_ref, kc_ref, kn_ref)):
            s = lax.dot_general(q, k_ref[:, cols], _NT, preferred_element_type=F32)
            ss.append(s + b_ref[0, hh, :, idx * tq:(idx + 1) * tq])
        ss.append(lax.dot_general(q, kx_ref[:, cols], _NT, preferred_element_type=F32))
        m = functools.reduce(jnp.maximum, [jnp.max(s, -1, keepdims=True) for s in ss])
        ps = [jnp.exp2(s - m) for s in ss]
        l = functools.reduce(jnp.add, [jnp.sum(p, -1, keepdims=True) for p in ps])
        acc = None
        for p, v_ref in zip(ps, (vp_ref, vc_ref, vn_ref, vx_ref)):
            pv = jnp.dot(p.astype(BF16), v_ref[:, cols], preferred_element_type=F32)
            acc = pv if acc is None else acc + pv
        o_ref[:, cols] = (acc / l).astype(o_ref.dtype)


def _neighbourhood_attention(p, p_ctx, bias, n_heads, q_col0, k_col0, v_col0, cast_tables=()):
    s_len = p.shape[0]
    tq = NA_ROWS_PER_STEP * GRID_W
    n_steps = s_len // tq
    hb = 4 if n_heads % 4 == 0 else 1
    bw = hb * HEAD_DIM
    assert p_ctx.shape[0] == tq
    qc, kc, vc = q_col0 // bw, k_col0 // bw, v_col0 // bw
    prev = lambda g: jnp.maximum(g - 1, 0)
    nxt = lambda g: jnp.minimum(g + 1, n_steps - 1)
    variant = lambda g: jnp.where(g == 0, 0, jnp.where(g == n_steps - 1, 2, 1))
    lat = lambda col, row: pl.BlockSpec((tq, bw), lambda h, g: (row(g), col + h))
    ident = lambda g: g
    side = _SideCast(cast_tables, (n_heads // hb) * n_steps, lambda h, g: h * n_steps + g)
    res = pl.pallas_call(
        functools.partial(_na_kernel, heads=hb, scale=HEAD_DIM ** -0.5 * LOG2E, n_side=side.n),
        grid=(n_heads // hb, n_steps),
        in_specs=[lat(qc, ident),
                  lat(kc, prev), lat(kc, ident), lat(kc, nxt),
                  lat(vc, prev), lat(vc, ident), lat(vc, nxt),
                  pl.BlockSpec((tq, bw), lambda h, g: (0, kc + h)),
                  pl.BlockSpec((tq, bw), lambda h, g: (0, vc + h)),
                  pl.BlockSpec((1, hb, tq, 3 * tq), lambda h, g: (variant(g), h, 0, 0))] + side.in_specs,
        out_specs=[pl.BlockSpec((tq, bw), lambda h, g: (g, h))] + side.out_specs,
        out_shape=[jax.ShapeDtypeStruct((s_len, n_heads * HEAD_DIM), BF16)] + side.out_shapes,
        compiler_params=_params("parallel", "arbitrary"),
    )(p, p, p, p, p, p, p, p_ctx, p_ctx, bias, *side.tables)
    return res if side.n else res[0]


def _swa_kernel(sink_ref, q_ref, kp_ref, kc_ref, kn_ref, vp_ref, vc_ref, vn_ref, kx_ref, vx_ref, mask_ref, o_ref,
                *, group):
    h = pl.program_id(0)
    tq = q_ref.shape[0]
    q = jnp.concatenate([q_ref[:, g * HEAD_DIM:(g + 1) * HEAD_DIM] for g in range(group)], 0)
    ss = []
    for idx, k_ref in enumerate((kp_ref, kc_ref, kn_ref)):
        s = lax.dot_general(q, k_ref[...], _NT, preferred_element_type=F32)
        ss.append(s + mask_ref[0, idx])
    ss.append(lax.dot_general(q, kx_ref[...], _NT, preferred_element_type=F32))
    row_g = lax.broadcasted_iota(jnp.int32, (group * tq, 1), 0) // tq
    sink = jnp.zeros((group * tq, 1), F32)
    for g in range(group):
        sink = jnp.where(row_g == g, sink_ref[h * group + g] * LOG2E, sink)
    m = functools.reduce(jnp.maximum, [jnp.max(s, -1, keepdims=True) for s in ss] + [sink])
    ps = [jnp.exp2(s - m) for s in ss]
    l = functools.reduce(jnp.add, [jnp.sum(p, -1, keepdims=True) for p in ps]) + jnp.exp2(sink - m)
    acc = None
    for p, v_ref in zip(ps, (vp_ref, vc_ref, vn_ref, vx_ref)):
        pv = jnp.dot(p.astype(BF16), v_ref[...], preferred_element_type=F32)
        acc = pv if acc is None else acc + pv
    out = acc / l
    for g in range(group):
        o_ref[:, g * HEAD_DIM:(g + 1) * HEAD_DIM] = out[g * tq:(g + 1) * tq].astype(o_ref.dtype)


def _sliding_window_attention(q, k, v, v_col0, kx, kx_col0, vx, vx_col0, sink, *, kv_heads, group):
    s_len = q.shape[0]
    tq = SW_BLOCK
    assert s_len % tq == 0 and tq >= SW_WINDOW and kx.shape[0] == vx.shape[0]
    n_blk = s_len // tq
    n_ctx = kx.shape[0]
    qw = group * HEAD_DIM
    vc, kxc, vxc = v_col0 // HEAD_DIM, kx_col0 // HEAD_DIM, vx_col0 // HEAD_DIM
    prev = lambda n: jnp.maximum(n - 1, 0)
    nxt = lambda n: jnp.minimum(n + 1, n_blk - 1)
    ident = lambda n: n
    blk = lambda col, row: pl.BlockSpec((tq, HEAD_DIM), lambda h, n: (row(n), col + h))
    assert n_blk >= 3
    ql = np.arange(group * tq)[:, None] % tq
    kl = np.arange(tq)[None, :]
    diff = kl - ql
    ok = np.stack([diff - tq >= -SW_WINDOW, np.abs(diff) <= SW_WINDOW, diff + tq <= SW_WINDOW])
    none = np.zeros_like(ok[0])
    variants = np.stack([np.stack([none, ok[1], ok[2]]), ok, np.stack([ok[0], ok[1], none])])
    mask = jnp.where(variants, 0.0, NEG_INF).astype(F32)
    variant = lambda n: jnp.where(n == 0, 0, jnp.where(n == n_blk - 1, 2, 1))
    return pl.pallas_call(
        functools.partial(_swa_kernel, group=group),
        grid=(kv_heads, n_blk),
        in_specs=[pl.BlockSpec(memory_space=pltpu.SMEM),
                  pl.BlockSpec((tq, qw), lambda h, n: (n, h)),
                  blk(0, prev), blk(0, ident), blk(0, nxt),
                  blk(vc, prev), blk(vc, ident), blk(vc, nxt),
                  pl.BlockSpec((n_ctx, HEAD_DIM), lambda h, n: (0, kxc + h)),
                  pl.BlockSpec((n_ctx, HEAD_DIM), lambda h, n: (0, vxc + h)),
                  pl.BlockSpec((1, 3, group * tq, tq), lambda h, n: (variant(n), 0, 0, 0))],
        out_specs=pl.BlockSpec((tq, qw), lambda h, n: (n, h)),
        out_shape=jax.ShapeDtypeStruct((s_len, kv_heads * qw), BF16),
        compiler_params=_params("parallel", "arbitrary"),
    )(sink.astype(F32), q, k, k, k, v, v, v, kx, vx, mask)


def _ret_kernel(cd_ref, qf_ref, kf_ref, vf_ref, qb_ref, kb_ref, vb_ref, kx_ref, vx_ref,
                intra_ref, qd_ref, kd_ref, wx_ref, of_ref, ob_ref, st_ref, *, heads):
    hb, n = pl.program_id(0), pl.program_id(1)

    @pl.when(n == 0)
    def _():
        for hh in range(heads):
            cols = slice(hh * HEAD_DIM, (hh + 1) * HEAD_DIM)
            for d in range(2):
                kw = (kx_ref[:, cols].astype(F32) * wx_ref[d, hh]).astype(BF16)
                st_ref[d, hh] = lax.dot_general(kw, vx_ref[:, cols], _TN, preferred_element_type=F32)

    for hh in range(heads):
        cols = slice(hh * HEAD_DIM, (hh + 1) * HEAD_DIM)
        head = hb * heads + hh
        for d, (q_ref, k_ref, v_ref, o_ref) in enumerate(((qf_ref, kf_ref, vf_ref, of_ref),
                                                          (qb_ref, kb_ref, vb_ref, ob_ref))):
            q, k, v = q_ref[:, cols], k_ref[:, cols], v_ref[:, cols]
            state = st_ref[d, hh]
            inner = lax.dot_general(q, k, _NT, preferred_element_type=F32) * intra_ref[d, hh]
            qs = (q.astype(F32) * qd_ref[d, hh]).astype(BF16)
            o_ref[:, cols] = (jnp.dot(inner.astype(BF16), v, preferred_element_type=F32)
                              + jnp.dot(qs, state.astype(BF16), preferred_element_type=F32))
            ks = (k.astype(F32) * kd_ref[d, hh]).astype(BF16)
            st_ref[d, hh] = state * cd_ref[d, head] + lax.dot_general(ks, v, _TN, preferred_element_type=F32)


def _retention(rq, rk, p, v_col0, rkx, p_ctx, log_gamma, n_heads):
    s_len = rq.shape[0]
    c = RET_CHUNK
    n_chunks = s_len // c
    n_ctx = rkx.shape[0]
    hb = 8 if n_heads % 8 == 0 else (4 if n_heads % 4 == 0 else 1)
    bw = hb * HEAD_DIM
    assert v_col0 % bw == 0
    vc = v_col0 // bw
    lg = log_gamma.astype(F32)
    pos = jnp.arange(c, dtype=F32)
    diff = pos[:, None] - pos[None, :]
    intra_f = jnp.where(diff >= 0, jnp.exp(lg[0][:, None, None] * jnp.maximum(diff, 0.0)), 0.0)
    intra_b = jnp.where(diff <= 0, jnp.exp(lg[1][:, None, None] * jnp.maximum(-diff, 0.0)), 0.0)
    intra = jnp.stack([intra_f, intra_b])
    lanes = lambda t: jnp.broadcast_to(t[..., None], t.shape + (HEAD_DIM,))
    qd = lanes(jnp.stack([jnp.exp(lg[0][:, None] * (pos + 1.0)), jnp.exp(lg[1][:, None] * (c - pos))]))
    kd = lanes(jnp.stack([jnp.exp(lg[0][:, None] * (c - 1.0 - pos)), jnp.exp(lg[1][:, None] * pos)]))
    cd = jnp.exp(lg * c)
    jx = jnp.arange(n_ctx, dtype=F32)
    wx = lanes(jnp.stack([jnp.exp(lg[0][:, None] * (n_ctx - 1.0 - jx)), jnp.exp(lg[1][:, None] * jx)]))
    fwd = lambda col: pl.BlockSpec((c, bw), lambda h, n: (n, col + h))
    bwd = lambda col: pl.BlockSpec((c, bw), lambda h, n: (n_chunks - 1 - n, col + h))
    tab = lambda rows: pl.BlockSpec((2, hb, rows, HEAD_DIM), lambda h, n: (0, h, 0, 0))
    out_sds = jax.ShapeDtypeStruct((s_len, n_heads * HEAD_DIM), F32)
    return pl.pallas_call(
        functools.partial(_ret_kernel, heads=hb),
        grid=(n_heads // hb, n_chunks),
        in_specs=[pl.BlockSpec(memory_space=pltpu.SMEM),
                  fwd(0), fwd(0), fwd(vc), bwd(0), bwd(0), bwd(vc),
                  pl.BlockSpec((n_ctx, bw), lambda h, n: (0, h)),
                  pl.BlockSpec((n_ctx, bw), lambda h, n: (0, vc + h)),
                  tab(c), tab(c), tab(c), tab(n_ctx)],
        out_specs=[fwd(0), bwd(0)],
        out_shape=[out_sds, out_sds],
        scratch_shapes=[pltpu.VMEM((2, hb, HEAD_DIM, HEAD_DIM), F32)],
        compiler_params=_params("parallel", "arbitrary"),
    )(cd, rq, rk, p, rq, rk, p, rkx, p_ctx, intra, qd, kd, wx)


def _ret_out_kernel(of_ref, ob_ref, g_ref, o_ref, *, heads):
    for hh in range(heads):
        cols = slice(hh * HEAD_DIM, (hh + 1) * HEAD_DIM)
        o = of_ref[:, cols] + ob_ref[:, cols]
        mu = jnp.mean(o, -1, keepdims=True)
        var = jnp.mean(jnp.square(o - mu), -1, keepdims=True)
        gate = g_ref[:, cols].astype(F32)
        gate = gate * (1.0 / (1.0 + jnp.exp(-gate)))
        o_ref[:, cols] = (gate * ((o - mu) * lax.rsqrt(var + LN_EPS))).astype(o_ref.dtype)


def _retention_output(o_f, o_b, p, gate_col0, n_heads):
    s_len = o_f.shape[0]
    hb = 4 if n_heads % 4 == 0 else 1
    bw = hb * HEAD_DIM
    gc = gate_col0 // bw
    tt = min(s_len, 512)
    spec = pl.BlockSpec((tt, bw), lambda i, j: (i, j))
    return pl.pallas_call(
        functools.partial(_ret_out_kernel, heads=hb),
        grid=(s_len // tt, n_heads // hb),
        in_specs=[spec, spec, pl.BlockSpec((tt, bw), lambda i, j: (i, gc + j))],
        out_specs=spec,
        out_shape=jax.ShapeDtypeStruct((s_len, n_heads * HEAD_DIM), BF16),
        compiler_params=_params("parallel", "parallel"),
    )(o_f, o_b, p)


def _residual_ln(x, y, gate, g, b, alpha):
    h = alpha * x + gate * y
    mu = jnp.mean(h, -1, keepdims=True)
    var = jnp.mean(jnp.square(h - mu), -1, keepdims=True)
    return (h - mu) * lax.rsqrt(var + LN_EPS) * g + b


def _outproj_kernel(ya_ref, yb_ref, w_ref, x_ref, gate_ref, g_ref, b_ref, o_ref, *, alpha):
    wa = ya_ref.shape[1]
    o_ref[...] = (jnp.dot(ya_ref[...], w_ref[:wa, :], preferred_element_type=F32)
                  + jnp.dot(yb_ref[...], w_ref[wa:, :], preferred_element_type=F32))
    for rows in _row_chunks(x_ref.shape[0], 64):
        o_ref[rows, :] = _residual_ln(x_ref[rows, :], o_ref[rows, :], gate_ref[...], g_ref[...], b_ref[...], alpha)


def _outproj_ln(ya, yb, w, x, gate, g, b, alpha, tm):
    m, wa = ya.shape
    wb = yb.shape[1]
    d = w.shape[1]
    tm = min(tm, m)
    vec = pl.BlockSpec((1, d), lambda i: (0, 0))
    return pl.pallas_call(
        functools.partial(_outproj_kernel, alpha=alpha),
        grid=(m // tm,),
        in_specs=[pl.BlockSpec((tm, wa), lambda i: (i, 0)),
                  pl.BlockSpec((tm, wb), lambda i: (i, 0)),
                  pl.BlockSpec((wa + wb, d), lambda i: (0, 0), pipeline_mode=pl.Buffered(1)),
                  pl.BlockSpec((tm, d), lambda i: (i, 0)),
                  vec, vec, vec],
        out_specs=pl.BlockSpec((tm, d), lambda i: (i, 0)),
        out_shape=jax.ShapeDtypeStruct((m, d), F32),
        compiler_params=_params("parallel", vmem=VMEM_LIMIT_RESIDENT_BYTES),
    )(ya, yb, w, x, gate, g.reshape(1, d), b.reshape(1, d))


def _add_ln_kernel(x_ref, f_ref, gate_ref, g_ref, b_ref, o_ref, *, alpha):
    o_ref[...] = _residual_ln(x_ref[...], f_ref[...].astype(F32), gate_ref[...], g_ref[...], b_ref[...], alpha)


def _add_ln(x, f, gate, g, b, alpha):
    m, d = x.shape
    tt = min(m, 256)
    row = pl.BlockSpec((tt, d), lambda i: (i, 0))
    vec = pl.BlockSpec((1, d), lambda i: (0, 0))
    return pl.pallas_call(
        functools.partial(_add_ln_kernel, alpha=alpha),
        grid=(m // tt,),
        in_specs=[row, row, vec, vec, vec],
        out_specs=row,
        out_shape=jax.ShapeDtypeStruct((m, d), F32),
        compiler_params=_params("parallel"),
    )(x, f, gate, g.reshape(1, d), b.reshape(1, d))


PEER_LIST = 24


def _peer_gate_kernel(q_ref, keys_ref, d1_ref, f1_ref, s2_ref, e2_ref, a_ref, b_ref, cand_ref):
    half = keys_ref.shape[-1]
    k = PEER_TOPK
    s1 = lax.dot_general(keys_ref[0, 0], q_ref[:, :half].astype(BF16), _NT, preferred_element_type=F32)
    s2 = lax.dot_general(keys_ref[0, 1], q_ref[:, half:].astype(BF16), _NT, preferred_element_type=F32)

    def top_rows(s, dst_ref):
        dst_ref[...] = jnp.full_like(dst_ref, -jnp.inf)
        n_slabs = s.shape[0] // 8
        if n_slabs & (n_slabs - 1):
            cur = s
            for r in range(k + 1):
                m = jnp.max(cur, 0, keepdims=True)
                dst_ref[r:r + 1, :] = m
                cur = jnp.where(cur == m, -jnp.inf, cur)
            return
        v = [s[8 * i:8 * i + 8, :] for i in range(n_slabs)]
        size = 2
        while size <= n_slabs:
            stride = size // 2
            while stride >= 1:
                for i in range(n_slabs):
                    j = i ^ stride
                    if j > i:
                        hi, lo = jnp.maximum(v[i], v[j]), jnp.minimum(v[i], v[j])
                        v[i], v[j] = (hi, lo) if (i & size) == 0 else (lo, hi)
                stride //= 2
            size *= 2
        depth = min(n_slabs, k + 1)
        v = v[:depth] + [jnp.full_like(v[0], -jnp.inf)]
        for r in range(k + 1):
            m = jnp.max(v[0], 0, keepdims=True)
            dst_ref[r:r + 1, :] = m
            win = v[0] == m
            for i in range(depth):
                v[i] = jnp.where(win, v[i + 1], v[i])

    top_rows(s1, a_ref)
    top_rows(s2, b_ref)
    half_k = k // 2
    cand_ref[0:PEER_LIST, :] = a_ref[0:1, :] + b_ref[...]
    for r in range(1, half_k):
        cand_ref[PEER_LIST + (r - 1) * half_k:PEER_LIST + r * half_k, :] = a_ref[r:r + 1, :] + b_ref[0:half_k, :]
    cand_ref[PEER_LIST + (half_k - 1) * half_k:, :] = a_ref[half_k:, :] + b_ref[0:1, :]
    best = a_ref[0:1, :] + b_ref[0:1, :]
    cur = cand_ref[...]
    z = jnp.zeros_like(best)
    kth = best
    for r in range(k):
        kth = jnp.max(cur, 0, keepdims=True)
        z = z + jnp.exp(kth - best)
        cur = jnp.where(cur == kth, -jnp.inf, cur)
    tau = 0.5 * (kth + jnp.max(cur, 0, keepdims=True))
    d1_ref[0] = tau - s1
    f1_ref[0] = jnp.exp(s1 - a_ref[0:1, :]) / z
    s2_ref[0] = s2
    e2_ref[0] = jnp.exp(s2 - b_ref[0:1, :])


def _peer_gates(q, sub_keys, tm):
    t = q.shape[0]
    n_heads, _, n_keys, half = sub_keys.shape
    tm = min(tm, t)
    big = jax.ShapeDtypeStruct((n_heads, n_keys, t), F32)
    blk = pl.BlockSpec((1, n_keys, tm), lambda i, h: (h, 0, i))
    return pl.pallas_call(
        _peer_gate_kernel,
        grid=(t // tm, n_heads),
        in_specs=[pl.BlockSpec((tm, 2 * half), lambda i, h: (i, h)),
                  pl.BlockSpec((1, 2, n_keys, half), lambda i, h: (h, 0, 0, 0))],
        out_specs=[blk, blk, blk, blk],
        out_shape=[big, big, big, big],
        scratch_shapes=[pltpu.VMEM((PEER_LIST, tm), F32), pltpu.VMEM((PEER_LIST, tm), F32),
                        pltpu.VMEM((2 * PEER_LIST - PEER_TOPK // 2 + (PEER_TOPK // 2 - 1) * (PEER_TOPK // 2), tm),
                                   F32)],
        compiler_params=_params("parallel", "parallel"),
    )(q, sub_keys)


def _gelu_tanh(x):
    c = math.sqrt(2.0 / math.pi)
    hx = 0.5 * x
    return hx + hx * jnp.tanh(x * (c + (c * 0.044715) * (x * x)))


def _peer_kernel(h_ref, u_ref, v_ref, d1_ref, f1_ref, s2_ref, e2_ref, *rest, n_side):
    side_in, o_ref, side_out = rest[:n_side], rest[n_side], rest[n_side + 1:2 * n_side + 1]
    acc_ref, g_ref = rest[2 * n_side + 1:]
    _SideCast.run(side_in, side_out)
    cj, ci = pl.program_id(1), pl.program_id(2)
    n_heads, ni, tm = d1_ref.shape
    nj = s2_ref.shape[1]
    d = h_ref.shape[1]

    @pl.when((cj == 0) & (ci == 0))
    def _():
        acc_ref[...] = jnp.zeros_like(acc_ref)

    rows = 32
    for ii in range(ni):
        for r0 in range(0, nj, rows):
            for t0 in range(0, tm, 128):
                tok = slice(t0, t0 + 128)
                g = None
                for h in range(n_heads):
                    w = f1_ref[h, ii:ii + 1, tok] * e2_ref[h, r0:r0 + rows, tok]
                    sel = jnp.where(s2_ref[h, r0:r0 + rows, tok] >= d1_ref[h, ii:ii + 1, tok], w, 0.0)
                    g = sel if g is None else g + sel
                g_ref[ii * nj + r0:ii * nj + r0 + rows, tok] = g
    u = u_ref[...].reshape(ni * nj, d)
    act = _gelu_tanh(lax.dot_general(u, h_ref[...], _NT, preferred_element_type=F32))
    wt = (g_ref[...] * act).astype(BF16)
    acc_ref[...] += lax.dot_general(wt, v_ref[...].reshape(ni * nj, d), _TN, preferred_element_type=F32)

    @pl.when((cj == pl.num_programs(1) - 1) & (ci == pl.num_programs(2) - 1))
    def _():
        o_ref[...] = acc_ref[...].astype(o_ref.dtype)


def _peer_dense(hf, u3, v3, d1, f1, s2, e2, tm, cast_tables=()):
    t, d = hf.shape
    n_heads, n_keys = d1.shape[0], d1.shape[1]
    tm = min(tm, t)
    ni, nj = PEER_NI, PEER_NJ
    n_cj, n_ci = n_keys // nj, n_keys // ni
    tab = pl.BlockSpec((ni, nj, d), lambda i, cj, ci: (ci, cj, 0))
    first = pl.BlockSpec((n_heads, ni, tm), lambda i, cj, ci: (0, ci, i))
    second = pl.BlockSpec((n_heads, nj, tm), lambda i, cj, ci: (0, cj, i))
    side = _SideCast(cast_tables, (t // tm) * n_cj * n_ci, lambda i, cj, ci: (i * n_cj + cj) * n_ci + ci)
    res = pl.pallas_call(
        functools.partial(_peer_kernel, n_side=side.n),
        grid=(t // tm, n_cj, n_ci),
        in_specs=[pl.BlockSpec((tm, d), lambda i, cj, ci: (i, 0)), tab, tab, first, first, second, second]
        + side.in_specs,
        out_specs=[pl.BlockSpec((tm, d), lambda i, cj, ci: (i, 0))] + side.out_specs,
        out_shape=[jax.ShapeDtypeStruct((t, d), BF16)] + side.out_shapes,
        scratch_shapes=[pltpu.VMEM((tm, d), F32), pltpu.VMEM((ni * nj, tm), F32)],
        compiler_params=_params("parallel", "arbitrary", "arbitrary"),
    )(hf, u3, v3, d1, f1, s2, e2, *side.tables)
    return res if side.n else res[0]


def _peer_block(x, shift, scale, gate, w_q, sub_keys, u3, v3, g, b, alpha, tm, cast_tables=()):
    q, hf = _proj(x, shift, scale, w_q, F32, tm, emit_h=True, resident_w=True)
    d1, f1, s2, e2 = _peer_gates(q, sub_keys, tm)
    res = _peer_dense(hf, u3, v3, d1, f1, s2, e2, tm, cast_tables)
    f, casts = (res[0], tuple(res[1:])) if cast_tables else (res, ())
    return _add_ln(x, f, gate, g, b, alpha), casts


def kernel(x, c, ctx, c_ctx, ada_w, ada_b, ln_g, ln_b, ev_w_in, ev_w_out, ev_na_rpb, ev_ga_q_gain, ev_ga_k_gain,
           od_w_in, od_w_out, od_ret_decay_exp, od_sw_sink, peer_w_q, peer_sub_keys, peer_u, peer_v):
    assert x.shape[0] == 1 and ada_w.shape[0] == 2
    depth = ada_w.shape[0]
    _, s_len, d = x.shape
    hd = HEAD_DIM
    mix_heads = d // (2 * hd)
    kv_heads = mix_heads // 4
    group = mix_heads // kv_heads
    alpha = (2 * depth) ** 0.25
    att_scale = hd ** -0.5
    sm_scale = att_scale * LOG2E
    n_keys = peer_sub_keys.shape[3]
    tm = 512
    tm_out = 256

    xs, cs = x[0], ctx[0]
    mods = _ada_modulation(c, c_ctx, ada_w, ada_b)
    cos_i, sin_s = _rope_tables(s_len)
    ones = jnp.ones((hd,), F32)

    def mod6(layer, which):
        return [m.reshape(1, d) for m in jnp.split(mods[layer, which], 6)]

    def peer_args(layer, wq_bf16, u_bf16, v_bf16):
        return (wq_bf16, peer_sub_keys[layer].astype(BF16),
                u_bf16.reshape(n_keys, n_keys, d), v_bf16.reshape(n_keys, n_keys, d),
                ln_g[layer, 1], ln_b[layer, 1], alpha)

    layer0_weights = ((ev_w_out, 0), (peer_w_q, 0))
    layer1_weights = ((peer_u, 1), (peer_v, 1), (od_w_in, 0), (od_w_out, 0), (peer_w_q, 1))

    sh_a, sc_a, g_a, sh_f, sc_f, g_f = mod6(0, 0)
    csh_a, csc_a, cg_a, csh_f, csc_f, cg_f = mod6(0, 1)
    w_in = _to_bf16(ev_w_in, 0)
    p = _proj(xs, sh_a, sc_a, w_in, BF16, tm)
    pc = _proj(cs, csh_a, csc_a, w_in, BF16, tm)
    mh = mix_heads * hd
    kvw = kv_heads * hd
    na_q, na_k, na_v, ga_q, ga_k, ga_v = 0, mh, 2 * mh, 3 * mh, 4 * mh, 4 * mh + kvw
    qg = _prep(p, ga_q, mix_heads, ev_ga_q_gain[0], cos_i, sin_s, norm=True, rope=True, scale=sm_scale)
    kg = _prep(p, ga_k, kv_heads, ev_ga_k_gain[0], cos_i, sin_s, norm=True, rope=True)
    qgc = _prep(pc, ga_q, mix_heads, ev_ga_q_gain[0], cos_i, sin_s, norm=True, scale=sm_scale)
    kgc = _prep(pc, ga_k, kv_heads, ev_ga_k_gain[0], cos_i, sin_s, norm=True)
    k_all = jnp.concatenate([kg, kgc], 0)
    v_all = jnp.concatenate([p[:, ga_v:ga_v + kvw], pc[:, ga_v:ga_v + kvw]], 0)
    tk = _pick(k_all.shape[0], 1024, 256)
    y_ga, u0, v0 = _flash(qg, 0, k_all, 0, v_all, 0, kv_heads=kv_heads, group=group, tq=512, tk=tk,
                          cast_tables=((peer_u, 0), (peer_v, 0)))
    yc_ga = _flash(qgc, 0, kgc, 0, pc, ga_v, kv_heads=kv_heads, group=group, tq=256, tk=256)
    bias = _na_bias_tables(ev_na_rpb[0], s_len // GRID_W)
    y_na, w_out, wq0 = _neighbourhood_attention(p, pc, bias, mix_heads, na_q, na_k, na_v,
                                                cast_tables=layer0_weights)
    yc_na = _flash(pc, na_q, pc, na_k, pc, na_v, kv_heads=mix_heads, group=1, tq=256, tk=256, scale=sm_scale)
    xs = _outproj_ln(y_na, y_ga, w_out, xs, g_a, ln_g[0, 0], ln_b[0, 0], alpha, tm_out)
    cs = _outproj_ln(yc_na, yc_ga, w_out, cs, cg_a, ln_g[0, 0], ln_b[0, 0], alpha, tm_out)
    pa = peer_args(0, wq0, u0, v0)
    xs, (u1, v1, w_in1, w_out1, wq1) = _peer_block(xs, sh_f, sc_f, g_f, *pa, tm, cast_tables=layer1_weights)
    cs, _ = _peer_block(cs, csh_f, csc_f, cg_f, *pa, tm)

    sh_a, sc_a, g_a, sh_f, sc_f, g_f = mod6(1, 0)
    csh_a, csc_a = mod6(1, 1)[:2]
    p = _proj(xs, sh_a, sc_a, w_in1, BF16, tm)
    pc = _proj(cs, csh_a, csc_a, w_in1, BF16, tm)
    r_q, r_k, r_v, r_g, s_q, s_k, s_v = 0, mh, 2 * mh, 3 * mh, 4 * mh, 5 * mh, 5 * mh + kvw
    rq = _prep(p, r_q, mix_heads, ones, cos_i, sin_s, rope=True)
    rk = _prep(p, r_k, mix_heads, ones, cos_i, sin_s, rope=True, scale=att_scale)
    rkc = _prep(pc, r_k, mix_heads, ones, cos_i, sin_s, scale=att_scale)
    sq = _prep(p, s_q, mix_heads, ones, cos_i, sin_s, rope=True, scale=sm_scale)
    sk = _prep(p, s_k, kv_heads, ones, cos_i, sin_s, rope=True)
    log_gamma = jnp.log1p(-jnp.exp2(-od_ret_decay_exp[0].astype(F32)))
    o_f, o_b = _retention(rq, rk, p, r_v, rkc, pc, log_gamma, mix_heads)
    y_ret = _retention_output(o_f, o_b, p, r_g, mix_heads)
    y_sw = _sliding_window_attention(sq, sk, p, s_v, pc, s_k, pc, s_v, od_sw_sink[0],
                                     kv_heads=kv_heads, group=group)
    xs = _outproj_ln(y_ret, y_sw, w_out1, xs, g_a, ln_g[1, 0], ln_b[1, 0], alpha, tm_out)
    xs, _ = _peer_block(xs, sh_f, sc_f, g_f, *peer_args(1, wq1, u1, v1), tm)
    return xs[None]
```

```python
import functools
import math

import numpy as np
import jax
import jax.numpy as jnp
from jax import lax
from jax.experimental import pallas as pl
from jax.experimental.pallas import tpu as pltpu

F32 = jnp.float32
BF16 = jnp.bfloat16

GRID_W = 64
HEAD_DIM = 128
NA_WIN_R = 8
NA_WIN_C = 16
NA_ROWS_PER_STEP = 4
SW_WINDOW = 128
SW_BLOCK = 256
RET_CHUNK = 128
ROPE_THETA = 10000.0
PEER_TOPK = 16
LN_EPS = 1e-6
NEG_INF = -1e30
LOG2E = math.log2(math.e)
VMEM_LIMIT_BYTES = 56 * 1024 * 1024
VMEM_LIMIT_RESIDENT_BYTES = 60 * 1024 * 1024

_NT = (((1,), (1,)), ((), ()))
_TN = (((0,), (0,)), ((), ()))


def _params(*sem, vmem=VMEM_LIMIT_BYTES):
    return pltpu.CompilerParams(dimension_semantics=sem, vmem_limit_bytes=vmem)


def _row_chunks(n, size=128):
    return [slice(r, min(r + size, n)) for r in range(0, n, size)]


def _pick(n, cap, mult=128):
    best = None
    for t in range(mult, min(n, cap) + 1, mult):
        if n % t == 0:
            best = t
    assert best is not None, (n, cap, mult)
    return best


def _cast_kernel(x_ref, o_ref):
    o_ref[...] = x_ref[...].astype(o_ref.dtype)


def _to_bf16(w, layer):
    _, rows, cols = w.shape
    tr = _pick(rows, max(8, (2 * 1024 * 1024) // cols), 8)
    return pl.pallas_call(
        _cast_kernel,
        grid=(rows // tr,),
        in_specs=[pl.BlockSpec((None, tr, cols), lambda i: (layer, i, 0))],
        out_specs=pl.BlockSpec((tr, cols), lambda i: (i, 0)),
        out_shape=jax.ShapeDtypeStruct((rows, cols), BF16),
        compiler_params=_params("parallel"),
    )(w)


PEER_NI, PEER_NJ = 8, 64


class _SideCast:
    def __init__(self, tables, n_steps, step_of):
        self.n = len(tables)
        self.tables = [w for w, _ in tables]
        self.in_specs, self.out_specs, self.out_shapes = [], [], []
        for w, layer in tables:
            _, rows, d = w.shape
            every = 1
            while rows * every % n_steps or (rows * every // n_steps) % 16:
                every *= 2
                assert every <= n_steps, (w.shape, n_steps)
            r = rows * every // n_steps
            self.in_specs.append(pl.BlockSpec(
                (None, r, d), lambda *ids, layer=layer, every=every: (layer, step_of(*ids) // every, 0)))
            self.out_specs.append(pl.BlockSpec((r, d), lambda *ids, every=every: (step_of(*ids) // every, 0)))
            self.out_shapes.append(jax.ShapeDtypeStruct((rows, d), BF16))

    @staticmethod
    def run(in_refs, out_refs):
        for src, dst in zip(in_refs, out_refs):
            for rows in _row_chunks(src.shape[0], 64):
                dst[rows, :] = src[rows, :].astype(dst.dtype)


def _ada_kernel(c_ref, w_ref, b_ref, o_ref):
    c = c_ref[...]
    s = c * (1.0 / (1.0 + jnp.exp(-c)))
    o_ref[0] = jnp.dot(s.astype(BF16), w_ref[0].astype(BF16), preferred_element_type=F32) + b_ref[0]


def _ada_modulation(c, c_ctx, ada_w, ada_b):
    depth, d, n = ada_w.shape
    cc = jnp.zeros((8, d), F32).at[0].set(c[0]).at[1].set(c_ctx)
    tn = _pick(n, 512)
    out = pl.pallas_call(
        _ada_kernel,
        grid=(depth, n // tn),
        in_specs=[pl.BlockSpec((8, d), lambda l, j: (0, 0)),
                  pl.BlockSpec((1, d, tn), lambda l, j: (l, 0, j)),
                  pl.BlockSpec((1, 1, tn), lambda l, j: (l, 0, j))],
        out_specs=pl.BlockSpec((1, 8, tn), lambda l, j: (l, 0, j)),
        out_shape=jax.ShapeDtypeStruct((depth, 8, n), F32),
        compiler_params=_params("parallel", "parallel"),
    )(cc, ada_w, ada_b.reshape(depth, 1, n))
    return out


def _proj_kernel(x_ref, sh_ref, sc_ref, w_ref, o_ref, *rest, emit_h):
    h_ref = rest[-1]

    @pl.when(pl.program_id(1) == 0)
    def _():
        for rows in _row_chunks(x_ref.shape[0]):
            h = (x_ref[rows, :] * (1.0 + sc_ref[...]) + sh_ref[...]).astype(BF16)
            h_ref[rows, :] = h
            if emit_h:
                rest[0][rows, :] = h

    o_ref[...] = jnp.dot(h_ref[...], w_ref[...], preferred_element_type=F32).astype(o_ref.dtype)


def _proj(x, shift, scale, w, out_dtype, tm, emit_h=False, resident_w=False):
    m, d = x.shape
    n = w.shape[1]
    tm = min(tm, m)
    tn = n if resident_w else _pick(n, 1024)
    w_spec = (pl.BlockSpec((d, tn), lambda i, j: (0, j), pipeline_mode=pl.Buffered(1)) if resident_w
              else pl.BlockSpec((d, tn), lambda i, j: (0, j)))
    out_shape = [jax.ShapeDtypeStruct((m, n), out_dtype)]
    out_specs = [pl.BlockSpec((tm, tn), lambda i, j: (i, j))]
    if emit_h:
        out_shape.append(jax.ShapeDtypeStruct((m, d), BF16))
        out_specs.append(pl.BlockSpec((tm, d), lambda i, j: (i, 0)))
    res = pl.pallas_call(
        functools.partial(_proj_kernel, emit_h=emit_h),
        grid=(m // tm, n // tn),
        in_specs=[pl.BlockSpec((tm, d), lambda i, j: (i, 0)),
                  pl.BlockSpec((1, d), lambda i, j: (0, 0)),
                  pl.BlockSpec((1, d), lambda i, j: (0, 0)),
                  w_spec],
        out_specs=out_specs,
        out_shape=out_shape,
        scratch_shapes=[pltpu.VMEM((tm, d), BF16)],
        compiler_params=_params("parallel", "arbitrary"),
    )(x, shift, scale, w)
    return res if emit_h else res[0]


def _prep_kernel(x_ref, g_ref, cos_ref, sin_ref, o_ref, *, heads, norm, rope, scale):
    for hh in range(heads):
        cols = slice(hh * HEAD_DIM, (hh + 1) * HEAD_DIM)
        x = x_ref[:, cols].astype(F32)
        if norm:
            x = x * lax.rsqrt(jnp.mean(x * x, -1, keepdims=True) + LN_EPS) * g_ref[...]
        if rope:
            lane = lax.broadcasted_iota(jnp.int32, x.shape, 1)
            partner = jnp.where(lane % 2 == 0, pltpu.roll(x, HEAD_DIM - 1, 1), pltpu.roll(x, 1, 1))
            x = x * cos_ref[...] + partner * sin_ref[...]
        if scale != 1.0:
            x = x * scale
        o_ref[:, cols] = x.astype(o_ref.dtype)


def _prep(p, col0, n_heads, gain, cos_i, sin_s, *, norm=False, rope=False, scale=1.0):
    t = p.shape[0]
    hb = 4 if n_heads % 4 == 0 else 1
    bw = hb * HEAD_DIM
    assert col0 % bw == 0
    tt = min(t, 1024)
    c0 = col0 // bw
    return pl.pallas_call(
        functools.partial(_prep_kernel, heads=hb, norm=norm, rope=rope, scale=scale),
        grid=(t // tt, n_heads // hb),
        in_specs=[pl.BlockSpec((tt, bw), lambda i, j: (i, c0 + j)),
                  pl.BlockSpec((1, HEAD_DIM), lambda i, j: (0, 0)),
                  pl.BlockSpec((tt, HEAD_DIM), lambda i, j: (i, 0)),
                  pl.BlockSpec((tt, HEAD_DIM), lambda i, j: (i, 0))],
        out_specs=pl.BlockSpec((tt, bw), lambda i, j: (i, j)),
        out_shape=jax.ShapeDtypeStruct((t, n_heads * HEAD_DIM), BF16),
        compiler_params=_params("parallel", "parallel"),
    )(p, gain.reshape(1, HEAD_DIM).astype(F32), cos_i[:t], sin_s[:t])


def _rope_tables(n_tokens):
    t = jnp.arange(n_tokens, dtype=jnp.int32)
    row = (t // GRID_W).astype(F32)
    col = (t % GRID_W).astype(F32)
    n_freq = HEAD_DIM // 4
    inv = ROPE_THETA ** (-jnp.arange(n_freq, dtype=F32) / n_freq)
    ang = jnp.concatenate([row[:, None] * inv, col[:, None] * inv], -1)
    cos, sin = jnp.cos(ang), jnp.sin(ang)
    cos_i = jnp.repeat(cos, 2, axis=-1)
    sin_s = jnp.stack([-sin, sin], -1).reshape(n_tokens, HEAD_DIM)
    return cos_i, sin_s


KV_BLOCKS_PER_ITER = 2


def _flash_kernel(q_ref, k_ref, v_ref, *rest, group, tk, scale, n_side):
    side_in, o_ref, side_out = rest[:n_side], rest[n_side], rest[n_side + 1:2 * n_side + 1]
    m_ref, l_ref, acc_ref = rest[2 * n_side + 1:]
    _SideCast.run(side_in, side_out)
    n_blocks = k_ref.shape[0] // tk
    m_ref[...] = jnp.full_like(m_ref, NEG_INF)
    l_ref[...] = jnp.zeros_like(l_ref)
    acc_ref[...] = jnp.zeros_like(acc_ref)

    def block(start):
        k = k_ref[pl.ds(start, tk), :]
        v = v_ref[pl.ds(start, tk), :]
        for g in range(group):
            q = q_ref[:, g * HEAD_DIM:(g + 1) * HEAD_DIM]
            if scale != 1.0:
                q = (q.astype(F32) * scale).astype(BF16)
            s = lax.dot_general(q, k, _NT, preferred_element_type=F32)
            m_prev = m_ref[g]
            m_new = jnp.maximum(m_prev, jnp.max(s, -1, keepdims=True))
            alpha = jnp.exp2(m_prev - m_new)
            p = jnp.exp2(s - jnp.concatenate([m_new] * (tk // HEAD_DIM), 1))
            l_ref[g] = alpha * l_ref[g] + jnp.sum(p, -1, keepdims=True)
            acc_ref[g] = alpha * acc_ref[g] + jnp.dot(p.astype(BF16), v, preferred_element_type=F32)
            m_ref[g] = m_new

    def group_of_blocks(i, carry):
        for b in range(KV_BLOCKS_PER_ITER):
            block(pl.multiple_of((i * KV_BLOCKS_PER_ITER + b) * tk, tk))
        return carry

    n_iter = n_blocks // KV_BLOCKS_PER_ITER
    lax.fori_loop(0, n_iter, group_of_blocks, 0)
    for b in range(n_iter * KV_BLOCKS_PER_ITER, n_blocks):
        block(b * tk)
    for g in range(group):
        o_ref[:, g * HEAD_DIM:(g + 1) * HEAD_DIM] = (acc_ref[g] / l_ref[g]).astype(o_ref.dtype)


def _flash(q, q_col0, k, k_col0, v, v_col0, *, kv_heads, group, tq, tk, scale=1.0, cast_tables=()):
    n_q, n_k = q.shape[0], k.shape[0]
    tq, tk = min(tq, n_q), min(tk, n_k)
    assert n_q % tq == 0 and n_k % tk == 0 and tk % HEAD_DIM == 0
    qw = group * HEAD_DIM
    assert q_col0 % qw == 0 and k_col0 % HEAD_DIM == 0 and v_col0 % HEAD_DIM == 0
    qc, kc, vc = q_col0 // qw, k_col0 // HEAD_DIM, v_col0 // HEAD_DIM
    n_qt = n_q // tq
    side = _SideCast(cast_tables, kv_heads * n_qt, lambda h, i: h * n_qt + i)
    res = pl.pallas_call(
        functools.partial(_flash_kernel, group=group, tk=tk, scale=scale, n_side=side.n),
        grid=(kv_heads, n_qt),
        in_specs=[pl.BlockSpec((tq, qw), lambda h, i: (i, qc + h)),
                  pl.BlockSpec((n_k, HEAD_DIM), lambda h, i: (0, kc + h)),
                  pl.BlockSpec((n_k, HEAD_DIM), lambda h, i: (0, vc + h))] + side.in_specs,
        out_specs=[pl.BlockSpec((tq, qw), lambda h, i: (i, h))] + side.out_specs,
        out_shape=[jax.ShapeDtypeStruct((n_q, kv_heads * qw), BF16)] + side.out_shapes,
        scratch_shapes=[pltpu.VMEM((group, tq, HEAD_DIM), F32),
                        pltpu.VMEM((group, tq, HEAD_DIM), F32),
                        pltpu.VMEM((group, tq, HEAD_DIM), F32)],
        compiler_params=_params("parallel", "parallel"),
    )(q, k, v, *side.tables)
    return res if side.n else res[0]


def _na_bias_tables(rpb, rows):
    n_heads = rpb.shape[0]
    rg, w, kr, kc = NA_ROWS_PER_STEP, GRID_W, NA_WIN_R, NA_WIN_C
    n_steps = rows // rg
    assert rows % rg == 0 and n_steps >= 3 and kr == 2 * rg and rows >= kr
    c = np.arange(w)
    dc = np.clip(c[None, :] - c[:, None] + (kc - 1), 0, 2 * kc - 2)
    oh_c = (np.arange(2 * kc - 1)[:, None, None] == dc[None]).astype(np.float32)
    rl, krl = np.arange(rg), np.arange(3 * rg)
    da = krl[None, :] - rl[:, None] - rg + (kr - 1)
    assert da.min() >= 0 and da.max() <= 2 * kr - 2
    oh_a = (np.arange(2 * kr - 1)[:, None, None] == da[None]).astype(np.float32)
    t1 = jnp.einsum('hab,bck->hack', rpb.astype(F32), oh_c, precision=lax.Precision.HIGHEST)
    bias = jnp.einsum('hack,arl->hrclk', t1, oh_a, precision=lax.Precision.HIGHEST)
    bias = bias.reshape(n_heads, rg * w, 3 * rg * w)
    c0 = np.clip(c - kc // 2, 0, w - kc)
    col_ok = (c[None, :] >= c0[:, None]) & (c[None, :] < c0[:, None] + kc)
    masks = []
    for g in (0, 1, n_steps - 1):
        r = rg * g + rl
        r0 = np.clip(r - kr // 2, 0, rows - kr)
        key_row = rg * (g - 1) + krl
        row_ok = (key_row[None, :] >= r0[:, None]) & (key_row[None, :] < r0[:, None] + kr)
        ok = row_ok[:, None, :, None] & col_ok[None, :, None, :]
        masks.append(ok.reshape(rg * w, 3 * rg * w))
    masks = np.stack(masks)
    return jnp.where(masks[:, None], bias[None] * LOG2E, NEG_INF)


def _na_kernel(q_ref, kp_ref, kc_ref, kn_ref, vp_ref, vc_ref, vn_ref, kx_ref, vx_ref, b_ref, *rest,
               heads, scale, n_side):
    side_in, o_ref, side_out = rest[:n_side], rest[n_side], rest[n_side + 1:]
    _SideCast.run(side_in, side_out)
    tq = q_ref.shape[0]
    for hh in range(heads):
        cols = slice(hh * HEAD_DIM, (hh + 1) * HEAD_DIM)
        q = (q_ref[:, cols].astype(F32) * scale).astype(BF16)
        ss = []
        for idx, k_ref in enumerate((kp_ref, kc_ref, kn_ref)):
            s = lax.dot_general(q, k_ref[:, cols], _NT, preferred_element_type=F32)
            ss.append(s + b_ref[0, hh, :, idx * tq:(idx + 1) * tq])
        ss.append(lax.dot_general(q, kx_ref[:, cols], _NT, preferred_element_type=F32))
        m = functools.reduce(jnp.maximum, [jnp.max(s, -1, keepdims=True) for s in ss])
        ps = [jnp.exp2(s - m) for s in ss]
        l = functools.reduce(jnp.add, [jnp.sum(p, -1, keepdims=True) for p in ps])
        acc = None
        for p, v_ref in zip(ps, (vp_ref, vc_ref, vn_ref, vx_ref)):
            pv = jnp.dot(p.astype(BF16), v_ref[:, cols], preferred_element_type=F32)
            acc = pv if acc is None else acc + pv
        o_ref[:, cols] = (acc / l).astype(o_ref.dtype)


def _neighbourhood_attention(p, p_ctx, bias, n_heads, q_col0, k_col0, v_col0, cast_tables=()):
    s_len = p.shape[0]
    tq = NA_ROWS_PER_STEP * GRID_W
    n_steps = s_len // tq
    hb = 4 if n_heads % 4 == 0 else 1
    bw = hb * HEAD_DIM
    assert p_ctx.shape[0] == tq
    qc, kc, vc = q_col0 // bw, k_col0 // bw, v_col0 // bw
    prev = lambda g: jnp.maximum(g - 1, 0)
    nxt = lambda g: jnp.minimum(g + 1, n_steps - 1)
    variant = lambda g: jnp.where(g == 0, 0, jnp.where(g == n_steps - 1, 2, 1))
    lat = lambda col, row: pl.BlockSpec((tq, bw), lambda h, g: (row(g), col + h))
    ident = lambda g: g
    side = _SideCast(cast_tables, (n_heads // hb) * n_steps, lambda h, g: h * n_steps + g)
    res = pl.pallas_call(
        functools.partial(_na_kernel, heads=hb, scale=HEAD_DIM ** -0.5 * LOG2E, n_side=side.n),
        grid=(n_heads // hb, n_steps),
        in_specs=[lat(qc, ident),
                  lat(kc, prev), lat(kc, ident), lat(kc, nxt),
                  lat(vc, prev), lat(vc, ident), lat(vc, nxt),
                  pl.BlockSpec((tq, bw), lambda h, g: (0, kc + h)),
                  pl.BlockSpec((tq, bw), lambda h, g: (0, vc + h)),
                  pl.BlockSpec((1, hb, tq, 3 * tq), lambda h, g: (variant(g), h, 0, 0))] + side.in_specs,
        out_specs=[pl.BlockSpec((tq, bw), lambda h, g: (g, h))] + side.out_specs,
        out_shape=[jax.ShapeDtypeStruct((s_len, n_heads * HEAD_DIM), BF16)] + side.out_shapes,
        compiler_params=_params("parallel", "arbitrary"),
    )(p, p, p, p, p, p, p, p_ctx, p_ctx, bias, *side.tables)
    return res if side.n else res[0]


def _swa_kernel(sink_ref, q_ref, kp_ref, kc_ref, kn_ref, vp_ref, vc_ref, vn_ref, kx_ref, vx_ref, mask_ref, o_ref,
                *, group):
    h = pl.program_id(0)
    tq = q_ref.shape[0]
    q = jnp.concatenate([q_ref[:, g * HEAD_DIM:(g + 1) * HEAD_DIM] for g in range(group)], 0)
    ss = []
    for idx, k_ref in enumerate((kp_ref, kc_ref, kn_ref)):
        s = lax.dot_general(q, k_ref[...], _NT, preferred_element_type=F32)
        ss.append(s + mask_ref[0, idx])
    ss.append(lax.dot_general(q, kx_ref[...], _NT, preferred_element_type=F32))
    row_g = lax.broadcasted_iota(jnp.int32, (group * tq, 1), 0) // tq
    sink = jnp.zeros((group * tq, 1), F32)
    for g in range(group):
        sink = jnp.where(row_g == g, sink_ref[h * group + g] * LOG2E, sink)
    m = functools.reduce(jnp.maximum, [jnp.max(s, -1, keepdims=True) for s in ss] + [sink])
    ps = [jnp.exp2(s - m) for s in ss]
    l = functools.reduce(jnp.add, [jnp.sum(p, -1, keepdims=True) for p in ps]) + jnp.exp2(sink - m)
    acc = None
    for p, v_ref in zip(ps, (vp_ref, vc_ref, vn_ref, vx_ref)):
        pv = jnp.dot(p.astype(BF16), v_ref[...], preferred_element_type=F32)
        acc = pv if acc is None else acc + pv
    out = acc / l
    for g in range(group):
        o_ref[:, g * HEAD_DIM:(g + 1) * HEAD_DIM] = out[g * tq:(g + 1) * tq].astype(o_ref.dtype)


def _sliding_window_attention(q, k, v, v_col0, kx, kx_col0, vx, vx_col0, sink, *, kv_heads, group):
    s_len = q.shape[0]
    tq = SW_BLOCK
    assert s_len % tq == 0 and tq >= SW_WINDOW and kx.shape[0] == vx.shape[0]
    n_blk = s_len // tq
    n_ctx = kx.shape[0]
    qw = group * HEAD_DIM
    vc, kxc, vxc = v_col0 // HEAD_DIM, kx_col0 // HEAD_DIM, vx_col0 // HEAD_DIM
    prev = lambda n: jnp.maximum(n - 1, 0)
    nxt = lambda n: jnp.minimum(n + 1, n_blk - 1)
    ident = lambda n: n
    blk = lambda col, row: pl.BlockSpec((tq, HEAD_DIM), lambda h, n: (row(n), col + h))
    assert n_blk >= 3
    ql = np.arange(group * tq)[:, None] % tq
    kl = np.arange(tq)[None, :]
    diff = kl - ql
    ok = np.stack([diff - tq >= -SW_WINDOW, np.abs(diff) <= SW_WINDOW, diff + tq <= SW_WINDOW])
    none = np.zeros_like(ok[0])
    variants = np.stack([np.stack([none, ok[1], ok[2]]), ok, np.stack([ok[0], ok[1], none])])
    mask = jnp.where(variants, 0.0, NEG_INF).astype(F32)
    variant = lambda n: jnp.where(n == 0, 0, jnp.where(n == n_blk - 1, 2, 1))
    return pl.pallas_call(
        functools.partial(_swa_kernel, group=group),
        grid=(kv_heads, n_blk),
        in_specs=[pl.BlockSpec(memory_space=pltpu.SMEM),
                  pl.BlockSpec((tq, qw), lambda h, n: (n, h)),
                  blk(0, prev), blk(0, ident), blk(0, nxt),
                  blk(vc, prev), blk(vc, ident), blk(vc, nxt),
                  pl.BlockSpec((n_ctx, HEAD_DIM), lambda h, n: (0, kxc + h)),
                  pl.BlockSpec((n_ctx, HEAD_DIM), lambda h, n: (0, vxc + h)),
                  pl.BlockSpec((1, 3, group * tq, tq), lambda h, n: (variant(n), 0, 0, 0))],
        out_specs=pl.BlockSpec((tq, qw), lambda h, n: (n, h)),
        out_shape=jax.ShapeDtypeStruct((s_len, kv_heads * qw), BF16),
        compiler_params=_params("parallel", "arbitrary"),
    )(sink.astype(F32), q, k, k, k, v, v, v, kx, vx, mask)


def _ret_kernel(cd_ref, qf_ref, kf_ref, vf_ref, qb_ref, kb_ref, vb_ref, kx_ref, vx_ref,
                intra_ref, qd_ref, kd_ref, wx_ref, of_ref, ob_ref, st_ref, *, heads):
    hb, n = pl.program_id(0), pl.program_id(1)

    @pl.when(n == 0)
    def _():
        for hh in range(heads):
            cols = slice(hh * HEAD_DIM, (hh + 1) * HEAD_DIM)
            for d in range(2):
                kw = (kx_ref[:, cols].astype(F32) * wx_ref[d, hh]).astype(BF16)
                st_ref[d, hh] = lax.dot_general(kw, vx_ref[:, cols], _TN, preferred_element_type=F32)

    for hh in range(heads):
        cols = slice(hh * HEAD_DIM, (hh + 1) * HEAD_DIM)
        head = hb * heads + hh
        for d, (q_ref, k_ref, v_ref, o_ref) in enumerate(((qf_ref, kf_ref, vf_ref, of_ref),
                                                          (qb_ref, kb_ref, vb_ref, ob_ref))):
            q, k, v = q_ref[:, cols], k_ref[:, cols], v_ref[:, cols]
            state = st_ref[d, hh]
            inner = lax.dot_general(q, k, _NT, preferred_element_type=F32) * intra_ref[d, hh]
            qs = (q.astype(F32) * qd_ref[d, hh]).astype(BF16)
            o_ref[:, cols] = (jnp.dot(inner.astype(BF16), v, preferred_element_type=F32)
                              + jnp.dot(qs, state.astype(BF16), preferred_element_type=F32))
            ks = (k.astype(F32) * kd_ref[d, hh]).astype(BF16)
            st_ref[d, hh] = state * cd_ref[d, head] + lax.dot_general(ks, v, _TN, preferred_element_type=F32)


def _retention(rq, rk, p, v_col0, rkx, p_ctx, log_gamma, n_heads):
    s_len = rq.shape[0]
    c = RET_CHUNK
    n_chunks = s_len // c
    n_ctx = rkx.shape[0]
    hb = 8 if n_heads % 8 == 0 else (4 if n_heads % 4 == 0 else 1)
    bw = hb * HEAD_DIM
    assert v_col0 % bw == 0
    vc = v_col0 // bw
    lg = log_gamma.astype(F32)
    pos = jnp.arange(c, dtype=F32)
    diff = pos[:, None] - pos[None, :]
    intra_f = jnp.where(diff >= 0, jnp.exp(lg[0][:, None, None] * jnp.maximum(diff, 0.0)), 0.0)
    intra_b = jnp.where(diff <= 0, jnp.exp(lg[1][:, None, None] * jnp.maximum(-diff, 0.0)), 0.0)
    intra = jnp.stack([intra_f, intra_b])
    lanes = lambda t: jnp.broadcast_to(t[..., None], t.shape + (HEAD_DIM,))
    qd = lanes(jnp.stack([jnp.exp(lg[0][:, None] * (pos + 1.0)), jnp.exp(lg[1][:, None] * (c - pos))]))
    kd = lanes(jnp.stack([jnp.exp(lg[0][:, None] * (c - 1.0 - pos)), jnp.exp(lg[1][:, None] * pos)]))
    cd = jnp.exp(lg * c)
    jx = jnp.arange(n_ctx, dtype=F32)
    wx = lanes(jnp.stack([jnp.exp(lg[0][:, None] * (n_ctx - 1.0 - jx)), jnp.exp(lg[1][:, None] * jx)]))
    fwd = lambda col: pl.BlockSpec((c, bw), lambda h, n: (n, col + h))
    bwd = lambda col: pl.BlockSpec((c, bw), lambda h, n: (n_chunks - 1 - n, col + h))
    tab = lambda rows: pl.BlockSpec((2, hb, rows, HEAD_DIM), lambda h, n: (0, h, 0, 0))
    out_sds = jax.ShapeDtypeStruct((s_len, n_heads * HEAD_DIM), F32)
    return pl.pallas_call(
        functools.partial(_ret_kernel, heads=hb),
        grid=(n_heads // hb, n_chunks),
        in_specs=[pl.BlockSpec(memory_space=pltpu.SMEM),
                  fwd(0), fwd(0), fwd(vc), bwd(0), bwd(0), bwd(vc),
                  pl.BlockSpec((n_ctx, bw), lambda h, n: (0, h)),
                  pl.BlockSpec((n_ctx, bw), lambda h, n: (0, vc + h)),
                  tab(c), tab(c), tab(c), tab(n_ctx)],
        out_specs=[fwd(0), bwd(0)],
        out_shape=[out_sds, out_sds],
        scratch_shapes=[pltpu.VMEM((2, hb, HEAD_DIM, HEAD_DIM), F32)],
        compiler_params=_params("parallel", "arbitrary"),
    )(cd, rq, rk, p, rq, rk, p, rkx, p_ctx, intra, qd, kd, wx)


def _ret_out_kernel(of_ref, ob_ref, g_ref, o_ref, *, heads):
    for hh in range(heads):
        cols = slice(hh * HEAD_DIM, (hh + 1) * HEAD_DIM)
        o = of_ref[:, cols] + ob_ref[:, cols]
        mu = jnp.mean(o, -1, keepdims=True)
        var = jnp.mean(jnp.square(o - mu), -1, keepdims=True)
        gate = g_ref[:, cols].astype(F32)
        gate = gate * (1.0 / (1.0 + jnp.exp(-gate)))
        o_ref[:, cols] = (gate * ((o - mu) * lax.rsqrt(var + LN_EPS))).astype(o_ref.dtype)


def _retention_output(o_f, o_b, p, gate_col0, n_heads):
    s_len = o_f.shape[0]
    hb = 4 if n_heads % 4 == 0 else 1
    bw = hb * HEAD_DIM
    gc = gate_col0 // bw
    tt = min(s_len, 512)
    spec = pl.BlockSpec((tt, bw), lambda i, j: (i, j))
    return pl.pallas_call(
        functools.partial(_ret_out_kernel, heads=hb),
        grid=(s_len // tt, n_heads // hb),
        in_specs=[spec, spec, pl.BlockSpec((tt, bw), lambda i, j: (i, gc + j))],
        out_specs=spec,
        out_shape=jax.ShapeDtypeStruct((s_len, n_heads * HEAD_DIM), BF16),
        compiler_params=_params("parallel", "parallel"),
    )(o_f, o_b, p)


def _residual_ln(x, y, gate, g, b, alpha):
    h = alpha * x + gate * y
    mu = jnp.mean(h, -1, keepdims=True)
    var = jnp.mean(jnp.square(h - mu), -1, keepdims=True)
    return (h - mu) * lax.rsqrt(var + LN_EPS) * g + b


def _outproj_kernel(ya_ref, yb_ref, w_ref, x_ref, gate_ref, g_ref, b_ref, o_ref, *, alpha):
    wa = ya_ref.shape[1]
    o_ref[...] = (jnp.dot(ya_ref[...], w_ref[:wa, :], preferred_element_type=F32)
                  + jnp.dot(yb_ref[...], w_ref[wa:, :], preferred_element_type=F32))
    for rows in _row_chunks(x_ref.shape[0], 64):
        o_ref[rows, :] = _residual_ln(x_ref[rows, :], o_ref[rows, :], gate_ref[...], g_ref[...], b_ref[...], alpha)


def _outproj_ln(ya, yb, w, x, gate, g, b, alpha, tm):
    m, wa = ya.shape
    wb = yb.shape[1]
    d = w.shape[1]
    tm = min(tm, m)
    vec = pl.BlockSpec((1, d), lambda i: (0, 0))
    return pl.pallas_call(
        functools.partial(_outproj_kernel, alpha=alpha),
        grid=(m // tm,),
        in_specs=[pl.BlockSpec((tm, wa), lambda i: (i, 0)),
                  pl.BlockSpec((tm, wb), lambda i: (i, 0)),
                  pl.BlockSpec((wa + wb, d), lambda i: (0, 0), pipeline_mode=pl.Buffered(1)),
                  pl.BlockSpec((tm, d), lambda i: (i, 0)),
                  vec, vec, vec],
        out_specs=pl.BlockSpec((tm, d), lambda i: (i, 0)),
        out_shape=jax.ShapeDtypeStruct((m, d), F32),
        compiler_params=_params("parallel", vmem=VMEM_LIMIT_RESIDENT_BYTES),
    )(ya, yb, w, x, gate, g.reshape(1, d), b.reshape(1, d))


def _add_ln_kernel(x_ref, f_ref, gate_ref, g_ref, b_ref, o_ref, *, alpha):
    o_ref[...] = _residual_ln(x_ref[...], f_ref[...].astype(F32), gate_ref[...], g_ref[...], b_ref[...], alpha)


def _add_ln(x, f, gate, g, b, alpha):
    m, d = x.shape
    tt = min(m, 256)
    row = pl.BlockSpec((tt, d), lambda i: (i, 0))
    vec = pl.BlockSpec((1, d), lambda i: (0, 0))
    return pl.pallas_call(
        functools.partial(_add_ln_kernel, alpha=alpha),
        grid=(m // tt,),
        in_specs=[row, row, vec, vec, vec],
        out_specs=row,
        out_shape=jax.ShapeDtypeStruct((m, d), F32),
        compiler_params=_params("parallel"),
    )(x, f, gate, g.reshape(1, d), b.reshape(1, d))


PEER_LIST = 24


def _peer_gate_kernel(q_ref, keys_ref, d1_ref, f1_ref, s2_ref, e2_ref, a_ref, b_ref, cand_ref):
    half = keys_ref.shape[-1]
    k = PEER_TOPK
    s1 = lax.dot_general(keys_ref[0, 0], q_ref[:, :half].astype(BF16), _NT, preferred_element_type=F32)
    s2 = lax.dot_general(keys_ref[0, 1], q_ref[:, half:].astype(BF16), _NT, preferred_element_type=F32)

    def top_rows(s, dst_ref):
        dst_ref[...] = jnp.full_like(dst_ref, -jnp.inf)
        n_slabs = s.shape[0] // 8
        if n_slabs & (n_slabs - 1):
            cur = s
            for r in range(k + 1):
                m = jnp.max(cur, 0, keepdims=True)
                dst_ref[r:r + 1, :] = m
                cur = jnp.where(cur == m, -jnp.inf, cur)
            return
        v = [s[8 * i:8 * i + 8, :] for i in range(n_slabs)]
        size = 2
        while size <= n_slabs:
            stride = size // 2
            while stride >= 1:
                for i in range(n_slabs):
                    j = i ^ stride
                    if j > i:
                        hi, lo = jnp.maximum(v[i], v[j]), jnp.minimum(v[i], v[j])
                        v[i], v[j] = (hi, lo) if (i & size) == 0 else (lo, hi)
                stride //= 2
            size *= 2
        depth = min(n_slabs, k + 1)
        v = v[:depth] + [jnp.full_like(v[0], -jnp.inf)]
        for r in range(k + 1):
            m = jnp.max(v[0], 0, keepdims=True)
            dst_ref[r:r + 1, :] = m
            win = v[0] == m
            for i in range(depth):
                v[i] = jnp.where(win, v[i + 1], v[i])

    top_rows(s1, a_ref)
    top_rows(s2, b_ref)
    half_k = k // 2
    cand_ref[0:PEER_LIST, :] = a_ref[0:1, :] + b_ref[...]
    for r in range(1, half_k):
        cand_ref[PEER_LIST + (r - 1) * half_k:PEER_LIST + r * half_k, :] = a_ref[r:r + 1, :] + b_ref[0:half_k, :]
    cand_ref[PEER_LIST + (half_k - 1) * half_k:, :] = a_ref[half_k:, :] + b_ref[0:1, :]
    best = a_ref[0:1, :] + b_ref[0:1, :]
    cur = cand_ref[...]
    z = jnp.zeros_like(best)
    kth = best
    for r in range(k):
        kth = jnp.max(cur, 0, keepdims=True)
        z = z + jnp.exp(kth - best)
        cur = jnp.where(cur == kth, -jnp.inf, cur)
    tau = 0.5 * (kth + jnp.max(cur, 0, keepdims=True))
    d1_ref[0] = tau - s1
    f1_ref[0] = jnp.exp(s1 - a_ref[0:1, :]) / z
    s2_ref[0] = s2
    e2_ref[0] = jnp.exp(s2 - b_ref[0:1, :])


def _peer_gates(q, sub_keys, tm):
    t = q.shape[0]
    n_heads, _, n_keys, half = sub_keys.shape
    tm = min(tm, t)
    big = jax.ShapeDtypeStruct((n_heads, n_keys, t), F32)
    blk = pl.BlockSpec((1, n_keys, tm), lambda i, h: (h, 0, i))
    return pl.pallas_call(
        _peer_gate_kernel,
        grid=(t // tm, n_heads),
        in_specs=[pl.BlockSpec((tm, 2 * half), lambda i, h: (i, h)),
                  pl.BlockSpec((1, 2, n_keys, half), lambda i, h: (h, 0, 0, 0))],
        out_specs=[blk, blk, blk, blk],
        out_shape=[big, big, big, big],
        scratch_shapes=[pltpu.VMEM((PEER_LIST, tm), F32), pltpu.VMEM((PEER_LIST, tm), F32),
                        pltpu.VMEM((2 * PEER_LIST - PEER_TOPK // 2 + (PEER_TOPK // 2 - 1) * (PEER_TOPK // 2), tm),
                                   F32)],
        compiler_params=_params("parallel", "parallel"),
    )(q, sub_keys)


def _gelu_tanh(x):
    c = math.sqrt(2.0 / math.pi)
    hx = 0.5 * x
    return hx + hx * jnp.tanh(x * (c + (c * 0.044715) * (x * x)))


def _peer_kernel(h_ref, u_ref, v_ref, d1_ref, f1_ref, s2_ref, e2_ref, *rest, n_side):
    side_in, o_ref, side_out = rest[:n_side], rest[n_side], rest[n_side + 1:2 * n_side + 1]
    acc_ref, g_ref = rest[2 * n_side + 1:]
    _SideCast.run(side_in, side_out)
    cj, ci = pl.program_id(1), pl.program_id(2)
    n_heads, ni, tm = d1_ref.shape
    nj = s2_ref.shape[1]
    d = h_ref.shape[1]

    @pl.when((cj == 0) & (ci == 0))
    def _():
        acc_ref[...] = jnp.zeros_like(acc_ref)

    rows = 32
    for ii in range(ni):
        for r0 in range(0, nj, rows):
            for t0 in range(0, tm, 128):
                tok = slice(t0, t0 + 128)
                g = None
                for h in range(n_heads):
                    w = f1_ref[h, ii:ii + 1, tok] * e2_ref[h, r0:r0 + rows, tok]
                    sel = jnp.where(s2_ref[h, r0:r0 + rows, tok] >= d1_ref[h, ii:ii + 1, tok], w, 0.0)
                    g = sel if g is None else g + sel
                g_ref[ii * nj + r0:ii * nj + r0 + rows, tok] = g
    u = u_ref[...].reshape(ni * nj, d)
    act = _gelu_tanh(lax.dot_general(u, h_ref[...], _NT, preferred_element_type=F32))
    wt = (g_ref[...] * act).astype(BF16)
    acc_ref[...] += lax.dot_general(wt, v_ref[...].reshape(ni * nj, d), _TN, preferred_element_type=F32)

    @pl.when((cj == pl.num_programs(1) - 1) & (ci == pl.num_programs(2) - 1))
    def _():
        o_ref[...] = acc_ref[...].astype(o_ref.dtype)


def _peer_dense(hf, u3, v3, d1, f1, s2, e2, tm, cast_tables=()):
    t, d = hf.shape
    n_heads, n_keys = d1.shape[0], d1.shape[1]
    tm = min(tm, t)
    ni, nj = PEER_NI, PEER_NJ
    n_cj, n_ci = n_keys // nj, n_keys // ni
    tab = pl.BlockSpec((ni, nj, d), lambda i, cj, ci: (ci, cj, 0))
    first = pl.BlockSpec((n_heads, ni, tm), lambda i, cj, ci: (0, ci, i))
    second = pl.BlockSpec((n_heads, nj, tm), lambda i, cj, ci: (0, cj, i))
    side = _SideCast(cast_tables, (t // tm) * n_cj * n_ci, lambda i, cj, ci: (i * n_cj + cj) * n_ci + ci)
    res = pl.pallas_call(
        functools.partial(_peer_kernel, n_side=side.n),
        grid=(t // tm, n_cj, n_ci),
        in_specs=[pl.BlockSpec((tm, d), lambda i, cj, ci: (i, 0)), tab, tab, first, first, second, second]
        + side.in_specs,
        out_specs=[pl.BlockSpec((tm, d), lambda i, cj, ci: (i, 0))] + side.out_specs,
        out_shape=[jax.ShapeDtypeStruct((t, d), BF16)] + side.out_shapes,
        scratch_shapes=[pltpu.VMEM((tm, d), F32), pltpu.VMEM((ni * nj, tm), F32)],
        compiler_params=_params("parallel", "arbitrary", "arbitrary"),
    )(hf, u3, v3, d1, f1, s2, e2, *side.tables)
    return res if side.n else res[0]


def _peer_block(x, shift, scale, gate, w_q, sub_keys, u3, v3, g, b, alpha, tm, cast_tables=()):
    q, hf = _proj(x, shift, scale, w_q, F32, tm, emit_h=True, resident_w=True)
    d1, f1, s2, e2 = _peer_gates(q, sub_keys, tm)
    res = _peer_dense(hf, u3, v3, d1, f1, s2, e2, tm, cast_tables)
    f, casts = (res[0], tuple(res[1:])) if cast_tables else (res, ())
    return _add_ln(x, f, gate, g, b, alpha), casts


def kernel(x, c, ctx, c_ctx, ada_w, ada_b, ln_g, ln_b, ev_w_in, ev_w_out, ev_na_rpb, ev_ga_q_gain, ev_ga_k_gain,
           od_w_in, od_w_out, od_ret_decay_exp, od_sw_sink, peer_w_q, peer_sub_keys, peer_u, peer_v):
    assert x.shape[0] == 1 and ada_w.shape[0] == 2
    depth = ada_w.shape[0]
    _, s_len, d = x.shape
    hd = HEAD_DIM
    mix_heads = d // (2 * hd)
    kv_heads = mix_heads // 4
    group = mix_heads // kv_heads
    alpha = (2 * depth) ** 0.25
    att_scale = hd ** -0.5
    sm_scale = att_scale * LOG2E
    n_keys = peer_sub_keys.shape[3]
    tm = 512
    tm_out = 256

    xs, cs = x[0], ctx[0]
    mods = _ada_modulation(c, c_ctx, ada_w, ada_b)
    cos_i, sin_s = _rope_tables(s_len)
    ones = jnp.ones((hd,), F32)

    def mod6(layer, which):
        return [m.reshape(1, d) for m in jnp.split(mods[layer, which], 6)]

    def peer_args(layer, wq_bf16, u_bf16, v_bf16):
        return (wq_bf16, peer_sub_keys[layer].astype(BF16),
                u_bf16.reshape(n_keys, n_keys, d), v_bf16.reshape(n_keys, n_keys, d),
                ln_g[layer, 1], ln_b[layer, 1], alpha)

    layer1_weights = ((peer_u, 1), (peer_v, 1), (od_w_in, 0), (od_w_out, 0), (peer_w_q, 1))

    sh_a, sc_a, g_a, sh_f, sc_f, g_f = mod6(0, 0)
    csh_a, csc_a, cg_a, csh_f, csc_f, cg_f = mod6(0, 1)
    w_in = _to_bf16(ev_w_in, 0)
    p = _proj(xs, sh_a, sc_a, w_in, BF16, tm)
    pc = _proj(cs, csh_a, csc_a, w_in, BF16, tm)
    mh = mix_heads * hd
    kvw = kv_heads * hd
    na_q, na_k, na_v, ga_q, ga_k, ga_v = 0, mh, 2 * mh, 3 * mh, 4 * mh, 4 * mh + kvw
    qg = _prep(p, ga_q, mix_heads, ev_ga_q_gain[0], cos_i, sin_s, norm=True, rope=True, scale=sm_scale)
    kg = _prep(p, ga_k, kv_heads, ev_ga_k_gain[0], cos_i, sin_s, norm=True, rope=True)
    qgc = _prep(pc, ga_q, mix_heads, ev_ga_q_gain[0], cos_i, sin_s, norm=True, scale=sm_scale)
    kgc = _prep(pc, ga_k, kv_heads, ev_ga_k_gain[0], cos_i, sin_s, norm=True)
    k_all = jnp.concatenate([kg, kgc], 0)
    v_all = jnp.concatenate([p[:, ga_v:ga_v + kvw], pc[:, ga_v:ga_v + kvw]], 0)
    tk = _pick(k_all.shape[0], 1024, 256)
    y_ga, u0, v0 = _flash(qg, 0, k_all, 0, v_all, 0, kv_heads=kv_heads, group=group, tq=512, tk=tk,
                          cast_tables=((peer_u, 0), (peer_v, 0)))
    yc_ga = _flash(qgc, 0, kgc, 0, pc, ga_v, kv_heads=kv_heads, group=group, tq=256, tk=256)
    bias = _na_bias_tables(ev_na_rpb[0], s_len // GRID_W)
    y_na = _neighbourhood_attention(p, pc, bias, mix_heads, na_q, na_k, na_v)
    w_out, wq0 = _to_bf16(ev_w_out, 0), _to_bf16(peer_w_q, 0)
    yc_na = _flash(pc, na_q, pc, na_k, pc, na_v, kv_heads=mix_heads, group=1, tq=256, tk=256, scale=sm_scale)
    xs = _outproj_ln(y_na, y_ga, w_out, xs, g_a, ln_g[0, 0], ln_b[0, 0], alpha, tm_out)
    cs = _outproj_ln(yc_na, yc_ga, w_out, cs, cg_a, ln_g[0, 0], ln_b[0, 0], alpha, tm_out)
    pa = peer_args(0, wq0, u0, v0)
    xs, (u1, v1, w_in1, w_out1, wq1) = _peer_block(xs, sh_f, sc_f, g_f, *pa, tm, cast_tables=layer1_weights)
    cs, _ = _peer_block(cs, csh_f, csc_f, cg_f, *pa, tm)

    sh_a, sc_a, g_a, sh_f, sc_f, g_f = mod6(1, 0)
    csh_a, csc_a = mod6(1, 1)[:2]
    p = _proj(xs, sh_a, sc_a, w_in1, BF16, tm)
    pc = _proj(cs, csh_a, csc_a, w_in1, BF16, tm)
    r_q, r_k, r_v, r_g, s_q, s_k, s_v = 0, mh, 2 * mh, 3 * mh, 4 * mh, 5 * mh, 5 * mh + kvw
    rq = _prep(p, r_q, mix_heads, ones, cos_i, sin_s, rope=True)
    rk = _prep(p, r_k, mix_heads, ones, cos_i, sin_s, rope=True, scale=att_scale)
    rkc = _prep(pc, r_k, mix_heads, ones, cos_i, sin_s, scale=att_scale)
    sq = _prep(p, s_q, mix_heads, ones, cos_i, sin_s, rope=True, scale=sm_scale)
    sk = _prep(p, s_k, kv_heads, ones, cos_i, sin_s, rope=True)
    log_gamma = jnp.log1p(-jnp.exp2(-od_ret_decay_exp[0].astype(F32)))
    o_f, o_b = _retention(rq, rk, p, r_v, rkc, pc, log_gamma, mix_heads)
    y_ret = _retention_output(o_f, o_b, p, r_g, mix_heads)
    y_sw = _sliding_window_attention(sq, sk, p, s_v, pc, s_k, pc, s_v, od_sw_sink[0],
                                     kv_heads=kv_heads, group=group)
    xs = _outproj_ln(y_ret, y_sw, w_out1, xs, g_a, ln_g[1, 0], ln_b[1, 0], alpha, tm_out)
    xs, _ = _peer_block(xs, sh_f, sc_f, g_f, *peer_args(1, wq1, u1, v1), tm)
    return xs[None]
```

```python
import functools
import math

import numpy as np
import jax
import jax.numpy as jnp
from jax import lax
from jax.experimental import pallas as pl
from jax.experimental.pallas import tpu as pltpu

F32 = jnp.float32
BF16 = jnp.bfloat16

GRID_W = 64
HEAD_DIM = 128
NA_WIN_R = 8
NA_WIN_C = 16
NA_ROWS_PER_STEP = 4
SW_WINDOW = 128
SW_BLOCK = 256
RET_CHUNK = 128
ROPE_THETA = 10000.0
PEER_TOPK = 16
LN_EPS = 1e-6
NEG_INF = -1e30
LOG2E = math.log2(math.e)
VMEM_LIMIT_BYTES = 56 * 1024 * 1024
VMEM_LIMIT_RESIDENT_BYTES = 60 * 1024 * 1024

_NT = (((1,), (1,)), ((), ()))
_TN = (((0,), (0,)), ((), ()))


def _params(*sem, vmem=VMEM_LIMIT_BYTES):
    return pltpu.CompilerParams(dimension_semantics=sem, vmem_limit_bytes=vmem)


def _row_chunks(n, size=128):
    return [slice(r, min(r + size, n)) for r in range(0, n, size)]


def _pick(n, cap, mult=128):
    best = None
    for t in range(mult, min(n, cap) + 1, mult):
        if n % t == 0:
            best = t
    assert best is not None, (n, cap, mult)
    return best


def _cast_kernel(x_ref, o_ref):
    o_ref[...] = x_ref[...].astype(o_ref.dtype)


def _to_bf16(w, layer):
    _, rows, cols = w.shape
    tr = _pick(rows, max(8, (2 * 1024 * 1024) // cols), 8)
    return pl.pallas_call(
        _cast_kernel,
        grid=(rows // tr,),
        in_specs=[pl.BlockSpec((None, tr, cols), lambda i: (layer, i, 0))],
        out_specs=pl.BlockSpec((tr, cols), lambda i: (i, 0)),
        out_shape=jax.ShapeDtypeStruct((rows, cols), BF16),
        compiler_params=_params("parallel"),
    )(w)


PEER_NI, PEER_NJ = 8, 64


class _SideCast:
    def __init__(self, tables, n_steps, step_of):
        self.n = len(tables)
        self.tables = [w for w, _ in tables]
        self.in_specs, self.out_specs, self.out_shapes = [], [], []
        for w, layer in tables:
            _, rows, d = w.shape
            assert rows % n_steps == 0 and (rows // n_steps) % 16 == 0, (w.shape, n_steps)
            r = rows // n_steps
            self.in_specs.append(pl.BlockSpec((None, r, d), lambda *ids, layer=layer: (layer, step_of(*ids), 0)))
            self.out_specs.append(pl.BlockSpec((r, d), lambda *ids: (step_of(*ids), 0)))
            self.out_shapes.append(jax.ShapeDtypeStruct((rows, d), BF16))

    @staticmethod
    def run(in_refs, out_refs):
        for src, dst in zip(in_refs, out_refs):
            for rows in _row_chunks(src.shape[0], 64):
                dst[rows, :] = src[rows, :].astype(dst.dtype)


def _ada_kernel(c_ref, w_ref, b_ref, o_ref):
    c = c_ref[...]
    s = c * (1.0 / (1.0 + jnp.exp(-c)))
    o_ref[0] = jnp.dot(s.astype(BF16), w_ref[0].astype(BF16), preferred_element_type=F32) + b_ref[0]


def _ada_modulation(c, c_ctx, ada_w, ada_b):
    depth, d, n = ada_w.shape
    cc = jnp.zeros((8, d), F32).at[0].set(c[0]).at[1].set(c_ctx)
    tn = _pick(n, 512)
    out = pl.pallas_call(
        _ada_kernel,
        grid=(depth, n // tn),
        in_specs=[pl.BlockSpec((8, d), lambda l, j: (0, 0)),
                  pl.BlockSpec((1, d, tn), lambda l, j: (l, 0, j)),
                  pl.BlockSpec((1, 1, tn), lambda l, j: (l, 0, j))],
        out_specs=pl.BlockSpec((1, 8, tn), lambda l, j: (l, 0, j)),
        out_shape=jax.ShapeDtypeStruct((depth, 8, n), F32),
        compiler_params=_params("parallel", "parallel"),
    )(cc, ada_w, ada_b.reshape(depth, 1, n))
    return out


def _proj_kernel(x_ref, sh_ref, sc_ref, w_ref, o_ref, *rest, emit_h):
    h_ref = rest[-1]

    @pl.when(pl.program_id(1) == 0)
    def _():
        for rows in _row_chunks(x_ref.shape[0]):
            h = (x_ref[rows, :] * (1.0 + sc_ref[...]) + sh_ref[...]).astype(BF16)
            h_ref[rows, :] = h
            if emit_h:
                rest[0][rows, :] = h

    o_ref[...] = jnp.dot(h_ref[...], w_ref[...], preferred_element_type=F32).astype(o_ref.dtype)


def _proj(x, shift, scale, w, out_dtype, tm, emit_h=False, resident_w=False):
    m, d = x.shape
    n = w.shape[1]
    tm = min(tm, m)
    tn = n if resident_w else _pick(n, 1024)
    w_spec = (pl.BlockSpec((d, tn), lambda i, j: (0, j), pipeline_mode=pl.Buffered(1)) if resident_w
              else pl.BlockSpec((d, tn), lambda i, j: (0, j)))
    out_shape = [jax.ShapeDtypeStruct((m, n), out_dtype)]
    out_specs = [pl.BlockSpec((tm, tn), lambda i, j: (i, j))]
    if emit_h:
        out_shape.append(jax.ShapeDtypeStruct((m, d), BF16))
        out_specs.append(pl.BlockSpec((tm, d), lambda i, j: (i, 0)))
    res = pl.pallas_call(
        functools.partial(_proj_kernel, emit_h=emit_h),
        grid=(m // tm, n // tn),
        in_specs=[pl.BlockSpec((tm, d), lambda i, j: (i, 0)),
                  pl.BlockSpec((1, d), lambda i, j: (0, 0)),
                  pl.BlockSpec((1, d), lambda i, j: (0, 0)),
                  w_spec],
        out_specs=out_specs,
        out_shape=out_shape,
        scratch_shapes=[pltpu.VMEM((tm, d), BF16)],
        compiler_params=_params("parallel", "arbitrary"),
    )(x, shift, scale, w)
    return res if emit_h else res[0]


def _prep_kernel(x_ref, g_ref, cos_ref, sin_ref, o_ref, *, heads, norm, rope, scale):
    for hh in range(heads):
        cols = slice(hh * HEAD_DIM, (hh + 1) * HEAD_DIM)
        x = x_ref[:, cols].astype(F32)
        if norm:
            x = x * lax.rsqrt(jnp.mean(x * x, -1, keepdims=True) + LN_EPS) * g_ref[...]
        if rope:
            lane = lax.broadcasted_iota(jnp.int32, x.shape, 1)
            partner = jnp.where(lane % 2 == 0, pltpu.roll(x, HEAD_DIM - 1, 1), pltpu.roll(x, 1, 1))
            x = x * cos_ref[...] + partner * sin_ref[...]
        if scale != 1.0:
            x = x * scale
        o_ref[:, cols] = x.astype(o_ref.dtype)


def _prep(p, col0, n_heads, gain, cos_i, sin_s, *, norm=False, rope=False, scale=1.0):
    t = p.shape[0]
    hb = 4 if n_heads % 4 == 0 else 1
    bw = hb * HEAD_DIM
    assert col0 % bw == 0
    tt = min(t, 1024)
    c0 = col0 // bw
    return pl.pallas_call(
        functools.partial(_prep_kernel, heads=hb, norm=norm, rope=rope, scale=scale),
        grid=(t // tt, n_heads // hb),
        in_specs=[pl.BlockSpec((tt, bw), lambda i, j: (i, c0 + j)),
                  pl.BlockSpec((1, HEAD_DIM), lambda i, j: (0, 0)),
                  pl.BlockSpec((tt, HEAD_DIM), lambda i, j: (i, 0)),
                  pl.BlockSpec((tt, HEAD_DIM), lambda i, j: (i, 0))],
        out_specs=pl.BlockSpec((tt, bw), lambda i, j: (i, j)),
        out_shape=jax.ShapeDtypeStruct((t, n_heads * HEAD_DIM), BF16),
        compiler_params=_params("parallel", "parallel"),
    )(p, gain.reshape(1, HEAD_DIM).astype(F32), cos_i[:t], sin_s[:t])


def _rope_tables(n_tokens):
    t = jnp.arange(n_tokens, dtype=jnp.int32)
    row = (t // GRID_W).astype(F32)
    col = (t % GRID_W).astype(F32)
    n_freq = HEAD_DIM // 4
    inv = ROPE_THETA ** (-jnp.arange(n_freq, dtype=F32) / n_freq)
    ang = jnp.concatenate([row[:, None] * inv, col[:, None] * inv], -1)
    cos, sin = jnp.cos(ang), jnp.sin(ang)
    cos_i = jnp.repeat(cos, 2, axis=-1)
    sin_s = jnp.stack([-sin, sin], -1).reshape(n_tokens, HEAD_DIM)
    return cos_i, sin_s


KV_BLOCKS_PER_ITER = 2


def _flash_kernel(q_ref, k_ref, v_ref, *rest, group, tk, scale, n_side):
    side_in, o_ref, side_out = rest[:n_side], rest[n_side], rest[n_side + 1:2 * n_side + 1]
    m_ref, l_ref, acc_ref = rest[2 * n_side + 1:]
    _SideCast.run(side_in, side_out)
    n_blocks = k_ref.shape[0] // tk
    m_ref[...] = jnp.full_like(m_ref, NEG_INF)
    l_ref[...] = jnp.zeros_like(l_ref)
    acc_ref[...] = jnp.zeros_like(acc_ref)

    def block(start):
        k = k_ref[pl.ds(start, tk), :]
        v = jnp.concatenate([v_ref[pl.ds(start, tk), :], jnp.ones((tk, HEAD_DIM), BF16)], 1)
        for g in range(group):
            q = q_ref[:, g * HEAD_DIM:(g + 1) * HEAD_DIM]
            if scale != 1.0:
                q = (q.astype(F32) * scale).astype(BF16)
            s = lax.dot_general(q, k, _NT, preferred_element_type=F32)
            m_prev = m_ref[g]
            m_new = jnp.maximum(m_prev, jnp.max(s, -1, keepdims=True))
            alpha = jnp.exp2(m_prev - m_new)
            p = jnp.exp2((s - jnp.concatenate([m_new] * (tk // HEAD_DIM), 1)).astype(BF16))
            pv = jnp.dot(p, v, preferred_element_type=F32)
            l_ref[g] = alpha * l_ref[g] + pv[:, HEAD_DIM:]
            acc_ref[g] = alpha * acc_ref[g] + pv[:, :HEAD_DIM]
            m_ref[g] = m_new

    def group_of_blocks(i, carry):
        for b in range(KV_BLOCKS_PER_ITER):
            block(pl.multiple_of((i * KV_BLOCKS_PER_ITER + b) * tk, tk))
        return carry

    n_iter = n_blocks // KV_BLOCKS_PER_ITER
    lax.fori_loop(0, n_iter, group_of_blocks, 0)
    for b in range(n_iter * KV_BLOCKS_PER_ITER, n_blocks):
        block(b * tk)
    for g in range(group):
        o_ref[:, g * HEAD_DIM:(g + 1) * HEAD_DIM] = (acc_ref[g] / l_ref[g]).astype(o_ref.dtype)


def _flash(q, q_col0, k, k_col0, v, v_col0, *, kv_heads, group, tq, tk, scale=1.0, cast_tables=()):
    n_q, n_k = q.shape[0], k.shape[0]
    tq, tk = min(tq, n_q), min(tk, n_k)
    assert n_q % tq == 0 and n_k % tk == 0 and tk % HEAD_DIM == 0
    qw = group * HEAD_DIM
    assert q_col0 % qw == 0 and k_col0 % HEAD_DIM == 0 and v_col0 % HEAD_DIM == 0
    qc, kc, vc = q_col0 // qw, k_col0 // HEAD_DIM, v_col0 // HEAD_DIM
    n_qt = n_q // tq
    side = _SideCast(cast_tables, kv_heads * n_qt, lambda h, i: h * n_qt + i)
    res = pl.pallas_call(
        functools.partial(_flash_kernel, group=group, tk=tk, scale=scale, n_side=side.n),
        grid=(kv_heads, n_qt),
        in_specs=[pl.BlockSpec((tq, qw), lambda h, i: (i, qc + h)),
                  pl.BlockSpec((n_k, HEAD_DIM), lambda h, i: (0, kc + h)),
                  pl.BlockSpec((n_k, HEAD_DIM), lambda h, i: (0, vc + h))] + side.in_specs,
        out_specs=[pl.BlockSpec((tq, qw), lambda h, i: (i, h))] + side.out_specs,
        out_shape=[jax.ShapeDtypeStruct((n_q, kv_heads * qw), BF16)] + side.out_shapes,
        scratch_shapes=[pltpu.VMEM((group, tq, HEAD_DIM), F32),
                        pltpu.VMEM((group, tq, HEAD_DIM), F32),
                        pltpu.VMEM((group, tq, HEAD_DIM), F32)],
        compiler_params=_params("parallel", "parallel"),
    )(q, k, v, *side.tables)
    return res if side.n else res[0]


def _na_bias_tables(rpb, rows):
    n_heads = rpb.shape[0]
    rg, w, kr, kc = NA_ROWS_PER_STEP, GRID_W, NA_WIN_R, NA_WIN_C
    n_steps = rows // rg
    assert rows % rg == 0 and n_steps >= 3 and kr == 2 * rg and rows >= kr
    c = np.arange(w)
    dc = np.clip(c[None, :] - c[:, None] + (kc - 1), 0, 2 * kc - 2)
    oh_c = (np.arange(2 * kc - 1)[:, None, None] == dc[None]).astype(np.float32)
    rl, krl = np.arange(rg), np.arange(3 * rg)
    da = krl[None, :] - rl[:, None] - rg + (kr - 1)
    assert da.min() >= 0 and da.max() <= 2 * kr - 2
    oh_a = (np.arange(2 * kr - 1)[:, None, None] == da[None]).astype(np.float32)
    t1 = jnp.einsum('hab,bck->hack', rpb.astype(F32), oh_c, precision=lax.Precision.HIGHEST)
    bias = jnp.einsum('hack,arl->hrclk', t1, oh_a, precision=lax.Precision.HIGHEST)
    bias = bias.reshape(n_heads, rg * w, 3 * rg * w)
    c0 = np.clip(c - kc // 2, 0, w - kc)
    col_ok = (c[None, :] >= c0[:, None]) & (c[None, :] < c0[:, None] + kc)
    masks = []
    for g in (0, 1, n_steps - 1):
        r = rg * g + rl
        r0 = np.clip(r - kr // 2, 0, rows - kr)
        key_row = rg * (g - 1) + krl
        row_ok = (key_row[None, :] >= r0[:, None]) & (key_row[None, :] < r0[:, None] + kr)
        ok = row_ok[:, None, :, None] & col_ok[None, :, None, :]
        masks.append(ok.reshape(rg * w, 3 * rg * w))
    masks = np.stack(masks)
    return jnp.where(masks[:, None], bias[None] * LOG2E, NEG_INF)


def _na_kernel(q_ref, kp_ref, kc_ref, kn_ref, vp_ref, vc_ref, vn_ref, kx_ref, vx_ref, b_ref, *rest,
               heads, scale, n_side):
    side_in, o_ref, side_out = rest[:n_side], rest[n_side], rest[n_side + 1:]
    _SideCast.run(side_in, side_out)
    tq = q_ref.shape[0]
    for hh in range(heads):
        cols = slice(hh * HEAD_DIM, (hh + 1) * HEAD_DIM)
        q = (q_ref[:, cols].astype(F32) * scale).astype(BF16)
        ss = []
        for idx, k_ref in enumerate((kp_ref, kc_ref, kn_ref)):
            s = lax.dot_general(q, k_ref[:, cols], _NT, preferred_element_type=F32)
            ss.append(s + b_ref[0, hh, :, idx * tq:(idx + 1) * tq])
        ss.append(lax.dot_general(q, kx_ref[:, cols], _NT, preferred_element_type=F32))
        m = functools.reduce(jnp.maximum, [jnp.max(s, -1, keepdims=True) for s in ss])
        ps = [jnp.exp2(s - m) for s in ss]
        l = functools.reduce(jnp.add, [jnp.sum(p, -1, keepdims=True) for p in ps])
        acc = None
        for p, v_ref in zip(ps, (vp_ref, vc_ref, vn_ref, vx_ref)):
            pv = jnp.dot(p.astype(BF16), v_ref[:, cols], preferred_element_type=F32)
            acc = pv if acc is None else acc + pv
        o_ref[:, cols] = (acc / l).astype(o_ref.dtype)


def _neighbourhood_attention(p, p_ctx, bias, n_heads, q_col0, k_col0, v_col0, cast_tables=()):
    s_len = p.shape[0]
    tq = NA_ROWS_PER_STEP * GRID_W
    n_steps = s_len // tq
    hb = 4 if n_heads % 4 == 0 else 1
    bw = hb * HEAD_DIM
    assert p_ctx.shape[0] == tq
    qc, kc, vc = q_col0 // bw, k_col0 // bw, v_col0 // bw
    prev = lambda g: jnp.maximum(g - 1, 0)
    nxt = lambda g: jnp.minimum(g + 1, n_steps - 1)
    variant = lambda g: jnp.where(g == 0, 0, jnp.where(g == n_steps - 1, 2, 1))
    lat = lambda col, row: pl.BlockSpec((tq, bw), lambda h, g: (row(g), col + h))
    ident = lambda g: g
    side = _SideCast(cast_tables, (n_heads // hb) * n_steps, lambda h, g: h * n_steps + g)
    res = pl.pallas_call(
        functools.partial(_na_kernel, heads=hb, scale=HEAD_DIM ** -0.5 * LOG2E, n_side=side.n),
        grid=(n_heads // hb, n_steps),
        in_specs=[lat(qc, ident),
                  lat(kc, prev), lat(kc, ident), lat(kc, nxt),
                  lat(vc, prev), lat(vc, ident), lat(vc, nxt),
                  pl.BlockSpec((tq, bw), lambda h, g: (0, kc + h)),
                  pl.BlockSpec((tq, bw), lambda h, g: (0, vc + h)),
                  pl.BlockSpec((1, hb, tq, 3 * tq), lambda h, g: (variant(g), h, 0, 0))] + side.in_specs,
        out_specs=[pl.BlockSpec((tq, bw), lambda h, g: (g, h))] + side.out_specs,
        out_shape=[jax.ShapeDtypeStruct((s_len, n_heads * HEAD_DIM), BF16)] + side.out_shapes,
        compiler_params=_params("parallel", "arbitrary"),
    )(p, p, p, p, p, p, p, p_ctx, p_ctx, bias, *side.tables)
    return res if side.n else res[0]


def _swa_kernel(sink_ref, q_ref, kp_ref, kc_ref, kn_ref, vp_ref, vc_ref, vn_ref, kx_ref, vx_ref, mask_ref, o_ref,
                *, group):
    h = pl.program_id(0)
    tq = q_ref.shape[0]
    q = jnp.concatenate([q_ref[:, g * HEAD_DIM:(g + 1) * HEAD_DIM] for g in range(group)], 0)
    ss = []
    for idx, k_ref in enumerate((kp_ref, kc_ref, kn_ref)):
        s = lax.dot_general(q, k_ref[...], _NT, preferred_element_type=F32)
        ss.append(s + mask_ref[0, idx])
    ss.append(lax.dot_general(q, kx_ref[...], _NT, preferred_element_type=F32))
    row_g = lax.broadcasted_iota(jnp.int32, (group * tq, 1), 0) // tq
    sink = jnp.zeros((group * tq, 1), F32)
    for g in range(group):
        sink = jnp.where(row_g == g, sink_ref[h * group + g] * LOG2E, sink)
    m = functools.reduce(jnp.maximum, [jnp.max(s, -1, keepdims=True) for s in ss] + [sink])
    ps = [jnp.exp2(s - m) for s in ss]
    l = functools.reduce(jnp.add, [jnp.sum(p, -1, keepdims=True) for p in ps]) + jnp.exp2(sink - m)
    acc = None
    for p, v_ref in zip(ps, (vp_ref, vc_ref, vn_ref, vx_ref)):
        pv = jnp.dot(p.astype(BF16), v_ref[...], preferred_element_type=F32)
        acc = pv if acc is None else acc + pv
    out = acc / l
    for g in range(group):
        o_ref[:, g * HEAD_DIM:(g + 1) * HEAD_DIM] = out[g * tq:(g + 1) * tq].astype(o_ref.dtype)


def _sliding_window_attention(q, k, v, v_col0, kx, kx_col0, vx, vx_col0, sink, *, kv_heads, group):
    s_len = q.shape[0]
    tq = SW_BLOCK
    assert s_len % tq == 0 and tq >= SW_WINDOW and kx.shape[0] == vx.shape[0]
    n_blk = s_len // tq
    n_ctx = kx.shape[0]
    qw = group * HEAD_DIM
    vc, kxc, vxc = v_col0 // HEAD_DIM, kx_col0 // HEAD_DIM, vx_col0 // HEAD_DIM
    prev = lambda n: jnp.maximum(n - 1, 0)
    nxt = lambda n: jnp.minimum(n + 1, n_blk - 1)
    ident = lambda n: n
    blk = lambda col, row: pl.BlockSpec((tq, HEAD_DIM), lambda h, n: (row(n), col + h))
    assert n_blk >= 3
    ql = np.arange(group * tq)[:, None] % tq
    kl = np.arange(tq)[None, :]
    diff = kl - ql
    ok = np.stack([diff - tq >= -SW_WINDOW, np.abs(diff) <= SW_WINDOW, diff + tq <= SW_WINDOW])
    none = np.zeros_like(ok[0])
    variants = np.stack([np.stack([none, ok[1], ok[2]]), ok, np.stack([ok[0], ok[1], none])])
    mask = jnp.where(variants, 0.0, NEG_INF).astype(F32)
    variant = lambda n: jnp.where(n == 0, 0, jnp.where(n == n_blk - 1, 2, 1))
    return pl.pallas_call(
        functools.partial(_swa_kernel, group=group),
        grid=(kv_heads, n_blk),
        in_specs=[pl.BlockSpec(memory_space=pltpu.SMEM),
                  pl.BlockSpec((tq, qw), lambda h, n: (n, h)),
                  blk(0, prev), blk(0, ident), blk(0, nxt),
                  blk(vc, prev), blk(vc, ident), blk(vc, nxt),
                  pl.BlockSpec((n_ctx, HEAD_DIM), lambda h, n: (0, kxc + h)),
                  pl.BlockSpec((n_ctx, HEAD_DIM), lambda h, n: (0, vxc + h)),
                  pl.BlockSpec((1, 3, group * tq, tq), lambda h, n: (variant(n), 0, 0, 0))],
        out_specs=pl.BlockSpec((tq, qw), lambda h, n: (n, h)),
        out_shape=jax.ShapeDtypeStruct((s_len, kv_heads * qw), BF16),
        compiler_params=_params("parallel", "arbitrary"),
    )(sink.astype(F32), q, k, k, k, v, v, v, kx, vx, mask)


def _ret_kernel(cd_ref, qf_ref, kf_ref, vf_ref, qb_ref, kb_ref, vb_ref, kx_ref, vx_ref,
                intra_ref, qd_ref, kd_ref, wx_ref, of_ref, ob_ref, st_ref, *, heads):
    hb, n = pl.program_id(0), pl.program_id(1)

    @pl.when(n == 0)
    def _():
        for hh in range(heads):
            cols = slice(hh * HEAD_DIM, (hh + 1) * HEAD_DIM)
            for d in range(2):
                kw = (kx_ref[:, cols].astype(F32) * wx_ref[d, hh]).astype(BF16)
                st_ref[d, hh] = lax.dot_general(kw, vx_ref[:, cols], _TN, preferred_element_type=F32)

    for hh in range(heads):
        cols = slice(hh * HEAD_DIM, (hh + 1) * HEAD_DIM)
        head = hb * heads + hh
        for d, (q_ref, k_ref, v_ref, o_ref) in enumerate(((qf_ref, kf_ref, vf_ref, of_ref),
                                                          (qb_ref, kb_ref, vb_ref, ob_ref))):
            q, k, v = q_ref[:, cols], k_ref[:, cols], v_ref[:, cols]
            state = st_ref[d, hh]
            inner = lax.dot_general(q, k, _NT, preferred_element_type=F32) * intra_ref[d, hh]
            qs = (q.astype(F32) * qd_ref[d, hh]).astype(BF16)
            o_ref[:, cols] = (jnp.dot(inner.astype(BF16), v, preferred_element_type=F32)
                              + jnp.dot(qs, state.astype(BF16), preferred_element_type=F32))
            ks = (k.astype(F32) * kd_ref[d, hh]).astype(BF16)
            st_ref[d, hh] = state * cd_ref[d, head] + lax.dot_general(ks, v, _TN, preferred_element_type=F32)


def _retention(rq, rk, p, v_col0, rkx, p_ctx, log_gamma, n_heads):
    s_len = rq.shape[0]
    c = RET_CHUNK
    n_chunks = s_len // c
    n_ctx = rkx.shape[0]
    hb = 8 if n_heads % 8 == 0 else (4 if n_heads % 4 == 0 else 1)
    bw = hb * HEAD_DIM
    assert v_col0 % bw == 0
    vc = v_col0 // bw
    lg = log_gamma.astype(F32)
    pos = jnp.arange(c, dtype=F32)
    diff = pos[:, None] - pos[None, :]
    intra_f = jnp.where(diff >= 0, jnp.exp(lg[0][:, None, None] * jnp.maximum(diff, 0.0)), 0.0)
    intra_b = jnp.where(diff <= 0, jnp.exp(lg[1][:, None, None] * jnp.maximum(-diff, 0.0)), 0.0)
    intra = jnp.stack([intra_f, intra_b])
    lanes = lambda t: jnp.broadcast_to(t[..., None], t.shape + (HEAD_DIM,))
    qd = lanes(jnp.stack([jnp.exp(lg[0][:, None] * (pos + 1.0)), jnp.exp(lg[1][:, None] * (c - pos))]))
    kd = lanes(jnp.stack([jnp.exp(lg[0][:, None] * (c - 1.0 - pos)), jnp.exp(lg[1][:, None] * pos)]))
    cd = jnp.exp(lg * c)
    jx = jnp.arange(n_ctx, dtype=F32)
    wx = lanes(jnp.stack([jnp.exp(lg[0][:, None] * (n_ctx - 1.0 - jx)), jnp.exp(lg[1][:, None] * jx)]))
    fwd = lambda col: pl.BlockSpec((c, bw), lambda h, n: (n, col + h))
    bwd = lambda col: pl.BlockSpec((c, bw), lambda h, n: (n_chunks - 1 - n, col + h))
    tab = lambda rows: pl.BlockSpec((2, hb, rows, HEAD_DIM), lambda h, n: (0, h, 0, 0))
    out_sds = jax.ShapeDtypeStruct((s_len, n_heads * HEAD_DIM), F32)
    return pl.pallas_call(
        functools.partial(_ret_kernel, heads=hb),
        grid=(n_heads // hb, n_chunks),
        in_specs=[pl.BlockSpec(memory_space=pltpu.SMEM),
                  fwd(0), fwd(0), fwd(vc), bwd(0), bwd(0), bwd(vc),
                  pl.BlockSpec((n_ctx, bw), lambda h, n: (0, h)),
                  pl.BlockSpec((n_ctx, bw), lambda h, n: (0, vc + h)),
                  tab(c), tab(c), tab(c), tab(n_ctx)],
        out_specs=[fwd(0), bwd(0)],
        out_shape=[out_sds, out_sds],
        scratch_shapes=[pltpu.VMEM((2, hb, HEAD_DIM, HEAD_DIM), F32)],
        compiler_params=_params("parallel", "arbitrary"),
    )(cd, rq, rk, p, rq, rk, p, rkx, p_ctx, intra, qd, kd, wx)


def _ret_out_kernel(of_ref, ob_ref, g_ref, o_ref, *, heads):
    for hh in range(heads):
        cols = slice(hh * HEAD_DIM, (hh + 1) * HEAD_DIM)
        o = of_ref[:, cols] + ob_ref[:, cols]
        mu = jnp.mean(o, -1, keepdims=True)
        var = jnp.mean(jnp.square(o - mu), -1, keepdims=True)
        gate = g_ref[:, cols].astype(F32)
        gate = gate * (1.0 / (1.0 + jnp.exp(-gate)))
        o_ref[:, cols] = (gate * ((o - mu) * lax.rsqrt(var + LN_EPS))).astype(o_ref.dtype)


def _retention_output(o_f, o_b, p, gate_col0, n_heads):
    s_len = o_f.shape[0]
    hb = 4 if n_heads % 4 == 0 else 1
    bw = hb * HEAD_DIM
    gc = gate_col0 // bw
    tt = min(s_len, 512)
    spec = pl.BlockSpec((tt, bw), lambda i, j: (i, j))
    return pl.pallas_call(
        functools.partial(_ret_out_kernel, heads=hb),
        grid=(s_len // tt, n_heads // hb),
        in_specs=[spec, spec, pl.BlockSpec((tt, bw), lambda i, j: (i, gc + j))],
        out_specs=spec,
        out_shape=jax.ShapeDtypeStruct((s_len, n_heads * HEAD_DIM), BF16),
        compiler_params=_params("parallel", "parallel"),
    )(o_f, o_b, p)


def _residual_ln(x, y, gate, g, b, alpha):
    h = alpha * x + gate * y
    mu = jnp.mean(h, -1, keepdims=True)
    var = jnp.mean(jnp.square(h - mu), -1, keepdims=True)
    return (h - mu) * lax.rsqrt(var + LN_EPS) * g + b


def _outproj_kernel(ya_ref, yb_ref, w_ref, x_ref, gate_ref, g_ref, b_ref, o_ref, *, alpha):
    wa = ya_ref.shape[1]
    o_ref[...] = (jnp.dot(ya_ref[...], w_ref[:wa, :], preferred_element_type=F32)
                  + jnp.dot(yb_ref[...], w_ref[wa:, :], preferred_element_type=F32))
    for rows in _row_chunks(x_ref.shape[0], 64):
        o_ref[rows, :] = _residual_ln(x_ref[rows, :], o_ref[rows, :], gate_ref[...], g_ref[...], b_ref[...], alpha)


def _outproj_ln(ya, yb, w, x, gate, g, b, alpha, tm):
    m, wa = ya.shape
    wb = yb.shape[1]
    d = w.shape[1]
    tm = min(tm, m)
    vec = pl.BlockSpec((1, d), lambda i: (0, 0))
    return pl.pallas_call(
        functools.partial(_outproj_kernel, alpha=alpha),
        grid=(m // tm,),
        in_specs=[pl.BlockSpec((tm, wa), lambda i: (i, 0)),
                  pl.BlockSpec((tm, wb), lambda i: (i, 0)),
                  pl.BlockSpec((wa + wb, d), lambda i: (0, 0), pipeline_mode=pl.Buffered(1)),
                  pl.BlockSpec((tm, d), lambda i: (i, 0)),
                  vec, vec, vec],
        out_specs=pl.BlockSpec((tm, d), lambda i: (i, 0)),
        out_shape=jax.ShapeDtypeStruct((m, d), F32),
        compiler_params=_params("parallel", vmem=VMEM_LIMIT_RESIDENT_BYTES),
    )(ya, yb, w, x, gate, g.reshape(1, d), b.reshape(1, d))


def _add_ln_kernel(x_ref, f_ref, gate_ref, g_ref, b_ref, o_ref, *, alpha):
    o_ref[...] = _residual_ln(x_ref[...], f_ref[...].astype(F32), gate_ref[...], g_ref[...], b_ref[...], alpha)


def _add_ln(x, f, gate, g, b, alpha):
    m, d = x.shape
    tt = min(m, 256)
    row = pl.BlockSpec((tt, d), lambda i: (i, 0))
    vec = pl.BlockSpec((1, d), lambda i: (0, 0))
    return pl.pallas_call(
        functools.partial(_add_ln_kernel, alpha=alpha),
        grid=(m // tt,),
        in_specs=[row, row, vec, vec, vec],
        out_specs=row,
        out_shape=jax.ShapeDtypeStruct((m, d), F32),
        compiler_params=_params("parallel"),
    )(x, f, gate, g.reshape(1, d), b.reshape(1, d))


PEER_LIST = 24


def _peer_gate_kernel(q_ref, keys_ref, d1_ref, f1_ref, s2_ref, e2_ref, a_ref, b_ref, cand_ref):
    half = keys_ref.shape[-1]
    k = PEER_TOPK
    s1 = lax.dot_general(keys_ref[0, 0], q_ref[:, :half].astype(BF16), _NT, preferred_element_type=F32)
    s2 = lax.dot_general(keys_ref[0, 1], q_ref[:, half:].astype(BF16), _NT, preferred_element_type=F32)

    def top_rows(s, dst_ref):
        dst_ref[...] = jnp.full_like(dst_ref, -jnp.inf)
        n_slabs = s.shape[0] // 8
        if n_slabs & (n_slabs - 1):
            cur = s
            for r in range(k + 1):
                m = jnp.max(cur, 0, keepdims=True)
                dst_ref[r:r + 1, :] = m
                cur = jnp.where(cur == m, -jnp.inf, cur)
            return
        v = [s[8 * i:8 * i + 8, :] for i in range(n_slabs)]
        size = 2
        while size <= n_slabs:
            stride = size // 2
            while stride >= 1:
                for i in range(n_slabs):
                    j = i ^ stride
                    if j > i:
                        hi, lo = jnp.maximum(v[i], v[j]), jnp.minimum(v[i], v[j])
                        v[i], v[j] = (hi, lo) if (i & size) == 0 else (lo, hi)
                stride //= 2
            size *= 2
        depth = min(n_slabs, k + 1)
        v = v[:depth] + [jnp.full_like(v[0], -jnp.inf)]
        for r in range(k + 1):
            m = jnp.max(v[0], 0, keepdims=True)
            dst_ref[r:r + 1, :] = m
            win = v[0] == m
            for i in range(depth):
                v[i] = jnp.where(win, v[i + 1], v[i])

    top_rows(s1, a_ref)
    top_rows(s2, b_ref)
    half_k = k // 2
    cand_ref[0:PEER_LIST, :] = a_ref[0:1, :] + b_ref[...]
    for r in range(1, half_k):
        cand_ref[PEER_LIST + (r - 1) * half_k:PEER_LIST + r * half_k, :] = a_ref[r:r + 1, :] + b_ref[0:half_k, :]
    cand_ref[PEER_LIST + (half_k - 1) * half_k:, :] = a_ref[half_k:, :] + b_ref[0:1, :]
    best = a_ref[0:1, :] + b_ref[0:1, :]
    cur = cand_ref[...]
    z = jnp.zeros_like(best)
    kth = best
    for r in range(k):
        kth = jnp.max(cur, 0, keepdims=True)
        z = z + jnp.exp(kth - best)
        cur = jnp.where(cur == kth, -jnp.inf, cur)
    tau = 0.5 * (kth + jnp.max(cur, 0, keepdims=True))
    d1_ref[0] = tau - s1
    f1_ref[0] = jnp.exp(s1 - a_ref[0:1, :]) / z
    s2_ref[0] = s2
    e2_ref[0] = jnp.exp(s2 - b_ref[0:1, :])


def _peer_gates(q, sub_keys, tm):
    t = q.shape[0]
    n_heads, _, n_keys, half = sub_keys.shape
    tm = min(tm, t)
    big = jax.ShapeDtypeStruct((n_heads, n_keys, t), F32)
    blk = pl.BlockSpec((1, n_keys, tm), lambda i, h: (h, 0, i))
    return pl.pallas_call(
        _peer_gate_kernel,
        grid=(t // tm, n_heads),
        in_specs=[pl.BlockSpec((tm, 2 * half), lambda i, h: (i, h)),
                  pl.BlockSpec((1, 2, n_keys, half), lambda i, h: (h, 0, 0, 0))],
        out_specs=[blk, blk, blk, blk],
        out_shape=[big, big, big, big],
        scratch_shapes=[pltpu.VMEM((PEER_LIST, tm), F32), pltpu.VMEM((PEER_LIST, tm), F32),
                        pltpu.VMEM((2 * PEER_LIST - PEER_TOPK // 2 + (PEER_TOPK // 2 - 1) * (PEER_TOPK // 2), tm),
                                   F32)],
        compiler_params=_params("parallel", "parallel"),
    )(q, sub_keys)


def _gelu_tanh(x):
    c = math.sqrt(2.0 / math.pi)
    hx = 0.5 * x
    return hx + hx * jnp.tanh(x * (c + (c * 0.044715) * (x * x)))


def _peer_kernel(h_ref, u_ref, v_ref, d1_ref, f1_ref, s2_ref, e2_ref, *rest, n_side):
    side_in, o_ref, side_out = rest[:n_side], rest[n_side], rest[n_side + 1:2 * n_side + 1]
    acc_ref, g_ref = rest[2 * n_side + 1:]
    _SideCast.run(side_in, side_out)
    cj, ci = pl.program_id(1), pl.program_id(2)
    n_heads, ni, tm = d1_ref.shape
    nj = s2_ref.shape[1]
    d = h_ref.shape[1]

    @pl.when((cj == 0) & (ci == 0))
    def _():
        acc_ref[...] = jnp.zeros_like(acc_ref)

    rows = 32
    for ii in range(ni):
        for r0 in range(0, nj, rows):
            for t0 in range(0, tm, 128):
                tok = slice(t0, t0 + 128)
                g = None
                for h in range(n_heads):
                    w = f1_ref[h, ii:ii + 1, tok] * e2_ref[h, r0:r0 + rows, tok]
                    sel = jnp.where(s2_ref[h, r0:r0 + rows, tok] >= d1_ref[h, ii:ii + 1, tok], w, 0.0)
                    g = sel if g is None else g + sel
                g_ref[ii * nj + r0:ii * nj + r0 + rows, tok] = g
    u = u_ref[...].reshape(ni * nj, d)
    act = _gelu_tanh(lax.dot_general(u, h_ref[...], _NT, preferred_element_type=F32))
    wt = (g_ref[...] * act).astype(BF16)
    acc_ref[...] += lax.dot_general(wt, v_ref[...].reshape(ni * nj, d), _TN, preferred_element_type=F32)

    @pl.when((cj == pl.num_programs(1) - 1) & (ci == pl.num_programs(2) - 1))
    def _():
        o_ref[...] = acc_ref[...].astype(o_ref.dtype)


def _peer_dense(hf, u3, v3, d1, f1, s2, e2, tm, cast_tables=()):
    t, d = hf.shape
    n_heads, n_keys = d1.shape[0], d1.shape[1]
    tm = min(tm, t)
    ni, nj = PEER_NI, PEER_NJ
    n_cj, n_ci = n_keys // nj, n_keys // ni
    tab = pl.BlockSpec((ni, nj, d), lambda i, cj, ci: (ci, cj, 0))
    first = pl.BlockSpec((n_heads, ni, tm), lambda i, cj, ci: (0, ci, i))
    second = pl.BlockSpec((n_heads, nj, tm), lambda i, cj, ci: (0, cj, i))
    side = _SideCast(cast_tables, (t // tm) * n_cj * n_ci, lambda i, cj, ci: (i * n_cj + cj) * n_ci + ci)
    res = pl.pallas_call(
        functools.partial(_peer_kernel, n_side=side.n),
        grid=(t // tm, n_cj, n_ci),
        in_specs=[pl.BlockSpec((tm, d), lambda i, cj, ci: (i, 0)), tab, tab, first, first, second, second]
        + side.in_specs,
        out_specs=[pl.BlockSpec((tm, d), lambda i, cj, ci: (i, 0))] + side.out_specs,
        out_shape=[jax.ShapeDtypeStruct((t, d), BF16)] + side.out_shapes,
        scratch_shapes=[pltpu.VMEM((tm, d), F32), pltpu.VMEM((ni * nj, tm), F32)],
        compiler_params=_params("parallel", "arbitrary", "arbitrary"),
    )(hf, u3, v3, d1, f1, s2, e2, *side.tables)
    return res if side.n else res[0]


def _peer_block(x, shift, scale, gate, w_q, sub_keys, u3, v3, g, b, alpha, tm, cast_tables=()):
    q, hf = _proj(x, shift, scale, w_q, F32, tm, emit_h=True, resident_w=True)
    d1, f1, s2, e2 = _peer_gates(q, sub_keys, tm)
    res = _peer_dense(hf, u3, v3, d1, f1, s2, e2, tm, cast_tables)
    f, casts = (res[0], tuple(res[1:])) if cast_tables else (res, ())
    return _add_ln(x, f, gate, g, b, alpha), casts


def kernel(x, c, ctx, c_ctx, ada_w, ada_b, ln_g, ln_b, ev_w_in, ev_w_out, ev_na_rpb, ev_ga_q_gain, ev_ga_k_gain,
           od_w_in, od_w_out, od_ret_decay_exp, od_sw_sink, peer_w_q, peer_sub_keys, peer_u, peer_v):
    assert x.shape[0] == 1 and ada_w.shape[0] == 2
    depth = ada_w.shape[0]
    _, s_len, d = x.shape
    hd = HEAD_DIM
    mix_heads = d // (2 * hd)
    kv_heads = mix_heads // 4
    group = mix_heads // kv_heads
    alpha = (2 * depth) ** 0.25
    att_scale = hd ** -0.5
    sm_scale = att_scale * LOG2E
    n_keys = peer_sub_keys.shape[3]
    tm = 512
    tm_out = 256

    xs, cs = x[0], ctx[0]
    mods = _ada_modulation(c, c_ctx, ada_w, ada_b)
    cos_i, sin_s = _rope_tables(s_len)
    ones = jnp.ones((hd,), F32)

    def mod6(layer, which):
        return [m.reshape(1, d) for m in jnp.split(mods[layer, which], 6)]

    def peer_args(layer, wq_bf16, u_bf16, v_bf16):
        return (wq_bf16, peer_sub_keys[layer].astype(BF16),
                u_bf16.reshape(n_keys, n_keys, d), v_bf16.reshape(n_keys, n_keys, d),
                ln_g[layer, 1], ln_b[layer, 1], alpha)

    later_weights = ((ev_w_out, 0), (od_w_in, 0), (od_w_out, 0), (peer_w_q, 0), (peer_w_q, 1))

    sh_a, sc_a, g_a, sh_f, sc_f, g_f = mod6(0, 0)
    csh_a, csc_a, cg_a, csh_f, csc_f, cg_f = mod6(0, 1)
    w_in = _to_bf16(ev_w_in, 0)
    p = _proj(xs, sh_a, sc_a, w_in, BF16, tm)
    pc = _proj(cs, csh_a, csc_a, w_in, BF16, tm)
    mh = mix_heads * hd
    kvw = kv_heads * hd
    na_q, na_k, na_v, ga_q, ga_k, ga_v = 0, mh, 2 * mh, 3 * mh, 4 * mh, 4 * mh + kvw
    qg = _prep(p, ga_q, mix_heads, ev_ga_q_gain[0], cos_i, sin_s, norm=True, rope=True, scale=sm_scale)
    kg = _prep(p, ga_k, kv_heads, ev_ga_k_gain[0], cos_i, sin_s, norm=True, rope=True)
    qgc = _prep(pc, ga_q, mix_heads, ev_ga_q_gain[0], cos_i, sin_s, norm=True, scale=sm_scale)
    kgc = _prep(pc, ga_k, kv_heads, ev_ga_k_gain[0], cos_i, sin_s, norm=True)
    k_all = jnp.concatenate([kg, kgc], 0)
    v_all = jnp.concatenate([p[:, ga_v:ga_v + kvw], pc[:, ga_v:ga_v + kvw]], 0)
    tk = _pick(k_all.shape[0], 1024, 256)
    y_ga, u0, v0 = _flash(qg, 0, k_all, 0, v_all, 0, kv_heads=kv_heads, group=group, tq=512, tk=tk,
                          cast_tables=((peer_u, 0), (peer_v, 0)))
    yc_ga = _flash(qgc, 0, kgc, 0, pc, ga_v, kv_heads=kv_heads, group=group, tq=256, tk=256)
    bias = _na_bias_tables(ev_na_rpb[0], s_len // GRID_W)
    y_na, w_out, w_in1, w_out1, wq0, wq1 = _neighbourhood_attention(p, pc, bias, mix_heads, na_q, na_k, na_v,
                                                                    cast_tables=later_weights)
    yc_na = _flash(pc, na_q, pc, na_k, pc, na_v, kv_heads=mix_heads, group=1, tq=256, tk=256, scale=sm_scale)
    xs = _outproj_ln(y_na, y_ga, w_out, xs, g_a, ln_g[0, 0], ln_b[0, 0], alpha, tm_out)
    cs = _outproj_ln(yc_na, yc_ga, w_out, cs, cg_a, ln_g[0, 0], ln_b[0, 0], alpha, tm_out)
    pa = peer_args(0, wq0, u0, v0)
    xs, (u1, v1) = _peer_block(xs, sh_f, sc_f, g_f, *pa, tm, cast_tables=((peer_u, 1), (peer_v, 1)))
    cs, _ = _peer_block(cs, csh_f, csc_f, cg_f, *pa, tm)

    sh_a, sc_a, g_a, sh_f, sc_f, g_f = mod6(1, 0)
    csh_a, csc_a = mod6(1, 1)[:2]
    p = _proj(xs, sh_a, sc_a, w_in1, BF16, tm)
    pc = _proj(cs, csh_a, csc_a, w_in1, BF16, tm)
    r_q, r_k, r_v, r_g, s_q, s_k, s_v = 0, mh, 2 * mh, 3 * mh, 4 * mh, 5 * mh, 5 * mh + kvw
    rq = _prep(p, r_q, mix_heads, ones, cos_i, sin_s, rope=True)
    rk = _prep(p, r_k, mix_heads, ones, cos_i, sin_s, rope=True, scale=att_scale)
    rkc = _prep(pc, r_k, mix_heads, ones, cos_i, sin_s, scale=att_scale)
    sq = _prep(p, s_q, mix_heads, ones, cos_i, sin_s, rope=True, scale=sm_scale)
    sk = _prep(p, s_k, kv_heads, ones, cos_i, sin_s, rope=True)
    log_gamma = jnp.log1p(-jnp.exp2(-od_ret_decay_exp[0].astype(F32)))
    o_f, o_b = _retention(rq, rk, p, r_v, rkc, pc, log_gamma, mix_heads)
    y_ret = _retention_output(o_f, o_b, p, r_g, mix_heads)
    y_sw = _sliding_window_attention(sq, sk, p, s_v, pc, s_k, pc, s_v, od_sw_sink[0],
                                     kv_heads=kv_heads, group=group)
    xs = _outproj_ln(y_ret, y_sw, w_out1, xs, g_a, ln_g[1, 0], ln_b[1, 0], alpha, tm_out)
    xs, _ = _peer_block(xs, sh_f, sc_f, g_f, *peer_args(1, wq1, u1, v1), tm)
    return xs[None]
```
